```python
import math
import jax
import jax.numpy as jnp
from jax import lax
import numpy as np

D_MODEL = 1024
BATCH = 4
SEQ = 8192
DEPTH = 2

NORM_EPS = 1e-6
N_MOD = 6
ADA_SCALE = 0.02

ATT_HEADS = 8
HEAD_DIM = 64
ATT_WIDTH = ATT_HEADS * HEAD_DIM
MOBA_BLOCK = 256
MOBA_TOPK = 3
MOBA_CHUNK = 128
CONV_CH = D_MODEL // 2
CONV_WIDTH = 31
HY_IN = 3 * ATT_WIDTH + 2 * CONV_CH
HY_MIX = ATT_WIDTH + CONV_CH

SSM_INNER = 2 * D_MODEL
SSM_HEAD_DIM = 64
SSM_HEADS = SSM_INNER // SSM_HEAD_DIM
SSM_GROUPS = 4
SSM_STATE = 128
SSM_CONV = 4
SSM_CHUNK = 256
SSM_CONV_DIM = SSM_INNER + 2 * SSM_GROUPS * SSM_STATE
SSM_IN = SSM_INNER + SSM_CONV_DIM + SSM_HEADS
DT_MIN = 1e-3
DT_MAX = 1e-1

N_EXPERTS = 16
N_EXPERT_GROUPS = 4
EXPERTS_PER_GROUP = N_EXPERTS // N_EXPERT_GROUPS
TOP_K = 2
EXPERT_FF = 512
MOE_CHUNK = 256

N_EVEN = (DEPTH + 1) // 2
N_ODD = DEPTH // 2

kernel_name = 'hybrid_moba_conformer_ssd_moe'


def _rmsnorm(x, g):
    xf = x.astype(jnp.float32)
    y = xf * lax.rsqrt(jnp.mean(xf * xf, axis=-1, keepdims=True) + NORM_EPS)
    return (y * g.astype(jnp.float32)).astype(x.dtype)


def _pad_seq(t, s_pad):
    pad = [(0, 0)] * t.ndim
    pad[1] = (0, s_pad - t.shape[1])
    return jnp.pad(t, pad)


def _causal_depthwise_conv(u, w, b):
    width, ch = w.shape
    y = lax.conv_general_dilated(u, w[:, None, :].astype(u.dtype), window_strides=(1,),
                                 padding=[(width - 1, 0)],
                                 dimension_numbers=('NWC', 'WIO', 'NWC'),
                                 feature_group_count=ch)
    return y + b.astype(u.dtype)


def _moba_head(q, k, v):
    s_pad = q.shape[0]
    nb = s_pad // MOBA_BLOCK
    k_eff = min(MOBA_TOPK, nb)
    n_pairs = s_pad * k_eff
    n_slots = n_pairs + nb * MOBA_CHUNK
    n_chunks = n_slots // MOBA_CHUNK
    scale = HEAD_DIM ** -0.5
    qb = q.reshape(nb, MOBA_BLOCK, HEAD_DIM)
    kb = k.reshape(nb, MOBA_BLOCK, HEAD_DIM)
    vb = v.reshape(nb, MOBA_BLOCK, HEAD_DIM)
    causal = jnp.tril(jnp.ones((MOBA_BLOCK, MOBA_BLOCK), bool))
    s_own = jnp.where(causal, jnp.einsum('nqd,nkd->nqk', qb, kb) * scale, -jnp.inf)
    lse_own = jax.nn.logsumexp(s_own, axis=-1)
    o_own = jnp.einsum('nqk,nkd->nqd', jnp.exp(s_own - lse_own[..., None]), vb)
    q_blk = jnp.arange(s_pad) // MOBA_BLOCK
    gate = q @ kb.mean(axis=1).T
    gate = jnp.where(jnp.arange(nb)[None, :] < q_blk[:, None], gate, -jnp.inf)
    _, sel = lax.top_k(gate, k_eff)
    valid = jnp.arange(k_eff)[None, :] < q_blk[:, None]
    pair_blk = jnp.where(valid, sel, nb).reshape(-1)
    pair_q = jnp.repeat(jnp.arange(s_pad), k_eff)
    order = jnp.argsort(pair_blk)
    s_blk = pair_blk[order]
    counts = jax.ops.segment_sum(jnp.ones((n_pairs,), jnp.int32), pair_blk, num_segments=nb + 1)
    start = jnp.cumsum(counts) - counts
    padded = (counts[:nb] + MOBA_CHUNK - 1) // MOBA_CHUNK * MOBA_CHUNK
    padded_end = jnp.cumsum(padded)
    padded_start = jnp.concatenate([jnp.zeros((1,), padded_end.dtype), padded_end])
    dest = padded_start[s_blk] + jnp.arange(n_pairs) - start[s_blk]
    dest = jnp.where(s_blk < nb, dest, n_slots)
    slot_q = jnp.zeros((n_slots,), jnp.int32).at[dest].set(pair_q[order].astype(jnp.int32), mode='drop')
    slot_pair = jnp.full((n_slots,), n_pairs, jnp.int32).at[dest].set(order.astype(jnp.int32), mode='drop')
    chunk_blk = jnp.minimum(jnp.searchsorted(padded_end, jnp.arange(n_chunks) * MOBA_CHUNK, side='right'), nb - 1)
    qc = q[slot_q].reshape(n_chunks, MOBA_CHUNK, HEAD_DIM)
    s_c = jnp.einsum('ncd,nkd->nck', qc, kb[chunk_blk]) * scale
    lse_c = jax.nn.logsumexp(s_c, axis=-1)
    o_c = jnp.einsum('nck,nkd->ncd', jnp.exp(s_c - lse_c[..., None]), vb[chunk_blk])
    o_sel = jnp.zeros((n_pairs, HEAD_DIM), q.dtype).at[slot_pair].set(o_c.reshape(n_slots, HEAD_DIM), mode='drop')
    lse_sel = jnp.full((n_pairs,), -jnp.inf, q.dtype).at[slot_pair].set(lse_c.reshape(n_slots), mode='drop')
    lse_all = jnp.concatenate([lse_sel.reshape(s_pad, k_eff), lse_own.reshape(s_pad, 1)], axis=1)
    o_all = jnp.concatenate([o_sel.reshape(s_pad, k_eff, HEAD_DIM), o_own.reshape(s_pad, 1, HEAD_DIM)], axis=1)
    return jnp.einsum('sr,srd->sd', jax.nn.softmax(lse_all, axis=-1), o_all)


def _moba_attention(q, k, v):
    s = q.shape[1]
    s_pad = -(-s // MOBA_BLOCK) * MOBA_BLOCK
    q, k, v = (_pad_seq(t.astype(jnp.float32), s_pad).transpose(0, 2, 1, 3) for t in (q, k, v))
    o = lax.map(lambda args: jax.vmap(_moba_head)(*args), (q, k, v))
    return o.transpose(0, 2, 1, 3)[:, :s]


def _conformer_conv(u, dw_w, dw_b, ln_g, ln_b):
    a, g = jnp.split(u, 2, axis=-1)
    y = _causal_depthwise_conv(a * jax.nn.sigmoid(g), dw_w, dw_b)
    yf = y.astype(jnp.float32)
    mu = jnp.mean(yf, axis=-1, keepdims=True)
    var = jnp.mean(jnp.square(yf - mu), axis=-1, keepdims=True)
    yf = (yf - mu) * lax.rsqrt(var + NORM_EPS) * ln_g.astype(jnp.float32) + ln_b.astype(jnp.float32)
    return jax.nn.silu(yf).astype(u.dtype)


def _hybrid_mixer(h, w_in, q_norm, k_norm, dw_w, dw_b, ln_g, ln_b, w_out):
    bsz, s, _ = h.shape
    proj = h @ w_in
    q, k, v, u = jnp.split(proj, [ATT_WIDTH, 2 * ATT_WIDTH, 3 * ATT_WIDTH], axis=-1)
    q = _rmsnorm(q.reshape(bsz, s, ATT_HEADS, HEAD_DIM), q_norm)
    k = _rmsnorm(k.reshape(bsz, s, ATT_HEADS, HEAD_DIM), k_norm)
    v = v.reshape(bsz, s, ATT_HEADS, HEAD_DIM)
    att = _moba_attention(q, k, v).reshape(bsz, s, ATT_WIDTH).astype(h.dtype)
    conv = _conformer_conv(u, dw_w, dw_b, ln_g, ln_b)
    return jnp.concatenate([att, conv], axis=-1) @ w_out


def _segsum(a):
    t = a.shape[-1]
    cs = jnp.cumsum(a, axis=-1)
    mask = jnp.tril(jnp.ones((t, t), bool))
    return jnp.where(mask, cs[..., :, None] - cs[..., None, :], -jnp.inf)


def _ssd_chunked(x, a, bm, cm):
    s = x.shape[0]
    nc = s // SSM_CHUNK
    e = SSM_HEADS // SSM_GROUPS
    x = x.reshape(nc, SSM_CHUNK, SSM_GROUPS, e, SSM_HEAD_DIM)
    a = a.reshape(nc, SSM_CHUNK, SSM_GROUPS, e).transpose(2, 3, 0, 1)
    bm = bm.reshape(nc, SSM_CHUNK, SSM_GROUPS, SSM_STATE)
    cm = cm.reshape(nc, SSM_CHUNK, SSM_GROUPS, SSM_STATE)
    a_cs = jnp.cumsum(a, axis=-1)
    decay = jnp.exp(_segsum(a))
    cb = jnp.einsum('clgn,csgn->gcls', cm, bm)
    y_diag = jnp.einsum('gcls,gecls,csgep->clgep', cb, decay, x)
    decay_states = jnp.exp(a_cs[..., -1:] - a_cs)
    states = jnp.einsum('clgn,gecl,clgep->cgepn', bm, decay_states, x)
    states = jnp.concatenate([jnp.zeros_like(states[:1]), states], axis=0)
    chunk_decay = jnp.exp(_segsum(jnp.pad(a_cs[..., -1], ((0, 0), (0, 0), (1, 0)))))
    states = jnp.einsum('gezc,cgepn->zgepn', chunk_decay, states)[:-1]
    y_off = jnp.einsum('clgn,cgepn,gecl->clgep', cm, states, jnp.exp(a_cs))
    return (y_diag + y_off).reshape(s, SSM_HEADS, SSM_HEAD_DIM)


def _mamba2_mixer(h, w_in, conv_w, conv_b, dt_bias, a_log, d_skip, norm_g, w_out):
    bsz, s, _ = h.shape
    proj = h @ w_in
    z, xbc, dt = jnp.split(proj, [SSM_INNER, SSM_INNER + SSM_CONV_DIM], axis=-1)
    xbc = jax.nn.silu(_causal_depthwise_conv(xbc, conv_w, conv_b)).astype(jnp.float32)
    xs, bm, cm = jnp.split(xbc, [SSM_INNER, SSM_INNER + SSM_GROUPS * SSM_STATE], axis=-1)
    xs = xs.reshape(bsz, s, SSM_HEADS, SSM_HEAD_DIM)
    bm = bm.reshape(bsz, s, SSM_GROUPS, SSM_STATE)
    cm = cm.reshape(bsz, s, SSM_GROUPS, SSM_STATE)
    dt = jax.nn.softplus(dt.astype(jnp.float32) + dt_bias.astype(jnp.float32))
    a = -jnp.exp(a_log.astype(jnp.float32))
    s_pad = -(-s // SSM_CHUNK) * SSM_CHUNK
    y = lax.map(lambda args: _ssd_chunked(*args),
                (_pad_seq(xs * dt[..., None], s_pad), _pad_seq(dt * a, s_pad),
                 _pad_seq(bm, s_pad), _pad_seq(cm, s_pad)))[:, :s]
    y = (y + xs * d_skip.astype(jnp.float32)[:, None]).reshape(bsz, s, SSM_INNER)
    g = (y * jax.nn.silu(z.astype(jnp.float32))).reshape(bsz, s, SSM_GROUPS, SSM_INNER // SSM_GROUPS)
    g = g * lax.rsqrt(jnp.mean(g * g, axis=-1, keepdims=True) + NORM_EPS)
    g = g.reshape(bsz, s, SSM_INNER) * norm_g.astype(jnp.float32)
    return g.astype(h.dtype) @ w_out


def _route(h, router_w, router_bias):
    t = h.shape[0]
    aff = jax.nn.sigmoid((h @ router_w).astype(jnp.float32))
    score = (aff + router_bias.astype(jnp.float32)).reshape(t, N_EXPERT_GROUPS, EXPERTS_PER_GROUP)
    group_score = lax.top_k(score, TOP_K)[0].sum(axis=-1)
    group = jnp.argmax(group_score, axis=-1)
    in_group = jnp.take_along_axis(score, group[:, None, None], axis=1)[:, 0]
    _, local = lax.top_k(in_group, TOP_K)
    expert = (group[:, None] * EXPERTS_PER_GROUP + local).astype(jnp.int32)
    w = jnp.take_along_axis(aff, expert, axis=1)
    return expert, w / jnp.sum(w, axis=-1, keepdims=True)


def _moe(h, router_w, router_bias, w_gate, w_up, w_down):
    t, d = h.shape
    expert, w = _route(h, router_w, router_bias)
    n_pairs = t * TOP_K
    n_slots = -(-n_pairs // MOE_CHUNK) * MOE_CHUNK + N_EXPERTS * MOE_CHUNK
    n_chunks = n_slots // MOE_CHUNK
    pair_e = expert.reshape(-1)
    pair_t = jnp.repeat(jnp.arange(t), TOP_K)
    order = jnp.argsort(pair_e)
    s_e = pair_e[order]
    counts = jax.ops.segment_sum(jnp.ones((n_pairs,), jnp.int32), pair_e, num_segments=N_EXPERTS)
    start = jnp.cumsum(counts) - counts
    padded = (counts + MOE_CHUNK - 1) // MOE_CHUNK * MOE_CHUNK
    padded_end = jnp.cumsum(padded)
    dest = padded_end[s_e] - padded[s_e] + jnp.arange(n_pairs) - start[s_e]
    slot_t = jnp.zeros((n_slots,), jnp.int32).at[dest].set(pair_t[order].astype(jnp.int32))
    slot_pair = jnp.full((n_slots,), n_pairs, jnp.int32).at[dest].set(order.astype(jnp.int32))
    chunk_e = jnp.minimum(jnp.searchsorted(padded_end, jnp.arange(n_chunks) * MOE_CHUNK, side='right'), N_EXPERTS - 1)
    xs = h[slot_t].reshape(n_chunks, MOE_CHUNK, d)

    def expert_block(args):
        xc, e = args
        return (jax.nn.silu(xc @ w_gate[e]) * (xc @ w_up[e])) @ w_down[e]

    ys = lax.map(expert_block, (xs, chunk_e))
    y_pair = jnp.zeros((n_pairs, d), ys.dtype).at[slot_pair].set(ys.reshape(n_slots, d), mode='drop')
    y = jnp.einsum('tk,tkd->td', w, y_pair.reshape(t, TOP_K, d).astype(jnp.float32))
    return y.astype(h.dtype)


def setup_inputs(seed: int = 0) -> dict:
    key = jax.random.key(seed)
    ks = jax.random.split(key, 32)
    f32 = jnp.float32

    def nrm(k, shape, scale):
        return jax.random.normal(k, shape, f32) * scale

    def gain(k, shape):
        return 1.0 + 0.02 * jax.random.normal(k, shape, f32)

    dt0 = jnp.exp(jax.random.uniform(ks[17], (N_ODD, SSM_HEADS), f32, math.log(DT_MIN), math.log(DT_MAX)))
    return {
        'x': nrm(ks[0], (BATCH, SEQ, D_MODEL), 1.0),
        'c': nrm(ks[1], (BATCH, D_MODEL), 1.0),
        'ada_w': nrm(ks[2], (DEPTH, D_MODEL, N_MOD * D_MODEL), ADA_SCALE),
        'ada_b': nrm(ks[3], (DEPTH, N_MOD * D_MODEL), 0.02),
        'norm_mix': gain(ks[4], (DEPTH, D_MODEL)),
        'norm_ffn': gain(ks[5], (DEPTH, D_MODEL)),
        'hy_w_in': nrm(ks[6], (N_EVEN, D_MODEL, HY_IN), D_MODEL ** -0.5),
        'hy_q_norm': gain(ks[7], (N_EVEN, HEAD_DIM)),
        'hy_k_norm': gain(ks[8], (N_EVEN, HEAD_DIM)),
        'hy_dw_w': nrm(ks[9], (N_EVEN, CONV_WIDTH, CONV_CH), CONV_WIDTH ** -0.5),
        'hy_dw_b': nrm(ks[10], (N_EVEN, CONV_CH), 0.02),
        'hy_ln_g': gain(ks[11], (N_EVEN, CONV_CH)),
        'hy_ln_b': nrm(ks[12], (N_EVEN, CONV_CH), 0.02),
        'hy_w_out': nrm(ks[13], (N_EVEN, HY_MIX, D_MODEL), HY_MIX ** -0.5),
        'ssm_w_in': nrm(ks[14], (N_ODD, D_MODEL, SSM_IN), D_MODEL ** -0.5),
        'ssm_conv_w': nrm(ks[15], (N_ODD, SSM_CONV, SSM_CONV_DIM), SSM_CONV ** -0.5),
        'ssm_conv_b': nrm(ks[16], (N_ODD, SSM_CONV_DIM), 0.02),
        'ssm_dt_bias': dt0 + jnp.log(-jnp.expm1(-dt0)),
        'ssm_a_log': jnp.log(jax.random.uniform(ks[18], (N_ODD, SSM_HEADS), f32, 1.0, 16.0)),
        'ssm_d': gain(ks[19], (N_ODD, SSM_HEADS)),
        'ssm_norm': gain(ks[20], (N_ODD, SSM_INNER)),
        'ssm_w_out': nrm(ks[21], (N_ODD, SSM_INNER, D_MODEL), SSM_INNER ** -0.5),
        'router_w': nrm(ks[22], (D_MODEL, N_EXPERTS), D_MODEL ** -0.5),
        'router_bias': nrm(ks[23], (N_EXPERTS,), 0.01),
        'exp_w_gate': nrm(ks[24], (DEPTH, N_EXPERTS, D_MODEL, EXPERT_FF), D_MODEL ** -0.5),
        'exp_w_up': nrm(ks[25], (DEPTH, N_EXPERTS, D_MODEL, EXPERT_FF), D_MODEL ** -0.5),
        'exp_w_down': nrm(ks[26], (DEPTH, N_EXPERTS, EXPERT_FF, D_MODEL), EXPERT_FF ** -0.5),
    }


def reference(x, c, ada_w, ada_b, norm_mix, norm_ffn, hy_w_in, hy_q_norm, hy_k_norm, hy_dw_w, hy_dw_b,
              hy_ln_g, hy_ln_b, hy_w_out, ssm_w_in, ssm_conv_w, ssm_conv_b, ssm_dt_bias, ssm_a_log, ssm_d,
              ssm_norm, ssm_w_out, router_w, router_bias, exp_w_gate, exp_w_up, exp_w_down):
    bsz, s, d = x.shape
    cond = jax.nn.silu(c)
    for layer in range(DEPTH):
        mod = (cond @ ada_w[layer] + ada_b[layer])[:, None, :]
        sh1, sc1, g1, sh2, sc2, g2 = jnp.split(mod, N_MOD, axis=-1)
        h = _rmsnorm(x, norm_mix[layer]) * (1 + sc1) + sh1
        j = layer // 2
        if layer % 2 == 0:
            m = _hybrid_mixer(h, hy_w_in[j], hy_q_norm[j], hy_k_norm[j], hy_dw_w[j], hy_dw_b[j],
                              hy_ln_g[j], hy_ln_b[j], hy_w_out[j])
        else:
            m = _mamba2_mixer(h, ssm_w_in[j], ssm_conv_w[j], ssm_conv_b[j], ssm_dt_bias[j], ssm_a_log[j],
                              ssm_d[j], ssm_norm[j], ssm_w_out[j])
        x = x + g1 * m
        h = _rmsnorm(x, norm_ffn[layer]) * (1 + sc2) + sh2
        f = _moe(h.reshape(bsz * s, d), router_w, router_bias, exp_w_gate[layer], exp_w_up[layer],
                 exp_w_down[layer]).reshape(bsz, s, d)
        x = x + g2 * f
    return x
```

```python
import functools

import jax
import jax.numpy as jnp
from jax import lax
from jax.experimental import pallas as pl
from jax.experimental.pallas import tpu as pltpu

F32 = jnp.float32
BF16 = jnp.bfloat16
I32 = jnp.int32

NORM_EPS = 1e-6
N_MOD = 6
HEAD_DIM = 64
ATT_HEADS = 8
ATT_WIDTH = ATT_HEADS * HEAD_DIM
MOBA_BLOCK = 256
MOBA_TOPK = 3
CONV_WIDTH = 31
CONV_HALO = 32
SSM_HEAD_DIM = 64
SSM_GROUPS = 4
SSM_STATE = 128
SSM_CONV = 4
SSM_CHUNK = 256
SSM_HALO = 8
N_EXPERTS = 16
N_EXPERT_GROUPS = 4
EXPERTS_PER_GROUP = 4
N_PAIRS = 6
N_CLASSES = N_EXPERT_GROUPS * N_PAIRS
CLASS_ROWS = 32
MOE_TILE = 256
W_LANES = 128
NEG = -1e30
VMEM_LIMIT = 56 * 1024 * 1024


def _params(n_axes):
    return pltpu.CompilerParams(dimension_semantics=("arbitrary",) * n_axes,
                                vmem_limit_bytes=VMEM_LIMIT)


def _dot(a, b):
    return jnp.dot(a, b, preferred_element_type=F32)


def _dot_nt(a, b):
    return lax.dot_general(a, b, (((1,), (1,)), ((), ())), preferred_element_type=F32)


def _split2(x):
    hi = x.astype(BF16)
    lo = (x - hi.astype(F32)).astype(BF16)
    return hi, lo


def _split3(x):
    a = x.astype(BF16)
    r = x - a.astype(F32)
    b = r.astype(BF16)
    c = (r - b.astype(F32)).astype(BF16)
    return a, b, c


def _silu(x):
    return x * jax.nn.sigmoid(x)


def _mod_norm(x, g, sc, sh):
    ms = jnp.mean(x * x, axis=-1, keepdims=True)
    return x * lax.rsqrt(ms + NORM_EPS) * g * (1.0 + sc) + sh


def _ada_kernel(c_ref, w_ref, b_ref, o_ref):
    cond = _silu(c_ref[...])
    ch, cl = _split2(cond)
    wh, wl = _split2(w_ref[0])
    o_ref[0] = _dot(ch, wh) + _dot(ch, wl) + _dot(cl, wh) + b_ref[0]


def _ada(c8, ada_w, ada_b):
    depth, d, n = ada_w.shape
    tn = 1536
    return pl.pallas_call(
        _ada_kernel,
        grid=(depth, n // tn),
        in_specs=[pl.BlockSpec((8, d), lambda l, j: (0, 0)),
                  pl.BlockSpec((1, d, tn), lambda l, j: (l, 0, j)),
                  pl.BlockSpec((1, 1, tn), lambda l, j: (l, 0, j))],
        out_specs=pl.BlockSpec((1, 8, tn), lambda l, j: (l, 0, j)),
        out_shape=jax.ShapeDtypeStruct((depth, 8, n), F32),
        compiler_params=_params(2), name="ada_mod",
    )(c8, ada_w, ada_b.reshape(depth, 1, n))


def _hy_in_kernel(x_ref, sc_ref, sh_ref, g_ref, wk_ref, wu_ref, wqt_ref, wvt_ref, qg_ref, kg_ref, p_ref,
                  k_ref, qt_ref, vt_ref, u_ref, km_ref):
    tm = x_ref.shape[1]
    h = _mod_norm(x_ref[0], g_ref[...], sc_ref[0], sh_ref[0])
    hb = h.astype(BF16)
    k = _dot(hb, wk_ref[...])
    khi, klo = _split2(k * k)
    ms = _dot(khi, p_ref[...]) + _dot(klo, p_ref[...])
    kn = k * lax.rsqrt(ms + NORM_EPS) * kg_ref[...]
    k_ref[0] = kn.astype(BF16)
    km_ref[0, 0] = jnp.mean(kn.reshape(tm // MOBA_BLOCK, MOBA_BLOCK, ATT_WIDTH), axis=1)
    qt = _dot_nt(wqt_ref[...], hb).reshape(ATT_HEADS, HEAD_DIM, tm)
    qms = jnp.mean(qt * qt, axis=1, keepdims=True)
    qn = (qt * lax.rsqrt(qms + NORM_EPS)).reshape(ATT_WIDTH, tm) * qg_ref[...]
    qt_ref[0] = qn.astype(BF16)
    vt_ref[0] = _dot_nt(wvt_ref[...], hb).astype(BF16)
    ag = _dot(hb, wu_ref[...])
    half = ag.shape[1] // 2
    u_ref[0] = (ag[:, :half] * jax.nn.sigmoid(ag[:, half:])).astype(BF16)


def _hy_in(x, sc, sh, g, wk, wu, wqt, wvt, qg, kg, pmat, tm):
    bsz, s, d = x.shape
    cc = wu.shape[1] // 2
    full = lambda shape: pl.BlockSpec(shape, lambda b, i: (0,) * len(shape))
    return pl.pallas_call(
        _hy_in_kernel,
        grid=(bsz, s // tm),
        in_specs=[pl.BlockSpec((1, tm, d), lambda b, i: (b, i, 0)),
                  pl.BlockSpec((1, 1, d), lambda b, i: (b, 0, 0)),
                  pl.BlockSpec((1, 1, d), lambda b, i: (b, 0, 0)),
                  full((1, d)), full(wk.shape), full(wu.shape), full(wqt.shape), full(wvt.shape),
                  full(qg.shape), full(kg.shape), full(pmat.shape)],
        out_specs=[pl.BlockSpec((1, tm, ATT_WIDTH), lambda b, i: (b, i, 0)),
                   pl.BlockSpec((1, ATT_WIDTH, tm), lambda b, i: (b, 0, i)),
                   pl.BlockSpec((1, ATT_WIDTH, tm), lambda b, i: (b, 0, i)),
                   pl.BlockSpec((1, tm, cc), lambda b, i: (b, i, 0)),
                   pl.BlockSpec((1, 1, tm // MOBA_BLOCK, ATT_WIDTH), lambda b, i: (b, i, 0, 0))],
        out_shape=[jax.ShapeDtypeStruct((bsz, s, ATT_WIDTH), BF16),
                   jax.ShapeDtypeStruct((bsz, ATT_WIDTH, s), BF16),
                   jax.ShapeDtypeStruct((bsz, ATT_WIDTH, s), BF16),
                   jax.ShapeDtypeStruct((bsz, s, cc), BF16),
                   jax.ShapeDtypeStruct((bsz, s // tm, tm // MOBA_BLOCK, ATT_WIDTH), F32)],
        compiler_params=_params(2), name="hy_in_proj",
    )(x, sc, sh, g, wk, wu, wqt, wvt, qg, kg, pmat)


def _gate_kernel(km_ref, qt_ref, o_ref):
    nb = km_ref.shape[1]
    ts = qt_ref.shape[2]
    rows = ATT_HEADS * nb
    km = km_ref[0]
    kmt = jnp.broadcast_to(km[None], (ATT_HEADS, nb, ATT_WIDTH)).reshape(rows, ATT_WIDTH)
    rh = lax.broadcasted_iota(I32, (rows, ATT_WIDTH), 0) // nb
    ch = lax.broadcasted_iota(I32, (rows, ATT_WIDTH), 1) // HEAD_DIM
    kbd = jnp.where(rh == ch, kmt, 0.0)
    khi, klo = _split2(kbd)
    qt = qt_ref[0]
    gate = (_dot(khi, qt) + _dot(klo, qt)).reshape(ATT_HEADS, nb, ts)
    blk = lax.broadcasted_iota(I32, (ATT_HEADS, nb, ts), 1)
    qblk = (pl.program_id(1) * ts + lax.broadcasted_iota(I32, (ATT_HEADS, nb, ts), 2)) // MOBA_BLOCK
    g = jnp.where(blk < qblk, gate, -jnp.inf)
    sel = jnp.zeros(g.shape, jnp.bool_)
    for _ in range(MOBA_TOPK):
        m = jnp.max(g, axis=1, keepdims=True)
        first = jnp.min(jnp.where((g == m) & (m > -jnp.inf), blk, nb), axis=1, keepdims=True)
        pick = blk == first
        sel = sel | pick
        g = jnp.where(pick, -jnp.inf, g)
    o_ref[0] = jnp.where(sel, 0.0, NEG).reshape(rows, ts)


def _gate(kmean, qt, ts):
    bsz, nb, _ = kmean.shape
    s = qt.shape[2]
    rows = ATT_HEADS * nb
    return pl.pallas_call(
        _gate_kernel,
        grid=(bsz, s // ts),
        in_specs=[pl.BlockSpec((1, nb, ATT_WIDTH), lambda b, i: (b, 0, 0)),
                  pl.BlockSpec((1, ATT_WIDTH, ts), lambda b, i: (b, 0, i))],
        out_specs=pl.BlockSpec((1, rows, ts), lambda b, i: (b, 0, i)),
        out_shape=jax.ShapeDtypeStruct((bsz, rows, s), F32),
        compiler_params=_params(2), name="moba_gate",
    )(kmean, qt)


def _attn_kernel(qt_ref, k_ref, vt_ref, bias_ref, o_ref, *, nb):
    blk = MOBA_BLOCK
    i = pl.program_id(2)
    qt = qt_ref[0]
    row = lax.broadcasted_iota(I32, qt.shape, 0)
    zero = jnp.zeros_like(qt)
    qh = (jnp.where(row < HEAD_DIM, qt, zero), jnp.where(row >= HEAD_DIM, qt, zero))
    kpos = lax.broadcasted_iota(I32, (blk, blk), 0)
    qpos = lax.broadcasted_iota(I32, (blk, blk), 1)
    causal = kpos <= qpos

    def step(j, s_mask, carry):
        off = pl.multiple_of(j * blk, blk)
        kj = k_ref[0, pl.ds(off, blk), :]
        out = []
        for h in range(2):
            m, l, acc = carry[h]
            s = s_mask(_dot(kj, qh[h]), h, j)
            m_new = jnp.maximum(m, jnp.max(s, axis=0, keepdims=True))
            p = jnp.exp(s - m_new)
            alpha = jnp.exp(m - m_new)
            vj = vt_ref[0, h * HEAD_DIM:(h + 1) * HEAD_DIM, pl.ds(off, blk)]
            acc = alpha * acc + _dot(vj, p.astype(BF16))
            l = alpha * l + jnp.sum(p, axis=0, keepdims=True)
            out.append((m_new, l, acc))
        return tuple(out)

    init = tuple((jnp.full((1, blk), NEG, F32), jnp.zeros((1, blk), F32), jnp.zeros((HEAD_DIM, blk), F32))
                 for _ in range(2))
    carry = step(i, lambda s, h, j: jnp.where(causal, s, NEG), init)
    carry = lax.fori_loop(
        0, i, lambda j, c: step(j, lambda s, h, jj: s + bias_ref[0, pl.ds(h * nb + jj, 1), :], c), carry)
    for h in range(2):
        m, l, acc = carry[h]
        o_ref[0, h * HEAD_DIM:(h + 1) * HEAD_DIM, :] = (acc / l).astype(BF16)


def _attn(qt, k, vt, bias):
    bsz, s, _ = k.shape
    nb = s // MOBA_BLOCK
    pairs = ATT_HEADS // 2
    return pl.pallas_call(
        functools.partial(_attn_kernel, nb=nb),
        grid=(bsz, pairs, nb),
        in_specs=[pl.BlockSpec((1, 2 * HEAD_DIM, MOBA_BLOCK), lambda b, p, i: (b, p, i)),
                  pl.BlockSpec((1, s, 2 * HEAD_DIM), lambda b, p, i: (b, 0, p)),
                  pl.BlockSpec((1, 2 * HEAD_DIM, s), lambda b, p, i: (b, p, 0)),
                  pl.BlockSpec((1, 2 * nb, MOBA_BLOCK), lambda b, p, i: (b, p, i))],
        out_specs=pl.BlockSpec((1, 2 * HEAD_DIM, MOBA_BLOCK), lambda b, p, i: (b, p, i)),
        out_shape=jax.ShapeDtypeStruct((bsz, ATT_WIDTH, s), BF16),
        compiler_params=_params(3), name="moba_attn",
    )(qt, k, vt, bias)


def _route(logit_t, rb):
    aff = jax.nn.sigmoid(logit_t)
    score = aff + rb
    s = [score[e:e + 1, :] for e in range(N_EXPERTS)]
    a = [aff[e:e + 1, :] for e in range(N_EXPERTS)]
    n = EXPERTS_PER_GROUP

    def top2_sum(v):
        best = None
        for x in range(n):
            for y in range(x + 1, n):
                t = v[x] + v[y]
                best = t if best is None else jnp.maximum(best, t)
        return best

    def first_argmax(v):
        idx = jnp.zeros(v[0].shape, I32)
        cur = v[0]
        for x in range(1, len(v)):
            better = v[x] > cur
            idx = jnp.where(better, x, idx)
            cur = jnp.where(better, v[x], cur)
        return idx

    grp = first_argmax([top2_sum(s[n * g:n * g + n]) for g in range(N_EXPERT_GROUPS)])

    def in_group(v, x):
        out = v[(N_EXPERT_GROUPS - 1) * n + x]
        for g in range(N_EXPERT_GROUPS - 2, -1, -1):
            out = jnp.where(grp == g, v[n * g + x], out)
        return out

    sg = [in_group(s, x) for x in range(n)]
    ag = [in_group(a, x) for x in range(n)]
    l1 = first_argmax(sg)
    l2 = first_argmax([jnp.where(l1 == x, -jnp.inf, sg[x]) for x in range(n)])
    lo = jnp.minimum(l1, l2)
    hi = jnp.maximum(l1, l2)
    pair = jnp.where(lo == 0, hi - 1, jnp.where(lo == 1, hi + 1, 5))
    a_lo = jnp.zeros_like(ag[0])
    a_hi = jnp.zeros_like(ag[0])
    for x in range(n):
        a_lo = jnp.where(lo == x, ag[x], a_lo)
        a_hi = jnp.where(hi == x, ag[x], a_hi)
    tot = a_lo + a_hi
    return grp * N_PAIRS + pair, a_lo / tot, a_hi / tot


def _post_mixer(x_new, g_ref, sc_ref, sh_ref, rwt_ref, rb_ref, h2x_ref, cls_ref):
    tm, d = x_new.shape
    h2 = _mod_norm(x_new, g_ref[...], sc_ref[0], sh_ref[0])
    hh, hl = _split2(h2)
    rh, rl = _split2(rwt_ref[...])
    logit_t = _dot_nt(rh, hh) + _dot_nt(rh, hl) + _dot_nt(rl, hh)
    cls, w_lo, w_hi = _route(logit_t, rb_ref[...])
    cls_ref[0] = cls
    wrow = lax.broadcasted_iota(I32, (W_LANES, tm), 0)
    wt = jnp.where(wrow == 0, w_lo, jnp.where(wrow == 1, w_hi, 0.0))
    h2x_ref[:, :d] = h2
    h2x_ref[:, d:] = wt.T


def _hy_out_kernel(u_ref, up_ref, at_ref, x_ref, g1_ref, wtop_ref, wbot_ref, dww_ref, dwb_ref, lng_ref, lnb_ref,
                   g_ref, sc_ref, sh_ref, rwt_ref, rb_ref,
                   xn_ref, h2x_ref, cls_ref, cat_ref):
    tm = u_ref.shape[1]
    prev = up_ref[0].astype(F32)
    cat_ref[:CONV_HALO, :] = jnp.where(pl.program_id(1) == 0, 0.0, prev)
    cat_ref[CONV_HALO:, :] = u_ref[0].astype(F32)
    y = jnp.zeros((tm, u_ref.shape[2]), F32) + dwb_ref[...]
    for j in range(CONV_WIDTH):
        y = y + dww_ref[j:j + 1, :] * cat_ref[pl.ds(CONV_HALO - CONV_WIDTH + 1 + j, tm), :]
    mu = jnp.mean(y, axis=-1, keepdims=True)
    var = jnp.mean(jnp.square(y - mu), axis=-1, keepdims=True)
    cv = _silu((y - mu) * lax.rsqrt(var + NORM_EPS) * lng_ref[...] + lnb_ref[...])
    att = at_ref[0].astype(F32).T.astype(BF16)
    m = _dot(att, wtop_ref[...]) + _dot(cv.astype(BF16), wbot_ref[...])
    x_new = x_ref[0] + g1_ref[0] * m
    xn_ref[0] = x_new
    _post_mixer(x_new, g_ref, sc_ref, sh_ref, rwt_ref, rb_ref, h2x_ref, cls_ref)


def _hy_out(u, att_t, x, g1, wtop, wbot, dww, dwb, lng, lnb, g, sc, sh, rwt, rb, tm):
    bsz, s, d = x.shape
    cc = u.shape[2]
    nt = s // tm
    full = lambda shape: pl.BlockSpec(shape, lambda b, i: (0,) * len(shape))
    per_b = pl.BlockSpec((1, 1, d), lambda b, i: (b, 0, 0))
    halo = tm // CONV_HALO
    return pl.pallas_call(
        _hy_out_kernel,
        grid=(bsz, nt),
        in_specs=[pl.BlockSpec((1, tm, cc), lambda b, i: (b, i, 0)),
                  pl.BlockSpec((1, CONV_HALO, cc), lambda b, i: (b, jnp.maximum(i * halo - 1, 0), 0)),
                  pl.BlockSpec((1, ATT_WIDTH, tm), lambda b, i: (b, 0, i)),
                  pl.BlockSpec((1, tm, d), lambda b, i: (b, i, 0)),
                  per_b, full(wtop.shape), full(wbot.shape), full(dww.shape), full(dwb.shape),
                  full(lng.shape), full(lnb.shape), full(g.shape), per_b, per_b, full(rwt.shape), full(rb.shape)],
        out_specs=[pl.BlockSpec((1, tm, d), lambda b, i: (b, i, 0)),
                   pl.BlockSpec((tm, d + W_LANES), lambda b, i: (b * nt + i, 0)),
                   pl.BlockSpec((1, 1, tm), lambda b, i: (b * nt + i, 0, 0))],
        out_shape=[jax.ShapeDtypeStruct((bsz, s, d), F32),
                   jax.ShapeDtypeStruct((bsz * s, d + W_LANES), F32),
                   jax.ShapeDtypeStruct((bsz * nt, 1, tm), I32)],
        scratch_shapes=[pltpu.VMEM((tm + CONV_HALO, cc), F32)],
        compiler_params=_params(2), name="hy_out_proj",
    )(u, u, att_t, x, g1, wtop, wbot, dww, dwb, lng, lnb, g, sc, sh, rwt, rb)


def _ssm_in_kernel(x_ref, sc_ref, sh_ref, g_ref, wz_ref, wx_ref, wdt_ref, cw_ref, cb_ref, dtb_ref,
                   sz_ref, xbc_ref, dt_ref, cat_ref):
    tm = x_ref.shape[1]
    h = _mod_norm(x_ref[0], g_ref[...], sc_ref[0], sh_ref[0])
    hb = h.astype(BF16)
    sz_ref[0] = _silu(_dot(hb, wz_ref[...])).astype(BF16)
    t = _dot(hb, wdt_ref[...]) + dtb_ref[...]
    dt_ref[0] = jnp.maximum(t, 0.0) + jnp.log(1.0 + jnp.exp(-jnp.abs(t)))

    @pl.when(pl.program_id(1) == 0)
    def _():
        cat_ref[:SSM_HALO, :] = jnp.zeros((SSM_HALO, cat_ref.shape[1]), F32)

    cat_ref[SSM_HALO:, :] = _dot(hb, wx_ref[...])
    y = jnp.zeros((tm, cat_ref.shape[1]), F32) + cb_ref[...]
    for j in range(SSM_CONV):
        y = y + cw_ref[j:j + 1, :] * cat_ref[pl.ds(SSM_HALO - SSM_CONV + 1 + j, tm), :]
    xbc_ref[0] = _silu(y).astype(BF16)
    cat_ref[:SSM_HALO, :] = cat_ref[pl.ds(tm, SSM_HALO), :]


def _ssm_in(x, sc, sh, g, wz, wx, wdt, cw, cb, dtb, tm):
    bsz, s, d = x.shape
    full = lambda shape: pl.BlockSpec(shape, lambda b, i: (0,) * len(shape))
    per_b = pl.BlockSpec((1, 1, d), lambda b, i: (b, 0, 0))
    tile = lambda n: pl.BlockSpec((1, tm, n), lambda b, i: (b, i, 0))
    return pl.pallas_call(
        _ssm_in_kernel,
        grid=(bsz, s // tm),
        in_specs=[tile(d), per_b, per_b, full(g.shape), full(wz.shape), full(wx.shape), full(wdt.shape),
                  full(cw.shape), full(cb.shape), full(dtb.shape)],
        out_specs=[tile(wz.shape[1]), tile(wx.shape[1]), tile(wdt.shape[1])],
        out_shape=[jax.ShapeDtypeStruct((bsz, s, wz.shape[1]), BF16),
                   jax.ShapeDtypeStruct((bsz, s, wx.shape[1]), BF16),
                   jax.ShapeDtypeStruct((bsz, s, wdt.shape[1]), F32)],
        scratch_shapes=[pltpu.VMEM((tm + SSM_HALO, wx.shape[1]), F32)],
        compiler_params=_params(2), name="ssm_in_proj",
    )(x, sc, sh, g, wz, wx, wdt, cw, cb, dtb)


def _ssd_kernel(xg_ref, b_ref, c_ref, sz_ref, dt_ref, alog_ref, dsk_ref, ng_ref, wout_ref, x_ref, g1_ref,
                g_ref, sc_ref, sh_ref, rwt_ref, rb_ref,
                xn_ref, h2x_ref, cls_ref, state_ref, macc_ref, y_ref):
    L = SSM_CHUNK
    c = pl.program_id(1)
    g = pl.program_id(2)
    gw = xg_ref.shape[2]
    pw = 2 * SSM_HEAD_DIM

    @pl.when((c == 0) & (g == 0))
    def _():
        state_ref[...] = jnp.zeros(state_ref.shape, F32)

    dt = dt_ref[0]
    a = dt * (-jnp.exp(alog_ref[0]))
    row = lax.broadcasted_iota(I32, (L, L), 0)
    col = lax.broadcasted_iota(I32, (L, L), 1)
    lower = row >= col
    tri = jnp.where(lower, 1.0, 0.0).astype(BF16)
    a1, a2, a3 = _split3(a)
    a_cs = _dot(tri, a1) + _dot(tri, a2) + _dot(tri, a3)
    a_cs_t = a_cs.T
    dt_t = dt.T
    a_last = a_cs[L - 1:L, :]
    w_state = dt * jnp.exp(a_last - a_cs)
    e_acs = jnp.exp(a_cs)
    bg = b_ref[0]
    cg = c_ref[0]
    cb = _dot_nt(cg, bg)
    st = state_ref[g]
    y_off = _dot_nt(cg, st.astype(BF16))
    lane = lax.broadcasted_iota(I32, (L, pw), 1)
    lo_half = lane < SSM_HEAD_DIM
    prow = lax.broadcasted_iota(I32, (pw, SSM_STATE), 0)
    for q in range(gw // pw):
        x2 = xg_ref[0, :, q * pw:(q + 1) * pw]
        yp = y_off[:, q * pw:(q + 1) * pw] * jnp.where(lo_half, e_acs[:, 2 * q:2 * q + 1],
                                                       e_acs[:, 2 * q + 1:2 * q + 2])
        for e in range(2):
            hd = 2 * q + e
            diff = a_cs[:, hd:hd + 1] - a_cs_t[hd:hd + 1, :]
            dec = jnp.exp(jnp.where(lower, diff, NEG))
            gm = (cb * dec * dt_t[hd:hd + 1, :]).astype(BF16)
            xm = jnp.where(lo_half if e == 0 else jnp.logical_not(lo_half), x2, jnp.zeros_like(x2))
            yp = yp + _dot(gm, xm)
        y_ref[:, q * pw:(q + 1) * pw] = yp
        wp = jnp.where(lo_half, w_state[:, 2 * q:2 * q + 1], w_state[:, 2 * q + 1:2 * q + 2])
        xw = (x2.astype(F32) * wp).T.astype(BF16)
        dec_c = jnp.exp(jnp.where(prow < SSM_HEAD_DIM, a_last[:, 2 * q:2 * q + 1], a_last[:, 2 * q + 1:2 * q + 2]))
        state_ref[g, q * pw:(q + 1) * pw, :] = st[q * pw:(q + 1) * pw, :] * dec_c + _dot(xw, bg)

    y = y_ref[...] + xg_ref[0].astype(F32) * dsk_ref[0]
    gt = y * sz_ref[0].astype(F32)
    ms = jnp.mean(gt * gt, axis=-1, keepdims=True)
    gn = gt * lax.rsqrt(ms + NORM_EPS) * ng_ref[0]
    contrib = _dot(gn.astype(BF16), wout_ref[pl.ds(pl.multiple_of(g * gw, gw), gw), :])

    @pl.when(g == 0)
    def _():
        macc_ref[...] = contrib

    @pl.when(g > 0)
    def _():
        macc_ref[...] += contrib

    @pl.when(g == SSM_GROUPS - 1)
    def _():
        x_new = x_ref[0] + g1_ref[0] * macc_ref[...]
        xn_ref[0] = x_new
        _post_mixer(x_new, g_ref, sc_ref, sh_ref, rwt_ref, rb_ref, h2x_ref, cls_ref)


def _ssd(xbc, sz, dt, alog, dskip, ng, wout, x, g1, g, sc, sh, rwt, rb):
    bsz, s, d = x.shape
    L = SSM_CHUNK
    nc = s // L
    inner = sz.shape[2]
    gw = inner // SSM_GROUPS
    nbc = inner // SSM_STATE
    full = lambda shape: pl.BlockSpec(shape, lambda b, c, q: (0,) * len(shape))
    per_b = pl.BlockSpec((1, 1, d), lambda b, c, q: (b, 0, 0))
    per_g = lambda n: pl.BlockSpec((1, 1, n), lambda b, c, q: (q, 0, 0))
    return pl.pallas_call(
        _ssd_kernel,
        grid=(bsz, nc, SSM_GROUPS),
        in_specs=[pl.BlockSpec((1, L, gw), lambda b, c, q: (b, c, q)),
                  pl.BlockSpec((1, L, SSM_STATE), lambda b, c, q: (b, c, nbc + q)),
                  pl.BlockSpec((1, L, SSM_STATE), lambda b, c, q: (b, c, nbc + SSM_GROUPS + q)),
                  pl.BlockSpec((1, L, gw), lambda b, c, q: (b, c, q)),
                  pl.BlockSpec((1, L, 128), lambda b, c, q: (b, c, q)),
                  per_g(128), per_g(gw), per_g(gw), full(wout.shape),
                  pl.BlockSpec((1, L, d), lambda b, c, q: (b, c, 0)),
                  per_b, full(g.shape), per_b, per_b, full(rwt.shape), full(rb.shape)],
        out_specs=[pl.BlockSpec((1, L, d), lambda b, c, q: (b, c, 0)),
                   pl.BlockSpec((L, d + W_LANES), lambda b, c, q: (b * nc + c, 0)),
                   pl.BlockSpec((1, 1, L), lambda b, c, q: (b * nc + c, 0, 0))],
        out_shape=[jax.ShapeDtypeStruct((bsz, s, d), F32),
                   jax.ShapeDtypeStruct((bsz * s, d + W_LANES), F32),
                   jax.ShapeDtypeStruct((bsz * nc, 1, L), I32)],
        scratch_shapes=[pltpu.VMEM((SSM_GROUPS, gw, SSM_STATE), F32),
                        pltpu.VMEM((L, d), F32),
                        pltpu.VMEM((L, gw), F32)],
        compiler_params=_params(3), name="ssd_out_proj",
    )(xbc, xbc, xbc, sz, dt, alog, dskip, ng, wout, x, g1, g, sc, sh, rwt, rb)


def _count_kernel(cls_ref, cnt_ref):
    @pl.when(pl.program_id(0) == 0)
    def _():
        cnt_ref[...] = jnp.zeros(cnt_ref.shape, F32)

    tr = cls_ref.shape[2]
    onehot = lax.broadcasted_iota(I32, (CLASS_ROWS, tr), 0) == cls_ref[0]
    cnt_ref[...] += jnp.sum(jnp.where(onehot, 1.0, 0.0), axis=1, keepdims=True)


def _dest_kernel(cls_ref, start_ref, dest_ref, run_ref):
    @pl.when(pl.program_id(0) == 0)
    def _():
        run_ref[...] = start_ref[...]

    tr = cls_ref.shape[2]
    onehot = lax.broadcasted_iota(I32, (CLASS_ROWS, tr), 0) == cls_ref[0]
    oh = jnp.where(onehot, 1.0, 0.0)
    upper = (lax.broadcasted_iota(I32, (tr, tr), 0) <= lax.broadcasted_iota(I32, (tr, tr), 1))
    prefix = _dot(oh.astype(BF16), jnp.where(upper, 1.0, 0.0).astype(BF16))
    dest = jnp.sum(oh * (prefix - 1.0 + run_ref[...]), axis=0, keepdims=True)
    dest_ref[0] = dest.astype(I32)
    run_ref[...] += jnp.sum(oh, axis=1, keepdims=True)


def _moe_plan(cls, tr):
    nt = cls.shape[0]
    t = nt * tr
    cnt = pl.pallas_call(
        _count_kernel, grid=(nt,),
        in_specs=[pl.BlockSpec((1, 1, tr), lambda i: (i, 0, 0))],
        out_specs=pl.BlockSpec((CLASS_ROWS, 1), lambda i: (0, 0)),
        out_shape=jax.ShapeDtypeStruct((CLASS_ROWS, 1), F32),
        compiler_params=_params(1), name="moe_count",
    )(cls)
    padded = jnp.ceil(cnt[:, 0] / MOE_TILE) * MOE_TILE
    end = jnp.cumsum(padded)
    start = end - padded
    dest = pl.pallas_call(
        _dest_kernel, grid=(nt,),
        in_specs=[pl.BlockSpec((1, 1, tr), lambda i: (i, 0, 0)),
                  pl.BlockSpec((CLASS_ROWS, 1), lambda i: (0, 0))],
        out_specs=pl.BlockSpec((1, 1, tr), lambda i: (i, 0, 0)),
        out_shape=jax.ShapeDtypeStruct((nt, 1, tr), I32),
        scratch_shapes=[pltpu.VMEM((CLASS_ROWS, 1), F32)],
        compiler_params=_params(1), name="moe_dest",
    )(cls, start.reshape(CLASS_ROWS, 1))
    n_tiles = t // MOE_TILE + N_CLASSES
    tile_row = jnp.arange(n_tiles, dtype=F32) * MOE_TILE
    total = end[N_CLASSES - 1]
    valid = tile_row < total
    tcls = jnp.sum((tile_row[:, None] >= end[None, :N_CLASSES]).astype(I32), axis=1)
    last = jnp.sum((total - MOE_TILE >= end[:N_CLASSES]).astype(I32))
    tcls = jnp.where(valid, tcls, last)
    grp = tcls // N_PAIRS
    pair = tcls % N_PAIRS
    lo = jnp.where(pair < 3, 0, jnp.where(pair < 5, 1, 2))
    hi = jnp.where(pair < 3, pair + 1, jnp.where(pair < 5, pair - 1, 3))
    meta = jnp.stack([grp * EXPERTS_PER_GROUP + lo, grp * EXPERTS_PER_GROUP + hi, valid.astype(I32)]).astype(I32)
    return dest.reshape(t), meta


def _scatter_kernel(dest_ref, src_ref, init_ref, out_ref, sem):
    del init_ref
    tm = src_ref.shape[0]
    base = pl.program_id(0) * tm

    def issue(r, carry):
        pltpu.make_async_copy(src_ref.at[pl.ds(r, 1), :], out_ref.at[pl.ds(dest_ref[base + r], 1), :], sem).start()
        return carry

    def drain(r, carry):
        pltpu.make_async_copy(src_ref.at[pl.ds(0, 1), :], out_ref.at[pl.ds(0, 1), :], sem).wait()
        return carry

    lax.fori_loop(0, tm, issue, 0)
    lax.fori_loop(0, tm, drain, 0)


def _scatter_rows(dest, src, n_slots, tm):
    t, w = src.shape
    init = jnp.zeros((n_slots, w), src.dtype)
    return pl.pallas_call(
        _scatter_kernel,
        grid_spec=pltpu.PrefetchScalarGridSpec(
            num_scalar_prefetch=1, grid=(t // tm,),
            in_specs=[pl.BlockSpec((tm, w), lambda i, d: (i, 0)),
                      pl.BlockSpec(memory_space=pl.ANY)],
            out_specs=pl.BlockSpec(memory_space=pl.ANY),
            scratch_shapes=[pltpu.SemaphoreType.DMA(())]),
        out_shape=jax.ShapeDtypeStruct((n_slots, w), src.dtype),
        input_output_aliases={2: 0},
        compiler_params=_params(1), name="moe_scatter",
    )(dest, src, init)


def _expert_kernel(meta_ref, xs_ref, wgu_a_ref, wgu_b_ref, wd_a_ref, wd_b_ref, y_ref):
    i = pl.program_id(0)
    d = y_ref.shape[1]

    @pl.when(meta_ref[2, i] == 0)
    def _():
        y_ref[...] = jnp.zeros(y_ref.shape, F32)

    @pl.when(meta_ref[2, i] != 0)
    def _():
        xb = xs_ref[:, :d].astype(BF16)
        wts = xs_ref[:, d:]
        y = jnp.zeros(y_ref.shape, F32)
        for e, (wgu_ref, wd_ref) in enumerate(((wgu_a_ref, wd_a_ref), (wgu_b_ref, wd_b_ref))):
            gu = _dot(xb, wgu_ref[0])
            ff = gu.shape[1] // 2
            act = _silu(gu[:, :ff]) * gu[:, ff:]
            y = y + wts[:, e:e + 1] * _dot(act.astype(BF16), wd_ref[0])
        y_ref[...] = y


def _experts(meta, xs, wgu, wd):
    n_slots, w = xs.shape
    d = w - W_LANES
    n_tiles = n_slots // MOE_TILE
    return pl.pallas_call(
        _expert_kernel,
        grid_spec=pltpu.PrefetchScalarGridSpec(
            num_scalar_prefetch=1, grid=(n_tiles,),
            in_specs=[pl.BlockSpec((MOE_TILE, w), lambda i, m: (i, 0)),
                      pl.BlockSpec((1,) + wgu.shape[1:], lambda i, m: (m[0, i], 0, 0)),
                      pl.BlockSpec((1,) + wgu.shape[1:], lambda i, m: (m[1, i], 0, 0)),
                      pl.BlockSpec((1,) + wd.shape[1:], lambda i, m: (m[0, i], 0, 0)),
                      pl.BlockSpec((1,) + wd.shape[1:], lambda i, m: (m[1, i], 0, 0))],
            out_specs=pl.BlockSpec((MOE_TILE, d), lambda i, m: (i, 0))),
        out_shape=jax.ShapeDtypeStruct((n_slots, d), F32),
        compiler_params=_params(1), name="moe_experts",
    )(meta, xs, wgu, wgu, wd, wd)


def _combine_kernel(dest_ref, ys_ref, x_ref, g2_ref, o_ref, buf_ref, sem):
    tm = x_ref.shape[1]
    base = (pl.program_id(0) * pl.num_programs(1) + pl.program_id(1)) * tm

    def issue(r, carry):
        pltpu.make_async_copy(ys_ref.at[pl.ds(dest_ref[base + r], 1), :], buf_ref.at[pl.ds(r, 1), :], sem).start()
        return carry

    def drain(r, carry):
        pltpu.make_async_copy(ys_ref.at[pl.ds(0, 1), :], buf_ref.at[pl.ds(0, 1), :], sem).wait()
        return carry

    lax.fori_loop(0, tm, issue, 0)
    lax.fori_loop(0, tm, drain, 0)
    o_ref[0] = x_ref[0] + g2_ref[0] * buf_ref[...]


def _combine(dest, ys, x, g2, tm):
    bsz, s, d = x.shape
    return pl.pallas_call(
        _combine_kernel,
        grid_spec=pltpu.PrefetchScalarGridSpec(
            num_scalar_prefetch=1, grid=(bsz, s // tm),
            in_specs=[pl.BlockSpec(memory_space=pl.ANY),
                      pl.BlockSpec((1, tm, d), lambda b, i, dd: (b, i, 0)),
                      pl.BlockSpec((1, 1, d), lambda b, i, dd: (b, 0, 0))],
            out_specs=pl.BlockSpec((1, tm, d), lambda b, i, dd: (b, i, 0)),
            scratch_shapes=[pltpu.VMEM((tm, d), F32), pltpu.SemaphoreType.DMA(())]),
        out_shape=jax.ShapeDtypeStruct((bsz, s, d), F32),
        compiler_params=_params(2), name="moe_combine",
    )(dest, ys, x, g2)


def _moe(x_new, h2x, cls, g2, wgu, wd):
    bsz, s, d = x_new.shape
    t = bsz * s
    tr = min(512, t)
    dest, meta = _moe_plan(cls.reshape(t // tr, 1, tr), tr)
    n_slots = t + N_CLASSES * MOE_TILE
    xs = _scatter_rows(dest, h2x, n_slots, min(256, t))
    ys = _experts(meta, xs, wgu, wd)
    return _combine(dest, ys, x_new, g2, min(256, s))


def kernel(x, c, ada_w, ada_b, norm_mix, norm_ffn, hy_w_in, hy_q_norm, hy_k_norm, hy_dw_w, hy_dw_b, hy_ln_g, hy_ln_b, hy_w_out, ssm_w_in, ssm_conv_w, ssm_conv_b, ssm_dt_bias, ssm_a_log, ssm_d, ssm_norm, ssm_w_out, router_w, router_bias, exp_w_gate, exp_w_up, exp_w_down):
    bsz, s, d = x.shape
    depth = ada_w.shape[0]
    assert s % 512 == 0 and bsz <= 8
    c8 = jnp.zeros((8, d), F32).at[:bsz].set(c)
    mod = _ada(c8, ada_w, ada_b)
    rwt = router_w.T
    rb = router_bias.reshape(N_EXPERTS, 1)
    lane_head = jnp.arange(ATT_WIDTH) // HEAD_DIM
    pmat = ((lane_head[:, None] == lane_head[None, :]).astype(F32) / HEAD_DIM).astype(BF16)

    for layer in range(depth):
        sh1, sc1, g1, sh2, sc2, g2 = (mod[layer, :bsz, i * d:(i + 1) * d].reshape(bsz, 1, d) for i in range(N_MOD))
        gm = norm_mix[layer].reshape(1, d)
        gf = norm_ffn[layer].reshape(1, d)
        j = layer // 2
        if layer % 2 == 0:
            w_in = hy_w_in[j].astype(BF16)
            aw = ATT_WIDTH
            k, qt, vt, u, kmean = _hy_in(
                x, sc1, sh1, gm, w_in[:, aw:2 * aw], w_in[:, 3 * aw:], w_in[:, :aw].T, w_in[:, 2 * aw:3 * aw].T,
                (jnp.tile(hy_q_norm[j], ATT_HEADS) * HEAD_DIM ** -0.5).reshape(aw, 1),
                jnp.tile(hy_k_norm[j], ATT_HEADS).reshape(1, aw), pmat, 512)
            bias = _gate(kmean.reshape(bsz, s // MOBA_BLOCK, aw), qt, min(1024, s))
            att_t = _attn(qt, k, vt, bias)
            w_out = hy_w_out[j].astype(BF16)
            dww = jnp.zeros((CONV_HALO, hy_dw_w.shape[2]), F32).at[:CONV_WIDTH].set(hy_dw_w[j])
            x_new, h2x, cls = _hy_out(
                u, att_t, x, g1, w_out[:aw], w_out[aw:], dww, hy_dw_b[j].reshape(1, -1),
                hy_ln_g[j].reshape(1, -1), hy_ln_b[j].reshape(1, -1), gf, sc2, sh2, rwt, rb, 256)
        else:
            w_in = ssm_w_in[j]
            inner = ssm_norm.shape[1]
            heads = ssm_a_log.shape[1]
            hpg = heads // SSM_GROUPS
            conv_dim = ssm_conv_w.shape[2]

            def by_group(v):
                v = v.reshape(v.shape[:-1] + (SSM_GROUPS, hpg))
                pad = [(0, 0)] * (v.ndim - 1) + [(0, 128 - hpg)]
                return jnp.pad(v, pad).reshape(v.shape[:-2] + (SSM_GROUPS * 128,))

            sz, xbc, dt = _ssm_in(
                x, sc1, sh1, gm, w_in[:, :inner].astype(BF16), w_in[:, inner:inner + conv_dim].astype(BF16),
                by_group(w_in[:, inner + conv_dim:]).astype(BF16), ssm_conv_w[j], ssm_conv_b[j].reshape(1, -1),
                by_group(ssm_dt_bias[j]).reshape(1, -1), 256)
            x_new, h2x, cls = _ssd(
                xbc, sz, dt, by_group(ssm_a_log[j]).reshape(SSM_GROUPS, 1, 128),
                jnp.repeat(ssm_d[j], SSM_HEAD_DIM).reshape(SSM_GROUPS, 1, -1),
                ssm_norm[j].reshape(SSM_GROUPS, 1, -1), ssm_w_out[j].astype(BF16), x, g1, gf, sc2, sh2, rwt, rb)
        wgu = jnp.concatenate([exp_w_gate[layer], exp_w_up[layer]], axis=-1).astype(BF16)
        x = _moe(x_new, h2x, cls, g2, wgu, exp_w_down[layer].astype(BF16))
    return x
```

```python
import functools

import jax
import jax.numpy as jnp
from jax import lax
from jax.experimental import pallas as pl
from jax.experimental.pallas import tpu as pltpu

F32 = jnp.float32
BF16 = jnp.bfloat16
I32 = jnp.int32

NORM_EPS = 1e-6
N_MOD = 6
HEAD_DIM = 64
ATT_HEADS = 8
ATT_WIDTH = ATT_HEADS * HEAD_DIM
MOBA_BLOCK = 256
MOBA_TOPK = 3
CONV_WIDTH = 31
CONV_HALO = 32
SSM_HEAD_DIM = 64
SSM_GROUPS = 4
SSM_STATE = 128
SSM_CONV = 4
SSM_CHUNK = 256
SSM_HALO = 8
N_EXPERTS = 16
N_EXPERT_GROUPS = 4
EXPERTS_PER_GROUP = 4
N_PAIRS = 6
N_CLASSES = N_EXPERT_GROUPS * N_PAIRS
CLASS_ROWS = 32
MOE_TILE = 256
W_LANES = 128
LOG2E = 1.4426950408889634
NEG = -1e30
VMEM_LIMIT = 56 * 1024 * 1024


def _params(n_axes):
    return pltpu.CompilerParams(dimension_semantics=("arbitrary",) * n_axes,
                                vmem_limit_bytes=VMEM_LIMIT)


def _dot(a, b):
    return jnp.dot(a, b, preferred_element_type=F32)


def _dot_nt(a, b):
    return lax.dot_general(a, b, (((1,), (1,)), ((), ())), preferred_element_type=F32)


def _split2(x):
    hi = x.astype(BF16)
    lo = (x - hi.astype(F32)).astype(BF16)
    return hi, lo


def _split3(x):
    a = x.astype(BF16)
    r = x - a.astype(F32)
    b = r.astype(BF16)
    c = (r - b.astype(F32)).astype(BF16)
    return a, b, c


def _silu(x):
    return x * jax.nn.sigmoid(x)


def _mod_norm(x, g, sc, sh):
    ms = jnp.mean(x * x, axis=-1, keepdims=True)
    return x * lax.rsqrt(ms + NORM_EPS) * g * (1.0 + sc) + sh


def _ada_kernel(c_ref, w_ref, b_ref, o_ref):
    cond = _silu(c_ref[...])
    ch, cl = _split2(cond)
    wh, wl = _split2(w_ref[0])
    o_ref[0] = _dot(ch, wh) + _dot(ch, wl) + _dot(cl, wh) + b_ref[0]


def _ada(c8, ada_w, ada_b):
    depth, d, n = ada_w.shape
    tn = 1536
    return pl.pallas_call(
        _ada_kernel,
        grid=(depth, n // tn),
        in_specs=[pl.BlockSpec((8, d), lambda l, j: (0, 0)),
                  pl.BlockSpec((1, d, tn), lambda l, j: (l, 0, j)),
                  pl.BlockSpec((1, 1, tn), lambda l, j: (l, 0, j))],
        out_specs=pl.BlockSpec((1, 8, tn), lambda l, j: (l, 0, j)),
        out_shape=jax.ShapeDtypeStruct((depth, 8, n), F32),
        compiler_params=_params(2), name="ada_mod",
    )(c8, ada_w, ada_b.reshape(depth, 1, n))


def _hy_in_kernel(x_ref, sc_ref, sh_ref, g_ref, wk_ref, wu_ref, wqt_ref, wvt_ref, qg_ref, kg_ref, p_ref,
                  k_ref, qt_ref, vt_ref, u_ref, km_ref):
    tm = x_ref.shape[1]
    h = _mod_norm(x_ref[0], g_ref[...], sc_ref[0], sh_ref[0])
    hb = h.astype(BF16)
    k = _dot(hb, wk_ref[...])
    khi, klo = _split2(k * k)
    ms = _dot(khi, p_ref[...]) + _dot(klo, p_ref[...])
    kn = k * lax.rsqrt(ms + NORM_EPS) * kg_ref[...]
    k_ref[0] = kn.astype(BF16)
    km_ref[0, 0] = jnp.mean(kn.reshape(tm // MOBA_BLOCK, MOBA_BLOCK, ATT_WIDTH), axis=1)
    qt = _dot_nt(wqt_ref[...], hb).reshape(ATT_HEADS, HEAD_DIM, tm)
    qms = jnp.mean(qt * qt, axis=1, keepdims=True)
    qn = (qt * lax.rsqrt(qms + NORM_EPS)).reshape(ATT_WIDTH, tm) * qg_ref[...]
    qt_ref[0] = qn.astype(BF16)
    vt_ref[0] = _dot_nt(wvt_ref[...], hb).astype(BF16)
    ag = _dot(hb, wu_ref[...])
    half = ag.shape[1] // 2
    u_ref[0] = (ag[:, :half] * jax.nn.sigmoid(ag[:, half:])).astype(BF16)


def _hy_in(x, sc, sh, g, wk, wu, wqt, wvt, qg, kg, pmat, tm):
    bsz, s, d = x.shape
    cc = wu.shape[1] // 2
    full = lambda shape: pl.BlockSpec(shape, lambda b, i: (0,) * len(shape))
    return pl.pallas_call(
        _hy_in_kernel,
        grid=(bsz, s // tm),
        in_specs=[pl.BlockSpec((1, tm, d), lambda b, i: (b, i, 0)),
                  pl.BlockSpec((1, 1, d), lambda b, i: (b, 0, 0)),
                  pl.BlockSpec((1, 1, d), lambda b, i: (b, 0, 0)),
                  full((1, d)), full(wk.shape), full(wu.shape), full(wqt.shape), full(wvt.shape),
                  full(qg.shape), full(kg.shape), full(pmat.shape)],
        out_specs=[pl.BlockSpec((1, tm, ATT_WIDTH), lambda b, i: (b, i, 0)),
                   pl.BlockSpec((1, ATT_WIDTH, tm), lambda b, i: (b, 0, i)),
                   pl.BlockSpec((1, ATT_WIDTH, tm), lambda b, i: (b, 0, i)),
                   pl.BlockSpec((1, tm, cc), lambda b, i: (b, i, 0)),
                   pl.BlockSpec((1, 1, tm // MOBA_BLOCK, ATT_WIDTH), lambda b, i: (b, i, 0, 0))],
        out_shape=[jax.ShapeDtypeStruct((bsz, s, ATT_WIDTH), BF16),
                   jax.ShapeDtypeStruct((bsz, ATT_WIDTH, s), BF16),
                   jax.ShapeDtypeStruct((bsz, ATT_WIDTH, s), BF16),
                   jax.ShapeDtypeStruct((bsz, s, cc), BF16),
                   jax.ShapeDtypeStruct((bsz, s // tm, tm // MOBA_BLOCK, ATT_WIDTH), F32)],
        compiler_params=_params(2), name="hy_in_proj",
    )(x, sc, sh, g, wk, wu, wqt, wvt, qg, kg, pmat)


def _gate_kernel(km_ref, qt_ref, o_ref):
    nb = km_ref.shape[1]
    ts = qt_ref.shape[2]
    rows = ATT_HEADS * nb
    km = km_ref[0]
    kmt = jnp.broadcast_to(km[None], (ATT_HEADS, nb, ATT_WIDTH)).reshape(rows, ATT_WIDTH)
    rh = lax.broadcasted_iota(I32, (rows, ATT_WIDTH), 0) // nb
    ch = lax.broadcasted_iota(I32, (rows, ATT_WIDTH), 1) // HEAD_DIM
    kbd = jnp.where(rh == ch, kmt, 0.0)
    khi, klo = _split2(kbd)
    qt = qt_ref[0]
    gate = (_dot(khi, qt) + _dot(klo, qt)).reshape(ATT_HEADS, nb, ts)
    blk = lax.broadcasted_iota(I32, (ATT_HEADS, nb, ts), 1)
    qblk = (pl.program_id(1) * ts + lax.broadcasted_iota(I32, (ATT_HEADS, nb, ts), 2)) // MOBA_BLOCK
    g = jnp.where(blk < qblk, gate, -jnp.inf)
    sel = jnp.zeros(g.shape, jnp.bool_)
    for _ in range(MOBA_TOPK):
        m = jnp.max(g, axis=1, keepdims=True)
        first = jnp.min(jnp.where((g == m) & (m > -jnp.inf), blk, nb), axis=1, keepdims=True)
        pick = blk == first
        sel = sel | pick
        g = jnp.where(pick, -jnp.inf, g)
    o_ref[0] = jnp.where(sel, 0.0, NEG).reshape(rows, ts)


def _gate(kmean, qt, ts):
    bsz, nb, _ = kmean.shape
    s = qt.shape[2]
    rows = ATT_HEADS * nb
    return pl.pallas_call(
        _gate_kernel,
        grid=(bsz, s // ts),
        in_specs=[pl.BlockSpec((1, nb, ATT_WIDTH), lambda b, i: (b, 0, 0)),
                  pl.BlockSpec((1, ATT_WIDTH, ts), lambda b, i: (b, 0, i))],
        out_specs=pl.BlockSpec((1, rows, ts), lambda b, i: (b, 0, i)),
        out_shape=jax.ShapeDtypeStruct((bsz, rows, s), F32),
        compiler_params=_params(2), name="moba_gate",
    )(kmean, qt)


def _attn_kernel(qt_ref, k_ref, vt_ref, bias_ref, o_ref, qa_ref, acc_ref, m_ref, *, nb, hp, lag):
    blk = MOBA_BLOCK
    pw = 2 * HEAD_DIM
    va = HEAD_DIM + 16
    i = pl.program_id(2)
    row = lax.broadcasted_iota(I32, (pw, blk), 0)
    for h in range(hp):
        qp = qt_ref[0, (h // 2) * pw:(h // 2 + 1) * pw, :]
        keep = (row < HEAD_DIM) if h % 2 == 0 else (row >= HEAD_DIM)
        qa_ref[h, :pw, :] = jnp.where(keep, qp, jnp.zeros_like(qp))
        qa_ref[h, pw:pw + nb, :] = bias_ref[0, h * nb:(h + 1) * nb, :].astype(BF16)
        qa_ref[h, pw + nb:, :] = jnp.zeros((blk - pw - nb, blk), BF16)
        m_ref[h] = jnp.full((1, blk), NEG, F32)
        acc_ref[h] = jnp.zeros((va, blk), F32)
    causal = lax.broadcasted_iota(I32, (blk, blk), 0) <= lax.broadcasted_iota(I32, (blk, blk), 1)
    lane = lax.broadcasted_iota(I32, (blk, pw), 1)
    ones = jnp.ones((va - HEAD_DIM, blk), BF16)

    def step(blocks):
        units = [(b, h) for b in range(len(blocks)) for h in range(hp)]
        offs = [pl.multiple_of(j * blk, blk) for j, _, _ in blocks]

        def scores(u):
            b, h = units[u]
            pp = h // 2
            ka = jnp.concatenate([k_ref[0, pl.ds(offs[b], blk), pp * pw:(pp + 1) * pw], blocks[b][1]], axis=1)
            return blocks[b][2](_dot(ka, qa_ref[h]))

        def softmax(u, s):
            h = units[u][1]
            m_old = m_ref[h]
            m_new = jnp.maximum(m_old, jnp.max(s, axis=0, keepdims=True))
            m_ref[h] = m_new
            return jnp.exp2(s - m_new).astype(BF16), jnp.exp2(m_old - m_new)

        def accumulate(u, p, alpha):
            b, h = units[u]
            vj = jnp.concatenate([vt_ref[0, h * HEAD_DIM:(h + 1) * HEAD_DIM, pl.ds(offs[b], blk)], ones], axis=0)
            acc_ref[h] = alpha * acc_ref[h] + _dot(vj, p)

        s, pa = {}, {}
        for t in range(len(units) + lag):
            if t < len(units):
                s[t] = scores(t)
            if 1 <= t <= len(units):
                pa[t - 1] = softmax(t - 1, s.pop(t - 1))
            if t >= lag:
                accumulate(t - lag, *pa.pop(t - lag))

    def past(j):
        return (j, jnp.where(lane == j, 1.0, 0.0).astype(BF16), lambda s: s)

    step([(i, jnp.zeros((blk, pw), BF16), lambda s: jnp.where(causal, s, NEG))])
    odd = i % 2

    @pl.when(odd == 1)
    def _():
        step([past(0)])

    def body(r, carry):
        step([past(odd + 2 * r), past(odd + 2 * r + 1)])
        return carry

    lax.fori_loop(0, i // 2, body, 0)
    for h in range(hp):
        acc = acc_ref[h]
        o_ref[0, h * HEAD_DIM:(h + 1) * HEAD_DIM, :] = (acc[:HEAD_DIM] / acc[HEAD_DIM:HEAD_DIM + 1]).astype(BF16)


def _attn(qt, k, vt, bias, hp):
    bsz, s, _ = k.shape
    nb = s // MOBA_BLOCK
    assert nb <= MOBA_BLOCK - 2 * HEAD_DIM and nb % 16 == 0
    hw = hp * HEAD_DIM
    return pl.pallas_call(
        functools.partial(_attn_kernel, nb=nb, hp=hp, lag=6),
        grid=(bsz, ATT_HEADS // hp, nb),
        in_specs=[pl.BlockSpec((1, hw, MOBA_BLOCK), lambda b, p, i: (b, p, i)),
                  pl.BlockSpec((1, s, hw), lambda b, p, i: (b, 0, p)),
                  pl.BlockSpec((1, hw, s), lambda b, p, i: (b, p, 0)),
                  pl.BlockSpec((1, hp * nb, MOBA_BLOCK), lambda b, p, i: (b, p, i))],
        out_specs=pl.BlockSpec((1, hw, MOBA_BLOCK), lambda b, p, i: (b, p, i)),
        out_shape=jax.ShapeDtypeStruct((bsz, ATT_WIDTH, s), BF16),
        scratch_shapes=[pltpu.VMEM((hp, MOBA_BLOCK, MOBA_BLOCK), BF16),
                        pltpu.VMEM((hp, HEAD_DIM + 16, MOBA_BLOCK), F32),
                        pltpu.VMEM((hp, 1, MOBA_BLOCK), F32)],
        compiler_params=_params(3), name="moba_attn",
    )(qt, k, vt, bias)


def _route(logit_t, rb):
    aff = jax.nn.sigmoid(logit_t)
    score = aff + rb
    s = [score[e:e + 1, :] for e in range(N_EXPERTS)]
    a = [aff[e:e + 1, :] for e in range(N_EXPERTS)]
    n = EXPERTS_PER_GROUP

    def top2_sum(v):
        best = None
        for x in range(n):
            for y in range(x + 1, n):
                t = v[x] + v[y]
                best = t if best is None else jnp.maximum(best, t)
        return best

    def first_argmax(v):
        idx = jnp.zeros(v[0].shape, I32)
        cur = v[0]
        for x in range(1, len(v)):
            better = v[x] > cur
            idx = jnp.where(better, x, idx)
            cur = jnp.where(better, v[x], cur)
        return idx

    grp = first_argmax([top2_sum(s[n * g:n * g + n]) for g in range(N_EXPERT_GROUPS)])

    def in_group(v, x):
        out = v[(N_EXPERT_GROUPS - 1) * n + x]
        for g in range(N_EXPERT_GROUPS - 2, -1, -1):
            out = jnp.where(grp == g, v[n * g + x], out)
        return out

    sg = [in_group(s, x) for x in range(n)]
    ag = [in_group(a, x) for x in range(n)]
    l1 = first_argmax(sg)
    l2 = first_argmax([jnp.where(l1 == x, -jnp.inf, sg[x]) for x in range(n)])
    lo = jnp.minimum(l1, l2)
    hi = jnp.maximum(l1, l2)
    pair = jnp.where(lo == 0, hi - 1, jnp.where(lo == 1, hi + 1, 5))
    a_lo = jnp.zeros_like(ag[0])
    a_hi = jnp.zeros_like(ag[0])
    for x in range(n):
        a_lo = jnp.where(lo == x, ag[x], a_lo)
        a_hi = jnp.where(hi == x, ag[x], a_hi)
    tot = a_lo + a_hi
    return grp * N_PAIRS + pair, a_lo / tot, a_hi / tot


def _post_mixer(x_new, g_ref, sc_ref, sh_ref, rwt_ref, rb_ref, h2x_ref, cls_ref):
    tm, d = x_new.shape
    h2 = _mod_norm(x_new, g_ref[...], sc_ref[0], sh_ref[0])
    hh, hl = _split2(h2)
    rh, rl = _split2(rwt_ref[...])
    logit_t = _dot_nt(rh, hh) + _dot_nt(rh, hl) + _dot_nt(rl, hh)
    cls, w_lo, w_hi = _route(logit_t, rb_ref[...])
    cls_ref[0] = cls
    wrow = lax.broadcasted_iota(I32, (W_LANES, tm), 0)
    wt = jnp.where(wrow == 0, w_lo, jnp.where(wrow == 1, w_hi, 0.0))
    h2x_ref[:, :d] = h2
    h2x_ref[:, d:] = wt.T


def _hy_out_kernel(u_ref, up_ref, at_ref, x_ref, g1_ref, wtop_ref, wbot_ref, dww_ref, dwb_ref, lng_ref, lnb_ref,
                   g_ref, sc_ref, sh_ref, rwt_ref, rb_ref,
                   xn_ref, h2x_ref, cls_ref, cat_ref):
    tm = u_ref.shape[1]
    prev = up_ref[0].astype(F32)
    cat_ref[:CONV_HALO, :] = jnp.where(pl.program_id(1) == 0, 0.0, prev)
    cat_ref[CONV_HALO:, :] = u_ref[0].astype(F32)
    y = jnp.zeros((tm, u_ref.shape[2]), F32) + dwb_ref[...]
    for j in range(CONV_WIDTH):
        y = y + dww_ref[j:j + 1, :] * cat_ref[pl.ds(CONV_HALO - CONV_WIDTH + 1 + j, tm), :]
    mu = jnp.mean(y, axis=-1, keepdims=True)
    var = jnp.mean(jnp.square(y - mu), axis=-1, keepdims=True)
    cv = _silu((y - mu) * lax.rsqrt(var + NORM_EPS) * lng_ref[...] + lnb_ref[...])
    att = at_ref[0].astype(F32).T.astype(BF16)
    m = _dot(att, wtop_ref[...]) + _dot(cv.astype(BF16), wbot_ref[...])
    x_new = x_ref[0] + g1_ref[0] * m
    xn_ref[0] = x_new
    _post_mixer(x_new, g_ref, sc_ref, sh_ref, rwt_ref, rb_ref, h2x_ref, cls_ref)


def _hy_out(u, att_t, x, g1, wtop, wbot, dww, dwb, lng, lnb, g, sc, sh, rwt, rb, tm):
    bsz, s, d = x.shape
    cc = u.shape[2]
    nt = s // tm
    full = lambda shape: pl.BlockSpec(shape, lambda b, i: (0,) * len(shape))
    per_b = pl.BlockSpec((1, 1, d), lambda b, i: (b, 0, 0))
    halo = tm // CONV_HALO
    return pl.pallas_call(
        _hy_out_kernel,
        grid=(bsz, nt),
        in_specs=[pl.BlockSpec((1, tm, cc), lambda b, i: (b, i, 0)),
                  pl.BlockSpec((1, CONV_HALO, cc), lambda b, i: (b, jnp.maximum(i * halo - 1, 0), 0)),
                  pl.BlockSpec((1, ATT_WIDTH, tm), lambda b, i: (b, 0, i)),
                  pl.BlockSpec((1, tm, d), lambda b, i: (b, i, 0)),
                  per_b, full(wtop.shape), full(wbot.shape), full(dww.shape), full(dwb.shape),
                  full(lng.shape), full(lnb.shape), full(g.shape), per_b, per_b, full(rwt.shape), full(rb.shape)],
        out_specs=[pl.BlockSpec((1, tm, d), lambda b, i: (b, i, 0)),
                   pl.BlockSpec((tm, d + W_LANES), lambda b, i: (b * nt + i, 0)),
                   pl.BlockSpec((1, 1, tm), lambda b, i: (b * nt + i, 0, 0))],
        out_shape=[jax.ShapeDtypeStruct((bsz, s, d), F32),
                   jax.ShapeDtypeStruct((bsz * s, d + W_LANES), F32),
                   jax.ShapeDtypeStruct((bsz * nt, 1, tm), I32)],
        scratch_shapes=[pltpu.VMEM((tm + CONV_HALO, cc), F32)],
        compiler_params=_params(2), name="hy_out_proj",
    )(u, u, att_t, x, g1, wtop, wbot, dww, dwb, lng, lnb, g, sc, sh, rwt, rb)


def _ssm_in_kernel(x_ref, sc_ref, sh_ref, g_ref, wz_ref, wx_ref, wdt_ref, cw_ref, cb_ref, dtb_ref,
                   sz_ref, xbc_ref, dt_ref, cat_ref):
    tm = x_ref.shape[1]
    h = _mod_norm(x_ref[0], g_ref[...], sc_ref[0], sh_ref[0])
    hb = h.astype(BF16)
    sz_ref[0] = _silu(_dot(hb, wz_ref[...])).astype(BF16)
    t = _dot(hb, wdt_ref[...]) + dtb_ref[...]
    dt_ref[0] = jnp.maximum(t, 0.0) + jnp.log(1.0 + jnp.exp(-jnp.abs(t)))

    @pl.when(pl.program_id(1) == 0)
    def _():
        cat_ref[:SSM_HALO, :] = jnp.zeros((SSM_HALO, cat_ref.shape[1]), F32)

    cat_ref[SSM_HALO:, :] = _dot(hb, wx_ref[...])
    y = jnp.zeros((tm, cat_ref.shape[1]), F32) + cb_ref[...]
    for j in range(SSM_CONV):
        y = y + cw_ref[j:j + 1, :] * cat_ref[pl.ds(SSM_HALO - SSM_CONV + 1 + j, tm), :]
    xbc_ref[0] = _silu(y).astype(BF16)
    cat_ref[:SSM_HALO, :] = cat_ref[pl.ds(tm, SSM_HALO), :]


def _ssm_in(x, sc, sh, g, wz, wx, wdt, cw, cb, dtb, tm):
    bsz, s, d = x.shape
    full = lambda shape: pl.BlockSpec(shape, lambda b, i: (0,) * len(shape))
    per_b = pl.BlockSpec((1, 1, d), lambda b, i: (b, 0, 0))
    tile = lambda n: pl.BlockSpec((1, tm, n), lambda b, i: (b, i, 0))
    return pl.pallas_call(
        _ssm_in_kernel,
        grid=(bsz, s // tm),
        in_specs=[tile(d), per_b, per_b, full(g.shape), full(wz.shape), full(wx.shape), full(wdt.shape),
                  full(cw.shape), full(cb.shape), full(dtb.shape)],
        out_specs=[tile(wz.shape[1]), tile(wx.shape[1]), tile(wdt.shape[1])],
        out_shape=[jax.ShapeDtypeStruct((bsz, s, wz.shape[1]), BF16),
                   jax.ShapeDtypeStruct((bsz, s, wx.shape[1]), BF16),
                   jax.ShapeDtypeStruct((bsz, s, wdt.shape[1]), F32)],
        scratch_shapes=[pltpu.VMEM((tm + SSM_HALO, wx.shape[1]), F32)],
        compiler_params=_params(2), name="ssm_in_proj",
    )(x, sc, sh, g, wz, wx, wdt, cw, cb, dtb)


def _ssd_kernel(xg_ref, b_ref, c_ref, sz_ref, dt_ref, alog_ref, dsk_ref, ng_ref, wout_ref, x_ref, g1_ref,
                g_ref, sc_ref, sh_ref, rwt_ref, rb_ref,
                xn_ref, h2x_ref, cls_ref, state_ref, macc_ref, y_ref):
    L = SSM_CHUNK
    c = pl.program_id(1)
    g = pl.program_id(2)
    gw = xg_ref.shape[2]
    pw = 2 * SSM_HEAD_DIM

    @pl.when((c == 0) & (g == 0))
    def _():
        state_ref[...] = jnp.zeros(state_ref.shape, F32)

    dt = dt_ref[0]
    a = dt * (-jnp.exp(alog_ref[0]))
    row = lax.broadcasted_iota(I32, (L, L), 0)
    col = lax.broadcasted_iota(I32, (L, L), 1)
    lower = row >= col
    tri = jnp.where(lower, 1.0, 0.0).astype(BF16)
    a1, a2, a3 = _split3(a)
    a_cs = _dot(tri, a1) + _dot(tri, a2) + _dot(tri, a3)
    a_cs_t = a_cs.T
    dt_t = dt.T
    a_last = a_cs[L - 1:L, :]
    w_state = dt * jnp.exp(a_last - a_cs)
    e_acs = jnp.exp(a_cs)
    bg = b_ref[0]
    cg = c_ref[0]
    cb = _dot_nt(cg, bg)
    st = state_ref[g]
    y_off = _dot_nt(cg, st.astype(BF16))
    lane = lax.broadcasted_iota(I32, (L, pw), 1)
    lo_half = lane < SSM_HEAD_DIM
    prow = lax.broadcasted_iota(I32, (pw, SSM_STATE), 0)
    for q in range(gw // pw):
        x2 = xg_ref[0, :, q * pw:(q + 1) * pw]
        yp = y_off[:, q * pw:(q + 1) * pw] * jnp.where(lo_half, e_acs[:, 2 * q:2 * q + 1],
                                                       e_acs[:, 2 * q + 1:2 * q + 2])
        for e in range(2):
            hd = 2 * q + e
            diff = a_cs[:, hd:hd + 1] - a_cs_t[hd:hd + 1, :]
            dec = jnp.exp(jnp.where(lower, diff, NEG))
            gm = (cb * dec * dt_t[hd:hd + 1, :]).astype(BF16)
            xm = jnp.where(lo_half if e == 0 else jnp.logical_not(lo_half), x2, jnp.zeros_like(x2))
            yp = yp + _dot(gm, xm)
        y_ref[:, q * pw:(q + 1) * pw] = yp
        wp = jnp.where(lo_half, w_state[:, 2 * q:2 * q + 1], w_state[:, 2 * q + 1:2 * q + 2])
        xw = (x2.astype(F32) * wp).T.astype(BF16)
        dec_c = jnp.exp(jnp.where(prow < SSM_HEAD_DIM, a_last[:, 2 * q:2 * q + 1], a_last[:, 2 * q + 1:2 * q + 2]))
        state_ref[g, q * pw:(q + 1) * pw, :] = st[q * pw:(q + 1) * pw, :] * dec_c + _dot(xw, bg)

    y = y_ref[...] + xg_ref[0].astype(F32) * dsk_ref[0]
    gt = y * sz_ref[0].astype(F32)
    ms = jnp.mean(gt * gt, axis=-1, keepdims=True)
    gn = gt * lax.rsqrt(ms + NORM_EPS) * ng_ref[0]
    contrib = _dot(gn.astype(BF16), wout_ref[pl.ds(pl.multiple_of(g * gw, gw), gw), :])

    @pl.when(g == 0)
    def _():
        macc_ref[...] = contrib

    @pl.when(g > 0)
    def _():
        macc_ref[...] += contrib

    @pl.when(g == SSM_GROUPS - 1)
    def _():
        x_new = x_ref[0] + g1_ref[0] * macc_ref[...]
        xn_ref[0] = x_new
        _post_mixer(x_new, g_ref, sc_ref, sh_ref, rwt_ref, rb_ref, h2x_ref, cls_ref)


def _ssd(xbc, sz, dt, alog, dskip, ng, wout, x, g1, g, sc, sh, rwt, rb):
    bsz, s, d = x.shape
    L = SSM_CHUNK
    nc = s // L
    inner = sz.shape[2]
    gw = inner // SSM_GROUPS
    nbc = inner // SSM_STATE
    full = lambda shape: pl.BlockSpec(shape, lambda b, c, q: (0,) * len(shape))
    per_b = pl.BlockSpec((1, 1, d), lambda b, c, q: (b, 0, 0))
    per_g = lambda n: pl.BlockSpec((1, 1, n), lambda b, c, q: (q, 0, 0))
    return pl.pallas_call(
        _ssd_kernel,
        grid=(bsz, nc, SSM_GROUPS),
        in_specs=[pl.BlockSpec((1, L, gw), lambda b, c, q: (b, c, q)),
                  pl.BlockSpec((1, L, SSM_STATE), lambda b, c, q: (b, c, nbc + q)),
                  pl.BlockSpec((1, L, SSM_STATE), lambda b, c, q: (b, c, nbc + SSM_GROUPS + q)),
                  pl.BlockSpec((1, L, gw), lambda b, c, q: (b, c, q)),
                  pl.BlockSpec((1, L, 128), lambda b, c, q: (b, c, q)),
                  per_g(128), per_g(gw), per_g(gw), full(wout.shape),
                  pl.BlockSpec((1, L, d), lambda b, c, q: (b, c, 0)),
                  per_b, full(g.shape), per_b, per_b, full(rwt.shape), full(rb.shape)],
        out_specs=[pl.BlockSpec((1, L, d), lambda b, c, q: (b, c, 0)),
                   pl.BlockSpec((L, d + W_LANES), lambda b, c, q: (b * nc + c, 0)),
                   pl.BlockSpec((1, 1, L), lambda b, c, q: (b * nc + c, 0, 0))],
        out_shape=[jax.ShapeDtypeStruct((bsz, s, d), F32),
                   jax.ShapeDtypeStruct((bsz * s, d + W_LANES), F32),
                   jax.ShapeDtypeStruct((bsz * nc, 1, L), I32)],
        scratch_shapes=[pltpu.VMEM((SSM_GROUPS, gw, SSM_STATE), F32),
                        pltpu.VMEM((L, d), F32),
                        pltpu.VMEM((L, gw), F32)],
        compiler_params=_params(3), name="ssd_out_proj",
    )(xbc, xbc, xbc, sz, dt, alog, dskip, ng, wout, x, g1, g, sc, sh, rwt, rb)


def _count_kernel(cls_ref, cnt_ref):
    @pl.when(pl.program_id(0) == 0)
    def _():
        cnt_ref[...] = jnp.zeros(cnt_ref.shape, F32)

    tr = cls_ref.shape[2]
    onehot = lax.broadcasted_iota(I32, (CLASS_ROWS, tr), 0) == cls_ref[0]
    cnt_ref[...] += jnp.sum(jnp.where(onehot, 1.0, 0.0), axis=1, keepdims=True)


def _dest_kernel(cls_ref, start_ref, dest_ref, run_ref):
    @pl.when(pl.program_id(0) == 0)
    def _():
        run_ref[...] = start_ref[...]

    tr = cls_ref.shape[2]
    onehot = lax.broadcasted_iota(I32, (CLASS_ROWS, tr), 0) == cls_ref[0]
    oh = jnp.where(onehot, 1.0, 0.0)
    upper = (lax.broadcasted_iota(I32, (tr, tr), 0) <= lax.broadcasted_iota(I32, (tr, tr), 1))
    prefix = _dot(oh.astype(BF16), jnp.where(upper, 1.0, 0.0).astype(BF16))
    dest = jnp.sum(oh * (prefix - 1.0 + run_ref[...]), axis=0, keepdims=True)
    dest_ref[0] = dest.astype(I32)
    run_ref[...] += jnp.sum(oh, axis=1, keepdims=True)


def _moe_plan(cls, tr):
    nt = cls.shape[0]
    t = nt * tr
    cnt = pl.pallas_call(
        _count_kernel, grid=(nt,),
        in_specs=[pl.BlockSpec((1, 1, tr), lambda i: (i, 0, 0))],
        out_specs=pl.BlockSpec((CLASS_ROWS, 1), lambda i: (0, 0)),
        out_shape=jax.ShapeDtypeStruct((CLASS_ROWS, 1), F32),
        compiler_params=_params(1), name="moe_count",
    )(cls)
    padded = jnp.ceil(cnt[:, 0] / MOE_TILE) * MOE_TILE
    end = jnp.cumsum(padded)
    start = end - padded
    dest = pl.pallas_call(
        _dest_kernel, grid=(nt,),
        in_specs=[pl.BlockSpec((1, 1, tr), lambda i: (i, 0, 0)),
                  pl.BlockSpec((CLASS_ROWS, 1), lambda i: (0, 0))],
        out_specs=pl.BlockSpec((1, 1, tr), lambda i: (i, 0, 0)),
        out_shape=jax.ShapeDtypeStruct((nt, 1, tr), I32),
        scratch_shapes=[pltpu.VMEM((CLASS_ROWS, 1), F32)],
        compiler_params=_params(1), name="moe_dest",
    )(cls, start.reshape(CLASS_ROWS, 1))
    n_tiles = t // MOE_TILE + N_CLASSES
    tile_row = jnp.arange(n_tiles, dtype=F32) * MOE_TILE
    total = end[N_CLASSES - 1]
    valid = tile_row < total
    tcls = jnp.sum((tile_row[:, None] >= end[None, :N_CLASSES]).astype(I32), axis=1)
    last = jnp.sum((total - MOE_TILE >= end[:N_CLASSES]).astype(I32))
    tcls = jnp.where(valid, tcls, last)
    grp = tcls // N_PAIRS
    pair = tcls % N_PAIRS
    lo = jnp.where(pair < 3, 0, jnp.where(pair < 5, 1, 2))
    hi = jnp.where(pair < 3, pair + 1, jnp.where(pair < 5, pair - 1, 3))
    meta = jnp.stack([grp * EXPERTS_PER_GROUP + lo, grp * EXPERTS_PER_GROUP + hi, valid.astype(I32)]).astype(I32)
    return dest.reshape(t), meta


def _scatter_kernel(dest_ref, src_ref, init_ref, out_ref, sem):
    del init_ref
    tm = src_ref.shape[0]
    base = pl.program_id(0) * tm

    def issue(r, carry):
        pltpu.make_async_copy(src_ref.at[pl.ds(r, 1), :], out_ref.at[pl.ds(dest_ref[base + r], 1), :], sem).start()
        return carry

    def drain(r, carry):
        pltpu.make_async_copy(src_ref.at[pl.ds(0, 1), :], out_ref.at[pl.ds(0, 1), :], sem).wait()
        return carry

    lax.fori_loop(0, tm, issue, 0)
    lax.fori_loop(0, tm, drain, 0)


def _scatter_rows(dest, src, n_slots, tm):
    t, w = src.shape
    init = jnp.zeros((n_slots, w), src.dtype)
    return pl.pallas_call(
        _scatter_kernel,
        grid_spec=pltpu.PrefetchScalarGridSpec(
            num_scalar_prefetch=1, grid=(t // tm,),
            in_specs=[pl.BlockSpec((tm, w), lambda i, d: (i, 0)),
                      pl.BlockSpec(memory_space=pl.ANY)],
            out_specs=pl.BlockSpec(memory_space=pl.ANY),
            scratch_shapes=[pltpu.SemaphoreType.DMA(())]),
        out_shape=jax.ShapeDtypeStruct((n_slots, w), src.dtype),
        input_output_aliases={2: 0},
        compiler_params=_params(1), name="moe_scatter",
    )(dest, src, init)


def _expert_kernel(meta_ref, xs_ref, wgu_a_ref, wgu_b_ref, wd_a_ref, wd_b_ref, y_ref):
    i = pl.program_id(0)
    d = y_ref.shape[1]

    @pl.when(meta_ref[2, i] == 0)
    def _():
        y_ref[...] = jnp.zeros(y_ref.shape, F32)

    @pl.when(meta_ref[2, i] != 0)
    def _():
        xb = xs_ref[:, :d].astype(BF16)
        wts = xs_ref[:, d:]
        y = jnp.zeros(y_ref.shape, F32)
        for e, (wgu_ref, wd_ref) in enumerate(((wgu_a_ref, wd_a_ref), (wgu_b_ref, wd_b_ref))):
            gu = _dot(xb, wgu_ref[0])
            ff = gu.shape[1] // 2
            act = _silu(gu[:, :ff]) * gu[:, ff:]
            y = y + wts[:, e:e + 1] * _dot(act.astype(BF16), wd_ref[0])
        y_ref[...] = y


def _experts(meta, xs, wgu, wd):
    n_slots, w = xs.shape
    d = w - W_LANES
    n_tiles = n_slots // MOE_TILE
    return pl.pallas_call(
        _expert_kernel,
        grid_spec=pltpu.PrefetchScalarGridSpec(
            num_scalar_prefetch=1, grid=(n_tiles,),
            in_specs=[pl.BlockSpec((MOE_TILE, w), lambda i, m: (i, 0)),
                      pl.BlockSpec((1,) + wgu.shape[1:], lambda i, m: (m[0, i], 0, 0)),
                      pl.BlockSpec((1,) + wgu.shape[1:], lambda i, m: (m[1, i], 0, 0)),
                      pl.BlockSpec((1,) + wd.shape[1:], lambda i, m: (m[0, i], 0, 0)),
                      pl.BlockSpec((1,) + wd.shape[1:], lambda i, m: (m[1, i], 0, 0))],
            out_specs=pl.BlockSpec((MOE_TILE, d), lambda i, m: (i, 0))),
        out_shape=jax.ShapeDtypeStruct((n_slots, d), F32),
        compiler_params=_params(1), name="moe_experts",
    )(meta, xs, wgu, wgu, wd, wd)


def _combine_kernel(dest_ref, ys_ref, x_ref, g2_ref, o_ref, buf_ref, sem):
    tm = x_ref.shape[1]
    base = (pl.program_id(0) * pl.num_programs(1) + pl.program_id(1)) * tm

    def issue(r, carry):
        pltpu.make_async_copy(ys_ref.at[pl.ds(dest_ref[base + r], 1), :], buf_ref.at[pl.ds(r, 1), :], sem).start()
        return carry

    def drain(r, carry):
        pltpu.make_async_copy(ys_ref.at[pl.ds(0, 1), :], buf_ref.at[pl.ds(0, 1), :], sem).wait()
        return carry

    lax.fori_loop(0, tm, issue, 0)
    lax.fori_loop(0, tm, drain, 0)
    o_ref[0] = x_ref[0] + g2_ref[0] * buf_ref[...]


def _combine(dest, ys, x, g2, tm):
    bsz, s, d = x.shape
    return pl.pallas_call(
        _combine_kernel,
        grid_spec=pltpu.PrefetchScalarGridSpec(
            num_scalar_prefetch=1, grid=(bsz, s // tm),
            in_specs=[pl.BlockSpec(memory_space=pl.ANY),
                      pl.BlockSpec((1, tm, d), lambda b, i, dd: (b, i, 0)),
                      pl.BlockSpec((1, 1, d), lambda b, i, dd: (b, 0, 0))],
            out_specs=pl.BlockSpec((1, tm, d), lambda b, i, dd: (b, i, 0)),
            scratch_shapes=[pltpu.VMEM((tm, d), F32), pltpu.SemaphoreType.DMA(())]),
        out_shape=jax.ShapeDtypeStruct((bsz, s, d), F32),
        compiler_params=_params(2), name="moe_combine",
    )(dest, ys, x, g2)


def _moe(x_new, h2x, cls, g2, wgu, wd):
    bsz, s, d = x_new.shape
    t = bsz * s
    tr = min(512, t)
    dest, meta = _moe_plan(cls.reshape(t // tr, 1, tr), tr)
    n_slots = t + N_CLASSES * MOE_TILE
    xs = _scatter_rows(dest, h2x, n_slots, min(256, t))
    ys = _experts(meta, xs, wgu, wd)
    return _combine(dest, ys, x_new, g2, min(256, s))


def kernel(x, c, ada_w, ada_b, norm_mix, norm_ffn, hy_w_in, hy_q_norm, hy_k_norm, hy_dw_w, hy_dw_b, hy_ln_g, hy_ln_b, hy_w_out, ssm_w_in, ssm_conv_w, ssm_conv_b, ssm_dt_bias, ssm_a_log, ssm_d, ssm_norm, ssm_w_out, router_w, router_bias, exp_w_gate, exp_w_up, exp_w_down):
    bsz, s, d = x.shape
    depth = ada_w.shape[0]
    assert s % 512 == 0 and bsz <= 8
    c8 = jnp.zeros((8, d), F32).at[:bsz].set(c)
    mod = _ada(c8, ada_w, ada_b)
    rwt = router_w.T
    rb = router_bias.reshape(N_EXPERTS, 1)
    lane_head = jnp.arange(ATT_WIDTH) // HEAD_DIM
    pmat = ((lane_head[:, None] == lane_head[None, :]).astype(F32) / HEAD_DIM).astype(BF16)

    for layer in range(depth):
        sh1, sc1, g1, sh2, sc2, g2 = (mod[layer, :bsz, i * d:(i + 1) * d].reshape(bsz, 1, d) for i in range(N_MOD))
        gm = norm_mix[layer].reshape(1, d)
        gf = norm_ffn[layer].reshape(1, d)
        j = layer // 2
        if layer % 2 == 0:
            w_in = hy_w_in[j].astype(BF16)
            aw = ATT_WIDTH
            k, qt, vt, u, kmean = _hy_in(
                x, sc1, sh1, gm, w_in[:, aw:2 * aw], w_in[:, 3 * aw:], w_in[:, :aw].T, w_in[:, 2 * aw:3 * aw].T,
                (jnp.tile(hy_q_norm[j], ATT_HEADS) * (HEAD_DIM ** -0.5 * LOG2E)).reshape(aw, 1),
                jnp.tile(hy_k_norm[j], ATT_HEADS).reshape(1, aw), pmat, 512)
            bias = _gate(kmean.reshape(bsz, s // MOBA_BLOCK, aw), qt, min(1024, s))
            att_t = _attn(qt, k, vt, bias, 8)
            w_out = hy_w_out[j].astype(BF16)
            dww = jnp.zeros((CONV_HALO, hy_dw_w.shape[2]), F32).at[:CONV_WIDTH].set(hy_dw_w[j])
            x_new, h2x, cls = _hy_out(
                u, att_t, x, g1, w_out[:aw], w_out[aw:], dww, hy_dw_b[j].reshape(1, -1),
                hy_ln_g[j].reshape(1, -1), hy_ln_b[j].reshape(1, -1), gf, sc2, sh2, rwt, rb, 256)
        else:
            w_in = ssm_w_in[j]
            inner = ssm_norm.shape[1]
            heads = ssm_a_log.shape[1]
            hpg = heads // SSM_GROUPS
            conv_dim = ssm_conv_w.shape[2]

            def by_group(v):
                v = v.reshape(v.shape[:-1] + (SSM_GROUPS, hpg))
                pad = [(0, 0)] * (v.ndim - 1) + [(0, 128 - hpg)]
                return jnp.pad(v, pad).reshape(v.shape[:-2] + (SSM_GROUPS * 128,))

            sz, xbc, dt = _ssm_in(
                x, sc1, sh1, gm, w_in[:, :inner].astype(BF16), w_in[:, inner:inner + conv_dim].astype(BF16),
                by_group(w_in[:, inner + conv_dim:]).astype(BF16), ssm_conv_w[j], ssm_conv_b[j].reshape(1, -1),
                by_group(ssm_dt_bias[j]).reshape(1, -1), 256)
            x_new, h2x, cls = _ssd(
                xbc, sz, dt, by_group(ssm_a_log[j]).reshape(SSM_GROUPS, 1, 128),
                jnp.repeat(ssm_d[j], SSM_HEAD_DIM).reshape(SSM_GROUPS, 1, -1),
                ssm_norm[j].reshape(SSM_GROUPS, 1, -1), ssm_w_out[j].astype(BF16), x, g1, gf, sc2, sh2, rwt, rb)
        wgu = jnp.concatenate([exp_w_gate[layer], exp_w_up[layer]], axis=-1).astype(BF16)
        x = _moe(x_new, h2x, cls, g2, wgu, exp_w_down[layer].astype(BF16))
    return x
```

```python
import functools

import jax
import jax.numpy as jnp
from jax import lax
from jax.experimental import pallas as pl
from jax.experimental.pallas import tpu as pltpu

F32 = jnp.float32
BF16 = jnp.bfloat16
I32 = jnp.int32

NORM_EPS = 1e-6
N_MOD = 6
HEAD_DIM = 64
ATT_HEADS = 8
ATT_WIDTH = ATT_HEADS * HEAD_DIM
MOBA_BLOCK = 256
MOBA_TOPK = 3
CONV_WIDTH = 31
CONV_HALO = 32
SSM_HEAD_DIM = 64
SSM_GROUPS = 4
SSM_STATE = 128
SSM_CONV = 4
SSM_CHUNK = 256
SSM_HALO = 8
SSD_SUB = 128
N_EXPERTS = 16
N_EXPERT_GROUPS = 4
EXPERTS_PER_GROUP = 4
N_PAIRS = 6
N_CLASSES = N_EXPERT_GROUPS * N_PAIRS
CLASS_ROWS = 32
MOE_TILE = 256
W_LANES = 128
ROW_DMA_UNROLL = 8
LOG2E = 1.4426950408889634
NEG = -1e30
VMEM_LIMIT = 56 * 1024 * 1024


def _params(n_axes):
    return pltpu.CompilerParams(dimension_semantics=("arbitrary",) * n_axes,
                                vmem_limit_bytes=VMEM_LIMIT)


def _dot(a, b):
    return jnp.dot(a, b, preferred_element_type=F32)


def _dot_nt(a, b):
    return lax.dot_general(a, b, (((1,), (1,)), ((), ())), preferred_element_type=F32)


def _split2(x):
    hi = x.astype(BF16)
    lo = (x - hi.astype(F32)).astype(BF16)
    return hi, lo


def _split3(x):
    a = x.astype(BF16)
    r = x - a.astype(F32)
    b = r.astype(BF16)
    c = (r - b.astype(F32)).astype(BF16)
    return a, b, c


def _silu(x):
    return x * jax.nn.sigmoid(x)


def _mod_norm(x, g, sc, sh):
    ms = jnp.mean(x * x, axis=-1, keepdims=True)
    return x * lax.rsqrt(ms + NORM_EPS) * g * (1.0 + sc) + sh


def _ada_kernel(c_ref, w_ref, b_ref, o_ref):
    cond = _silu(c_ref[...])
    ch, cl = _split2(cond)
    wh, wl = _split2(w_ref[0])
    o_ref[0] = _dot(ch, wh) + _dot(ch, wl) + _dot(cl, wh) + b_ref[0]


def _ada(c8, ada_w, ada_b):
    depth, d, n = ada_w.shape
    tn = 1536
    return pl.pallas_call(
        _ada_kernel,
        grid=(depth, n // tn),
        in_specs=[pl.BlockSpec((8, d), lambda l, j: (0, 0)),
                  pl.BlockSpec((1, d, tn), lambda l, j: (l, 0, j)),
                  pl.BlockSpec((1, 1, tn), lambda l, j: (l, 0, j))],
        out_specs=pl.BlockSpec((1, 8, tn), lambda l, j: (l, 0, j)),
        out_shape=jax.ShapeDtypeStruct((depth, 8, n), F32),
        compiler_params=_params(2), name="ada_mod",
    )(c8, ada_w, ada_b.reshape(depth, 1, n))


def _hy_in_kernel(x_ref, sc_ref, sh_ref, g_ref, wk_ref, wu_ref, wqt_ref, wvt_ref, qg_ref, kg_ref, p_ref,
                  k_ref, qt_ref, vt_ref, u_ref, km_ref):
    tm = x_ref.shape[1]
    h = _mod_norm(x_ref[0], g_ref[...], sc_ref[0], sh_ref[0])
    hb = h.astype(BF16)
    k = _dot(hb, wk_ref[...])
    khi, klo = _split2(k * k)
    ms = _dot(khi, p_ref[...]) + _dot(klo, p_ref[...])
    kn = k * lax.rsqrt(ms + NORM_EPS) * kg_ref[...]
    k_ref[0] = kn.astype(BF16)
    km_ref[0, 0] = jnp.mean(kn.reshape(tm // MOBA_BLOCK, MOBA_BLOCK, ATT_WIDTH), axis=1)
    qt = _dot_nt(wqt_ref[...], hb).reshape(ATT_HEADS, HEAD_DIM, tm)
    qms = jnp.mean(qt * qt, axis=1, keepdims=True)
    qn = (qt * lax.rsqrt(qms + NORM_EPS)).reshape(ATT_WIDTH, tm) * qg_ref[...]
    qt_ref[0] = qn.astype(BF16)
    vt_ref[0] = _dot_nt(wvt_ref[...], hb).astype(BF16)
    ag = _dot(hb, wu_ref[...])
    half = ag.shape[1] // 2
    u_ref[0] = (ag[:, :half] * jax.nn.sigmoid(ag[:, half:])).astype(BF16)


def _hy_in(x, sc, sh, g, wk, wu, wqt, wvt, qg, kg, pmat, tm):
    bsz, s, d = x.shape
    cc = wu.shape[1] // 2
    full = lambda shape: pl.BlockSpec(shape, lambda b, i: (0,) * len(shape))
    return pl.pallas_call(
        _hy_in_kernel,
        grid=(bsz, s // tm),
        in_specs=[pl.BlockSpec((1, tm, d), lambda b, i: (b, i, 0)),
                  pl.BlockSpec((1, 1, d), lambda b, i: (b, 0, 0)),
                  pl.BlockSpec((1, 1, d), lambda b, i: (b, 0, 0)),
                  full((1, d)), full(wk.shape), full(wu.shape), full(wqt.shape), full(wvt.shape),
                  full(qg.shape), full(kg.shape), full(pmat.shape)],
        out_specs=[pl.BlockSpec((1, tm, ATT_WIDTH), lambda b, i: (b, i, 0)),
                   pl.BlockSpec((1, ATT_WIDTH, tm), lambda b, i: (b, 0, i)),
                   pl.BlockSpec((1, ATT_WIDTH, tm), lambda b, i: (b, 0, i)),
                   pl.BlockSpec((1, tm, cc), lambda b, i: (b, i, 0)),
                   pl.BlockSpec((1, 1, tm // MOBA_BLOCK, ATT_WIDTH), lambda b, i: (b, i, 0, 0))],
        out_shape=[jax.ShapeDtypeStruct((bsz, s, ATT_WIDTH), BF16),
                   jax.ShapeDtypeStruct((bsz, ATT_WIDTH, s), BF16),
                   jax.ShapeDtypeStruct((bsz, ATT_WIDTH, s), BF16),
                   jax.ShapeDtypeStruct((bsz, s, cc), BF16),
                   jax.ShapeDtypeStruct((bsz, s // tm, tm // MOBA_BLOCK, ATT_WIDTH), F32)],
        compiler_params=_params(2), name="hy_in_proj",
    )(x, sc, sh, g, wk, wu, wqt, wvt, qg, kg, pmat)


def _gate_kernel(km_ref, qt_ref, o_ref):
    nb = km_ref.shape[1]
    ts = qt_ref.shape[2]
    rows = ATT_HEADS * nb
    km = km_ref[0]
    kmt = jnp.broadcast_to(km[None], (ATT_HEADS, nb, ATT_WIDTH)).reshape(rows, ATT_WIDTH)
    rh = lax.broadcasted_iota(I32, (rows, ATT_WIDTH), 0) // nb
    ch = lax.broadcasted_iota(I32, (rows, ATT_WIDTH), 1) // HEAD_DIM
    kbd = jnp.where(rh == ch, kmt, 0.0)
    khi, klo = _split2(kbd)
    qt = qt_ref[0]
    gate = (_dot(khi, qt) + _dot(klo, qt)).reshape(ATT_HEADS, nb, ts)
    blk = lax.broadcasted_iota(I32, (ATT_HEADS, nb, ts), 1)
    qblk = (pl.program_id(1) * ts + lax.broadcasted_iota(I32, (ATT_HEADS, nb, ts), 2)) // MOBA_BLOCK
    g = jnp.where(blk < qblk, gate, -jnp.inf)
    sel = jnp.zeros(g.shape, jnp.bool_)
    for _ in range(MOBA_TOPK):
        m = jnp.max(g, axis=1, keepdims=True)
        first = jnp.min(jnp.where((g == m) & (m > -jnp.inf), blk, nb), axis=1, keepdims=True)
        pick = blk == first
        sel = sel | pick
        g = jnp.where(pick, -jnp.inf, g)
    o_ref[0] = jnp.where(sel, 0.0, NEG).reshape(rows, ts)


def _gate(kmean, qt, ts):
    bsz, nb, _ = kmean.shape
    s = qt.shape[2]
    rows = ATT_HEADS * nb
    return pl.pallas_call(
        _gate_kernel,
        grid=(bsz, s // ts),
        in_specs=[pl.BlockSpec((1, nb, ATT_WIDTH), lambda b, i: (b, 0, 0)),
                  pl.BlockSpec((1, ATT_WIDTH, ts), lambda b, i: (b, 0, i))],
        out_specs=pl.BlockSpec((1, rows, ts), lambda b, i: (b, 0, i)),
        out_shape=jax.ShapeDtypeStruct((bsz, rows, s), F32),
        compiler_params=_params(2), name="moba_gate",
    )(kmean, qt)


def _attn_kernel(qt_ref, k_ref, vt_ref, bias_ref, o_ref, qa_ref, acc_ref, m_ref, *, nb, hp, lag):
    blk = MOBA_BLOCK
    pw = 2 * HEAD_DIM
    va = HEAD_DIM + 16
    i = pl.program_id(2)
    row = lax.broadcasted_iota(I32, (pw, blk), 0)
    for h in range(hp):
        qp = qt_ref[0, (h // 2) * pw:(h // 2 + 1) * pw, :]
        keep = (row < HEAD_DIM) if h % 2 == 0 else (row >= HEAD_DIM)
        qa_ref[h, :pw, :] = jnp.where(keep, qp, jnp.zeros_like(qp))
        qa_ref[h, pw:pw + nb, :] = bias_ref[0, h * nb:(h + 1) * nb, :].astype(BF16)
        qa_ref[h, pw + nb:, :] = jnp.zeros((blk - pw - nb, blk), BF16)
        m_ref[h] = jnp.full((1, blk), NEG, F32)
        acc_ref[h] = jnp.zeros((va, blk), F32)
    causal = lax.broadcasted_iota(I32, (blk, blk), 0) <= lax.broadcasted_iota(I32, (blk, blk), 1)
    lane = lax.broadcasted_iota(I32, (blk, pw), 1)
    ones = jnp.ones((va - HEAD_DIM, blk), BF16)

    def step(blocks):
        units = [(b, h) for b in range(len(blocks)) for h in range(hp)]
        offs = [pl.multiple_of(j * blk, blk) for j, _, _ in blocks]

        def scores(u):
            b, h = units[u]
            pp = h // 2
            ka = jnp.concatenate([k_ref[0, pl.ds(offs[b], blk), pp * pw:(pp + 1) * pw], blocks[b][1]], axis=1)
            return blocks[b][2](_dot(ka, qa_ref[h]))

        def softmax(u, s):
            h = units[u][1]
            m_old = m_ref[h]
            m_new = jnp.maximum(m_old, jnp.max(s, axis=0, keepdims=True))
            m_ref[h] = m_new
            return jnp.exp2(s - m_new).astype(BF16), jnp.exp2(m_old - m_new)

        def accumulate(u, p, alpha):
            b, h = units[u]
            vj = jnp.concatenate([vt_ref[0, h * HEAD_DIM:(h + 1) * HEAD_DIM, pl.ds(offs[b], blk)], ones], axis=0)
            acc_ref[h] = alpha * acc_ref[h] + _dot(vj, p)

        s, pa = {}, {}
        for t in range(len(units) + lag):
            if t < len(units):
                s[t] = scores(t)
            if 1 <= t <= len(units):
                pa[t - 1] = softmax(t - 1, s.pop(t - 1))
            if t >= lag:
                accumulate(t - lag, *pa.pop(t - lag))

    def past(j):
        return (j, jnp.where(lane == j, 1.0, 0.0).astype(BF16), lambda s: s)

    step([(i, jnp.zeros((blk, pw), BF16), lambda s: jnp.where(causal, s, NEG))])
    odd = i % 2

    @pl.when(odd == 1)
    def _():
        step([past(0)])

    def body(r, carry):
        step([past(odd + 2 * r), past(odd + 2 * r + 1)])
        return carry

    lax.fori_loop(0, i // 2, body, 0)
    for h in range(hp):
        acc = acc_ref[h]
        o_ref[0, h * HEAD_DIM:(h + 1) * HEAD_DIM, :] = (acc[:HEAD_DIM] / acc[HEAD_DIM:HEAD_DIM + 1]).astype(BF16)


def _attn(qt, k, vt, bias, hp):
    bsz, s, _ = k.shape
    nb = s // MOBA_BLOCK
    assert nb <= MOBA_BLOCK - 2 * HEAD_DIM and nb % 16 == 0
    hw = hp * HEAD_DIM
    return pl.pallas_call(
        functools.partial(_attn_kernel, nb=nb, hp=hp, lag=6),
        grid=(bsz, ATT_HEADS // hp, nb),
        in_specs=[pl.BlockSpec((1, hw, MOBA_BLOCK), lambda b, p, i: (b, p, i)),
                  pl.BlockSpec((1, s, hw), lambda b, p, i: (b, 0, p)),
                  pl.BlockSpec((1, hw, s), lambda b, p, i: (b, p, 0)),
                  pl.BlockSpec((1, hp * nb, MOBA_BLOCK), lambda b, p, i: (b, p, i))],
        out_specs=pl.BlockSpec((1, hw, MOBA_BLOCK), lambda b, p, i: (b, p, i)),
        out_shape=jax.ShapeDtypeStruct((bsz, ATT_WIDTH, s), BF16),
        scratch_shapes=[pltpu.VMEM((hp, MOBA_BLOCK, MOBA_BLOCK), BF16),
                        pltpu.VMEM((hp, HEAD_DIM + 16, MOBA_BLOCK), F32),
                        pltpu.VMEM((hp, 1, MOBA_BLOCK), F32)],
        compiler_params=_params(3), name="moba_attn",
    )(qt, k, vt, bias)


def _route(logit_t, rb):
    aff = jax.nn.sigmoid(logit_t)
    score = aff + rb
    s = [score[e:e + 1, :] for e in range(N_EXPERTS)]
    a = [aff[e:e + 1, :] for e in range(N_EXPERTS)]
    n = EXPERTS_PER_GROUP

    def top2_sum(v):
        best = None
        for x in range(n):
            for y in range(x + 1, n):
                t = v[x] + v[y]
                best = t if best is None else jnp.maximum(best, t)
        return best

    def first_argmax(v):
        idx = jnp.zeros(v[0].shape, I32)
        cur = v[0]
        for x in range(1, len(v)):
            better = v[x] > cur
            idx = jnp.where(better, x, idx)
            cur = jnp.where(better, v[x], cur)
        return idx

    grp = first_argmax([top2_sum(s[n * g:n * g + n]) for g in range(N_EXPERT_GROUPS)])

    def in_group(v, x):
        out = v[(N_EXPERT_GROUPS - 1) * n + x]
        for g in range(N_EXPERT_GROUPS - 2, -1, -1):
            out = jnp.where(grp == g, v[n * g + x], out)
        return out

    sg = [in_group(s, x) for x in range(n)]
    ag = [in_group(a, x) for x in range(n)]
    l1 = first_argmax(sg)
    l2 = first_argmax([jnp.where(l1 == x, -jnp.inf, sg[x]) for x in range(n)])
    lo = jnp.minimum(l1, l2)
    hi = jnp.maximum(l1, l2)
    pair = jnp.where(lo == 0, hi - 1, jnp.where(lo == 1, hi + 1, 5))
    a_lo = jnp.zeros_like(ag[0])
    a_hi = jnp.zeros_like(ag[0])
    for x in range(n):
        a_lo = jnp.where(lo == x, ag[x], a_lo)
        a_hi = jnp.where(hi == x, ag[x], a_hi)
    tot = a_lo + a_hi
    return grp * N_PAIRS + pair, a_lo / tot, a_hi / tot


def _post_mixer(x_new, g_ref, sc_ref, sh_ref, rwt_ref, rb_ref, h2x_ref, cls_ref):
    tm, d = x_new.shape
    h2 = _mod_norm(x_new, g_ref[...], sc_ref[0], sh_ref[0])
    hh, hl = _split2(h2)
    rh, rl = _split2(rwt_ref[...])
    logit_t = _dot_nt(rh, hh) + _dot_nt(rh, hl) + _dot_nt(rl, hh)
    cls, w_lo, w_hi = _route(logit_t, rb_ref[...])
    cls_ref[0] = cls
    wrow = lax.broadcasted_iota(I32, (W_LANES, tm), 0)
    wt = jnp.where(wrow == 0, w_lo, jnp.where(wrow == 1, w_hi, 0.0))
    h2x_ref[:, :d] = h2
    h2x_ref[:, d:] = wt.T


def _hy_out_kernel(u_ref, up_ref, at_ref, x_ref, g1_ref, wtop_ref, wbot_ref, dww_ref, dwb_ref, lng_ref, lnb_ref,
                   g_ref, sc_ref, sh_ref, rwt_ref, rb_ref,
                   xn_ref, h2x_ref, cls_ref, cat_ref, shifted_ref):
    tm = u_ref.shape[1]
    prev = up_ref[0].astype(F32)
    cat_ref[:CONV_HALO, :] = jnp.where(pl.program_id(1) == 0, 0.0, prev)
    cat_ref[CONV_HALO:, :] = u_ref[0].astype(F32)
    span = tm + CONV_HALO - 8
    for r in range(1, 8):
        shifted_ref[r - 1] = cat_ref[pl.ds(r, span), :]
    y = jnp.zeros((tm, u_ref.shape[2]), F32) + dwb_ref[...]
    for j in range(CONV_WIDTH):
        a, r = divmod(CONV_HALO - CONV_WIDTH + 1 + j, 8)
        tap = cat_ref[pl.ds(8 * a, tm), :] if r == 0 else shifted_ref[r - 1, pl.ds(8 * a, tm), :]
        y = y + dww_ref[j:j + 1, :] * tap
    mu = jnp.mean(y, axis=-1, keepdims=True)
    var = jnp.mean(jnp.square(y - mu), axis=-1, keepdims=True)
    cv = _silu((y - mu) * lax.rsqrt(var + NORM_EPS) * lng_ref[...] + lnb_ref[...])
    att = at_ref[0].astype(F32).T.astype(BF16)
    m = _dot(att, wtop_ref[...]) + _dot(cv.astype(BF16), wbot_ref[...])
    x_new = x_ref[0] + g1_ref[0] * m
    xn_ref[0] = x_new
    _post_mixer(x_new, g_ref, sc_ref, sh_ref, rwt_ref, rb_ref, h2x_ref, cls_ref)


def _hy_out(u, att_t, x, g1, wtop, wbot, dww, dwb, lng, lnb, g, sc, sh, rwt, rb, tm):
    bsz, s, d = x.shape
    cc = u.shape[2]
    nt = s // tm
    full = lambda shape: pl.BlockSpec(shape, lambda b, i: (0,) * len(shape))
    per_b = pl.BlockSpec((1, 1, d), lambda b, i: (b, 0, 0))
    halo = tm // CONV_HALO
    return pl.pallas_call(
        _hy_out_kernel,
        grid=(bsz, nt),
        in_specs=[pl.BlockSpec((1, tm, cc), lambda b, i: (b, i, 0)),
                  pl.BlockSpec((1, CONV_HALO, cc), lambda b, i: (b, jnp.maximum(i * halo - 1, 0), 0)),
                  pl.BlockSpec((1, ATT_WIDTH, tm), lambda b, i: (b, 0, i)),
                  pl.BlockSpec((1, tm, d), lambda b, i: (b, i, 0)),
                  per_b, full(wtop.shape), full(wbot.shape), full(dww.shape), full(dwb.shape),
                  full(lng.shape), full(lnb.shape), full(g.shape), per_b, per_b, full(rwt.shape), full(rb.shape)],
        out_specs=[pl.BlockSpec((1, tm, d), lambda b, i: (b, i, 0)),
                   pl.BlockSpec((tm, d + W_LANES), lambda b, i: (b * nt + i, 0)),
                   pl.BlockSpec((1, 1, tm), lambda b, i: (b * nt + i, 0, 0))],
        out_shape=[jax.ShapeDtypeStruct((bsz, s, d), F32),
                   jax.ShapeDtypeStruct((bsz * s, d + W_LANES), F32),
                   jax.ShapeDtypeStruct((bsz * nt, 1, tm), I32)],
        scratch_shapes=[pltpu.VMEM((tm + CONV_HALO, cc), F32),
                        pltpu.VMEM((7, tm + CONV_HALO - 8, cc), F32)],
        compiler_params=_params(2), name="hy_out_proj",
    )(u, u, att_t, x, g1, wtop, wbot, dww, dwb, lng, lnb, g, sc, sh, rwt, rb)


def _ssm_in_kernel(x_ref, sc_ref, sh_ref, g_ref, wz_ref, wx_ref, wdt_ref, cw_ref, cb_ref, dtb_ref,
                   sz_ref, xbc_ref, dt_ref, cat_ref):
    tm = x_ref.shape[1]
    h = _mod_norm(x_ref[0], g_ref[...], sc_ref[0], sh_ref[0])
    hb = h.astype(BF16)
    sz_ref[0] = _silu(_dot(hb, wz_ref[...])).astype(BF16)
    t = _dot(hb, wdt_ref[...]) + dtb_ref[...]
    dt_ref[0] = jnp.maximum(t, 0.0) + jnp.log(1.0 + jnp.exp(-jnp.abs(t)))

    @pl.when(pl.program_id(1) == 0)
    def _():
        cat_ref[:SSM_HALO, :] = jnp.zeros((SSM_HALO, cat_ref.shape[1]), F32)

    cat_ref[SSM_HALO:, :] = _dot(hb, wx_ref[...])
    y = jnp.zeros((tm, cat_ref.shape[1]), F32) + cb_ref[...]
    for j in range(SSM_CONV):
        y = y + cw_ref[j:j + 1, :] * cat_ref[pl.ds(SSM_HALO - SSM_CONV + 1 + j, tm), :]
    xbc_ref[0] = _silu(y).astype(BF16)
    cat_ref[:SSM_HALO, :] = cat_ref[pl.ds(tm, SSM_HALO), :]


def _ssm_in(x, sc, sh, g, wz, wx, wdt, cw, cb, dtb, tm):
    bsz, s, d = x.shape
    full = lambda shape: pl.BlockSpec(shape, lambda b, i: (0,) * len(shape))
    per_b = pl.BlockSpec((1, 1, d), lambda b, i: (b, 0, 0))
    tile = lambda n: pl.BlockSpec((1, tm, n), lambda b, i: (b, i, 0))
    return pl.pallas_call(
        _ssm_in_kernel,
        grid=(bsz, s // tm),
        in_specs=[tile(d), per_b, per_b, full(g.shape), full(wz.shape), full(wx.shape), full(wdt.shape),
                  full(cw.shape), full(cb.shape), full(dtb.shape)],
        out_specs=[tile(wz.shape[1]), tile(wx.shape[1]), tile(wdt.shape[1])],
        out_shape=[jax.ShapeDtypeStruct((bsz, s, wz.shape[1]), BF16),
                   jax.ShapeDtypeStruct((bsz, s, wx.shape[1]), BF16),
                   jax.ShapeDtypeStruct((bsz, s, wdt.shape[1]), F32)],
        scratch_shapes=[pltpu.VMEM((tm + SSM_HALO, wx.shape[1]), F32)],
        compiler_params=_params(2), name="ssm_in_proj",
    )(x, sc, sh, g, wz, wx, wdt, cw, cb, dtb)


def _ssd_kernel(xg_ref, b_ref, c_ref, sz_ref, dt_ref, alog_ref, dsk_ref, ng_ref, wout_ref, e2_ref, x_ref, g1_ref,
                g_ref, sc_ref, sh_ref, rwt_ref, rb_ref,
                xn_ref, h2x_ref, cls_ref, state_ref, macc_ref, y_ref):
    L = SSD_SUB
    c = pl.program_id(1)
    g = pl.program_id(2)
    gw = xg_ref.shape[2]
    pw = 2 * SSM_HEAD_DIM

    @pl.when((c == 0) & (g == 0))
    def _():
        state_ref[...] = jnp.zeros(state_ref.shape, F32)

    rate = -jnp.exp(alog_ref[0]) * LOG2E
    n_sub = SSM_CHUNK // L
    subs = [slice(s * L, (s + 1) * L) for s in range(n_sub)]
    brow = lax.broadcasted_iota(I32, (SSM_CHUNK, SSM_CHUNK), 0)
    bcol = lax.broadcasted_iota(I32, (SSM_CHUNK, SSM_CHUNK), 1)
    tri = jnp.where((brow // L == bcol // L) & (brow >= bcol), 1.0, 0.0).astype(BF16)
    lower = lax.broadcasted_iota(I32, (L, L), 0) >= lax.broadcasted_iota(I32, (L, L), 1)
    lo_half = lax.broadcasted_iota(I32, (L, pw), 1) < SSM_HEAD_DIM

    def expand(v):
        return _dot(jnp.concatenate(_split2(v), axis=1), e2_ref[...])

    dt = dt_ref[0]
    t3 = _dot(tri, jnp.concatenate(_split3(dt * rate), axis=1))
    a_cs = t3[:, :128] + t3[:, 128:256] + t3[:, 256:]
    xdt = xg_ref[0].astype(F32) * expand(dt)
    xdt_b = xdt.astype(BF16)
    cbm = [jnp.where(lower, _dot_nt(c_ref[0, r, :], b_ref[0, r, :]), 0.0) for r in subs]
    a_end = jnp.concatenate([jnp.broadcast_to(a_cs[r.stop - 1:r.stop, :], (L, 128)) for r in subs], axis=0)
    xw = (xdt * expand(jnp.exp2(a_end - a_cs))).astype(BF16)
    grow = expand(jnp.exp2(a_cs))
    keep = expand(jnp.exp2(jnp.concatenate([a_end[r.start:r.start + 16, :] for r in subs], axis=0)))
    st = state_ref[g]
    y_off = []
    for s, r in enumerate(subs):
        y_off.append(_dot(c_ref[0, r, :], st.astype(BF16)) * grow[r, :])
        st = st * keep[16 * s:16 * s + 1, :] + _dot(b_ref[0, r, :].astype(F32).T.astype(BF16), xw[r, :])
    state_ref[g] = st
    for s, r in enumerate(subs):
        a_sub = a_cs[r, :]
        a_sub_t = a_sub.T
        for q in range(gw // pw):
            x2 = xdt_b[r, q * pw:(q + 1) * pw]
            yp = y_off[s][:, q * pw:(q + 1) * pw]
            for e in range(2):
                hd = 2 * q + e
                dec = jnp.exp2(jnp.minimum(a_sub[:, hd:hd + 1] - a_sub_t[hd:hd + 1, :], 0.0))
                xm = jnp.where(lo_half if e == 0 else jnp.logical_not(lo_half), x2, jnp.zeros_like(x2))
                yp = yp + _dot((cbm[s] * dec).astype(BF16), xm)
            y_ref[r, q * pw:(q + 1) * pw] = yp

    y = y_ref[...] + xg_ref[0].astype(F32) * dsk_ref[0]
    gt = y * sz_ref[0].astype(F32)
    ms = jnp.mean(gt * gt, axis=-1, keepdims=True)
    gn = gt * lax.rsqrt(ms + NORM_EPS) * ng_ref[0]
    contrib = _dot(gn.astype(BF16), wout_ref[pl.ds(pl.multiple_of(g * gw, gw), gw), :])

    @pl.when(g == 0)
    def _():
        macc_ref[...] = contrib

    @pl.when(g > 0)
    def _():
        macc_ref[...] += contrib

    @pl.when(g == SSM_GROUPS - 1)
    def _():
        x_new = x_ref[0] + g1_ref[0] * macc_ref[...]
        xn_ref[0] = x_new
        _post_mixer(x_new, g_ref, sc_ref, sh_ref, rwt_ref, rb_ref, h2x_ref, cls_ref)


def _ssd(xbc, sz, dt, alog, dskip, ng, wout, e2, x, g1, g, sc, sh, rwt, rb):
    bsz, s, d = x.shape
    L = SSM_CHUNK
    nc = s // L
    inner = sz.shape[2]
    gw = inner // SSM_GROUPS
    nbc = inner // SSM_STATE
    full = lambda shape: pl.BlockSpec(shape, lambda b, c, q: (0,) * len(shape))
    per_b = pl.BlockSpec((1, 1, d), lambda b, c, q: (b, 0, 0))
    per_g = lambda n: pl.BlockSpec((1, 1, n), lambda b, c, q: (q, 0, 0))
    return pl.pallas_call(
        _ssd_kernel,
        grid=(bsz, nc, SSM_GROUPS),
        in_specs=[pl.BlockSpec((1, L, gw), lambda b, c, q: (b, c, q)),
                  pl.BlockSpec((1, L, SSM_STATE), lambda b, c, q: (b, c, nbc + q)),
                  pl.BlockSpec((1, L, SSM_STATE), lambda b, c, q: (b, c, nbc + SSM_GROUPS + q)),
                  pl.BlockSpec((1, L, gw), lambda b, c, q: (b, c, q)),
                  pl.BlockSpec((1, L, 128), lambda b, c, q: (b, c, q)),
                  per_g(128), per_g(gw), per_g(gw), full(wout.shape), full(e2.shape),
                  pl.BlockSpec((1, L, d), lambda b, c, q: (b, c, 0)),
                  per_b, full(g.shape), per_b, per_b, full(rwt.shape), full(rb.shape)],
        out_specs=[pl.BlockSpec((1, L, d), lambda b, c, q: (b, c, 0)),
                   pl.BlockSpec((L, d + W_LANES), lambda b, c, q: (b * nc + c, 0)),
                   pl.BlockSpec((1, 1, L), lambda b, c, q: (b * nc + c, 0, 0))],
        out_shape=[jax.ShapeDtypeStruct((bsz, s, d), F32),
                   jax.ShapeDtypeStruct((bsz * s, d + W_LANES), F32),
                   jax.ShapeDtypeStruct((bsz * nc, 1, L), I32)],
        scratch_shapes=[pltpu.VMEM((SSM_GROUPS, SSM_STATE, gw), F32),
                        pltpu.VMEM((L, d), F32),
                        pltpu.VMEM((L, gw), F32)],
        compiler_params=_params(3), name="ssd_out_proj",
    )(xbc, xbc, xbc, sz, dt, alog, dskip, ng, wout, e2, x, g1, g, sc, sh, rwt, rb)


def _count_kernel(cls_ref, cnt_ref):
    @pl.when(pl.program_id(0) == 0)
    def _():
        cnt_ref[...] = jnp.zeros(cnt_ref.shape, F32)

    tr = cls_ref.shape[2]
    onehot = lax.broadcasted_iota(I32, (CLASS_ROWS, tr), 0) == cls_ref[0]
    cnt_ref[...] += jnp.sum(jnp.where(onehot, 1.0, 0.0), axis=1, keepdims=True)


def _dest_kernel(cls_ref, start_ref, dest_ref, run_ref):
    @pl.when(pl.program_id(0) == 0)
    def _():
        run_ref[...] = start_ref[...]

    tr = cls_ref.shape[2]
    onehot = lax.broadcasted_iota(I32, (CLASS_ROWS, tr), 0) == cls_ref[0]
    oh = jnp.where(onehot, 1.0, 0.0)
    upper = (lax.broadcasted_iota(I32, (tr, tr), 0) <= lax.broadcasted_iota(I32, (tr, tr), 1))
    prefix = _dot(oh.astype(BF16), jnp.where(upper, 1.0, 0.0).astype(BF16))
    dest = jnp.sum(oh * (prefix - 1.0 + run_ref[...]), axis=0, keepdims=True)
    dest_ref[0] = dest.astype(I32)
    run_ref[...] += jnp.sum(oh, axis=1, keepdims=True)


def _moe_plan(cls, tr):
    nt = cls.shape[0]
    t = nt * tr
    cnt = pl.pallas_call(
        _count_kernel, grid=(nt,),
        in_specs=[pl.BlockSpec((1, 1, tr), lambda i: (i, 0, 0))],
        out_specs=pl.BlockSpec((CLASS_ROWS, 1), lambda i: (0, 0)),
        out_shape=jax.ShapeDtypeStruct((CLASS_ROWS, 1), F32),
        compiler_params=_params(1), name="moe_count",
    )(cls)
    padded = jnp.ceil(cnt[:, 0] / MOE_TILE) * MOE_TILE
    end = jnp.cumsum(padded)
    start = end - padded
    dest = pl.pallas_call(
        _dest_kernel, grid=(nt,),
        in_specs=[pl.BlockSpec((1, 1, tr), lambda i: (i, 0, 0)),
                  pl.BlockSpec((CLASS_ROWS, 1), lambda i: (0, 0))],
        out_specs=pl.BlockSpec((1, 1, tr), lambda i: (i, 0, 0)),
        out_shape=jax.ShapeDtypeStruct((nt, 1, tr), I32),
        scratch_shapes=[pltpu.VMEM((CLASS_ROWS, 1), F32)],
        compiler_params=_params(1), name="moe_dest",
    )(cls, start.reshape(CLASS_ROWS, 1))
    n_tiles = t // MOE_TILE + N_CLASSES
    tile_row = jnp.arange(n_tiles, dtype=F32) * MOE_TILE
    total = end[N_CLASSES - 1]
    valid = tile_row < total
    tcls = jnp.sum((tile_row[:, None] >= end[None, :N_CLASSES]).astype(I32), axis=1)
    last = jnp.sum((total - MOE_TILE >= end[:N_CLASSES]).astype(I32))
    tcls = jnp.where(valid, tcls, last)
    grp = tcls // N_PAIRS
    pair = tcls % N_PAIRS
    lo = jnp.where(pair < 3, 0, jnp.where(pair < 5, 1, 2))
    hi = jnp.where(pair < 3, pair + 1, jnp.where(pair < 5, pair - 1, 3))
    meta = jnp.stack([grp * EXPERTS_PER_GROUP + lo, grp * EXPERTS_PER_GROUP + hi, valid.astype(I32)]).astype(I32)
    return dest.reshape(t), meta


def _scatter_kernel(dest_ref, src_ref, init_ref, out_ref, sem):
    del init_ref
    tm = src_ref.shape[0]
    base = pl.program_id(0) * tm

    def issue(r0, carry):
        for u in range(ROW_DMA_UNROLL):
            r = r0 * ROW_DMA_UNROLL + u
            pltpu.make_async_copy(src_ref.at[pl.ds(r, 1), :], out_ref.at[pl.ds(dest_ref[base + r], 1), :],
                                  sem).start(priority=u % 2)
        return carry

    lax.fori_loop(0, tm // ROW_DMA_UNROLL, issue, 0)
    pltpu.make_async_copy(src_ref, out_ref.at[pl.ds(0, tm), :], sem).wait()


def _scatter_rows(dest, src, n_slots, tm):
    t, w = src.shape
    init = jnp.zeros((n_slots, w), src.dtype)
    return pl.pallas_call(
        _scatter_kernel,
        grid_spec=pltpu.PrefetchScalarGridSpec(
            num_scalar_prefetch=1, grid=(t // tm,),
            in_specs=[pl.BlockSpec((tm, w), lambda i, d: (i, 0)),
                      pl.BlockSpec(memory_space=pl.ANY)],
            out_specs=pl.BlockSpec(memory_space=pl.ANY),
            scratch_shapes=[pltpu.SemaphoreType.DMA(())]),
        out_shape=jax.ShapeDtypeStruct((n_slots, w), src.dtype),
        input_output_aliases={2: 0},
        compiler_params=_params(1), name="moe_scatter",
    )(dest, src, init)


def _expert_kernel(meta_ref, xs_ref, wgu_a_ref, wgu_b_ref, wd_a_ref, wd_b_ref, y_ref):
    i = pl.program_id(0)
    d = y_ref.shape[1]

    @pl.when(meta_ref[2, i] == 0)
    def _():
        y_ref[...] = jnp.zeros(y_ref.shape, F32)

    @pl.when(meta_ref[2, i] != 0)
    def _():
        xb = xs_ref[:, :d].astype(BF16)
        wts = xs_ref[:, d:]
        y = jnp.zeros(y_ref.shape, F32)
        for e, (wgu_ref, wd_ref) in enumerate(((wgu_a_ref, wd_a_ref), (wgu_b_ref, wd_b_ref))):
            gu = _dot(xb, wgu_ref[0])
            ff = gu.shape[1] // 2
            act = _silu(gu[:, :ff]) * gu[:, ff:]
            y = y + wts[:, e:e + 1] * _dot(act.astype(BF16), wd_ref[0])
        y_ref[...] = y


def _experts(meta, xs, wgu, wd):
    n_slots, w = xs.shape
    d = w - W_LANES
    n_tiles = n_slots // MOE_TILE
    return pl.pallas_call(
        _expert_kernel,
        grid_spec=pltpu.PrefetchScalarGridSpec(
            num_scalar_prefetch=1, grid=(n_tiles,),
            in_specs=[pl.BlockSpec((MOE_TILE, w), lambda i, m: (i, 0)),
                      pl.BlockSpec((1,) + wgu.shape[1:], lambda i, m: (m[0, i], 0, 0)),
                      pl.BlockSpec((1,) + wgu.shape[1:], lambda i, m: (m[1, i], 0, 0)),
                      pl.BlockSpec((1,) + wd.shape[1:], lambda i, m: (m[0, i], 0, 0)),
                      pl.BlockSpec((1,) + wd.shape[1:], lambda i, m: (m[1, i], 0, 0))],
            out_specs=pl.BlockSpec((MOE_TILE, d), lambda i, m: (i, 0))),
        out_shape=jax.ShapeDtypeStruct((n_slots, d), F32),
        compiler_params=_params(1), name="moe_experts",
    )(meta, xs, wgu, wgu, wd, wd)


def _combine_kernel(dest_ref, ys_ref, x_ref, g2_ref, o_ref, buf_ref, sem):
    tm = x_ref.shape[1]
    base = (pl.program_id(0) * pl.num_programs(1) + pl.program_id(1)) * tm

    def issue(r0, carry):
        for u in range(ROW_DMA_UNROLL):
            r = r0 * ROW_DMA_UNROLL + u
            pltpu.make_async_copy(ys_ref.at[pl.ds(dest_ref[base + r], 1), :], buf_ref.at[pl.ds(r, 1), :],
                                  sem).start(priority=u % 2)
        return carry

    lax.fori_loop(0, tm // ROW_DMA_UNROLL, issue, 0)
    pltpu.make_async_copy(ys_ref.at[pl.ds(0, tm), :], buf_ref, sem).wait()
    o_ref[0] = x_ref[0] + g2_ref[0] * buf_ref[...]


def _combine(dest, ys, x, g2, tm):
    bsz, s, d = x.shape
    return pl.pallas_call(
        _combine_kernel,
        grid_spec=pltpu.PrefetchScalarGridSpec(
            num_scalar_prefetch=1, grid=(bsz, s // tm),
            in_specs=[pl.BlockSpec(memory_space=pl.ANY),
                      pl.BlockSpec((1, tm, d), lambda b, i, dd: (b, i, 0)),
                      pl.BlockSpec((1, 1, d), lambda b, i, dd: (b, 0, 0))],
            out_specs=pl.BlockSpec((1, tm, d), lambda b, i, dd: (b, i, 0)),
            scratch_shapes=[pltpu.VMEM((tm, d), F32), pltpu.SemaphoreType.DMA(())]),
        out_shape=jax.ShapeDtypeStruct((bsz, s, d), F32),
        compiler_params=_params(2), name="moe_combine",
    )(dest, ys, x, g2)


def _moe(x_new, h2x, cls, g2, wgu, wd):
    bsz, s, d = x_new.shape
    t = bsz * s
    tr = min(512, t)
    dest, meta = _moe_plan(cls.reshape(t // tr, 1, tr), tr)
    n_slots = t + N_CLASSES * MOE_TILE
    xs = _scatter_rows(dest, h2x, n_slots, min(256, t))
    ys = _experts(meta, xs, wgu, wd)
    return _combine(dest, ys, x_new, g2, min(256, s))


def kernel(x, c, ada_w, ada_b, norm_mix, norm_ffn, hy_w_in, hy_q_norm, hy_k_norm, hy_dw_w, hy_dw_b, hy_ln_g, hy_ln_b, hy_w_out, ssm_w_in, ssm_conv_w, ssm_conv_b, ssm_dt_bias, ssm_a_log, ssm_d, ssm_norm, ssm_w_out, router_w, router_bias, exp_w_gate, exp_w_up, exp_w_down):
    bsz, s, d = x.shape
    depth = ada_w.shape[0]
    assert s % 512 == 0 and bsz <= 8
    c8 = jnp.zeros((8, d), F32).at[:bsz].set(c)
    mod = _ada(c8, ada_w, ada_b)
    rwt = router_w.T
    rb = router_bias.reshape(N_EXPERTS, 1)
    lane_head = jnp.arange(ATT_WIDTH) // HEAD_DIM
    pmat = ((lane_head[:, None] == lane_head[None, :]).astype(F32) / HEAD_DIM).astype(BF16)

    for layer in range(depth):
        sh1, sc1, g1, sh2, sc2, g2 = (mod[layer, :bsz, i * d:(i + 1) * d].reshape(bsz, 1, d) for i in range(N_MOD))
        gm = norm_mix[layer].reshape(1, d)
        gf = norm_ffn[layer].reshape(1, d)
        j = layer // 2
        if layer % 2 == 0:
            w_in = hy_w_in[j].astype(BF16)
            aw = ATT_WIDTH
            k, qt, vt, u, kmean = _hy_in(
                x, sc1, sh1, gm, w_in[:, aw:2 * aw], w_in[:, 3 * aw:], w_in[:, :aw].T, w_in[:, 2 * aw:3 * aw].T,
                (jnp.tile(hy_q_norm[j], ATT_HEADS) * (HEAD_DIM ** -0.5 * LOG2E)).reshape(aw, 1),
                jnp.tile(hy_k_norm[j], ATT_HEADS).reshape(1, aw), pmat, 512)
            bias = _gate(kmean.reshape(bsz, s // MOBA_BLOCK, aw), qt, min(1024, s))
            att_t = _attn(qt, k, vt, bias, 8)
            w_out = hy_w_out[j].astype(BF16)
            dww = jnp.zeros((CONV_HALO, hy_dw_w.shape[2]), F32).at[:CONV_WIDTH].set(hy_dw_w[j])
            x_new, h2x, cls = _hy_out(
                u, att_t, x, g1, w_out[:aw], w_out[aw:], dww, hy_dw_b[j].reshape(1, -1),
                hy_ln_g[j].reshape(1, -1), hy_ln_b[j].reshape(1, -1), gf, sc2, sh2, rwt, rb, 256)
        else:
            w_in = ssm_w_in[j]
            inner = ssm_norm.shape[1]
            heads = ssm_a_log.shape[1]
            hpg = heads // SSM_GROUPS
            conv_dim = ssm_conv_w.shape[2]

            def by_group(v):
                v = v.reshape(v.shape[:-1] + (SSM_GROUPS, hpg))
                pad = [(0, 0)] * (v.ndim - 1) + [(0, 128 - hpg)]
                return jnp.pad(v, pad).reshape(v.shape[:-2] + (SSM_GROUPS * 128,))

            sz, xbc, dt = _ssm_in(
                x, sc1, sh1, gm, w_in[:, :inner].astype(BF16), w_in[:, inner:inner + conv_dim].astype(BF16),
                by_group(w_in[:, inner + conv_dim:]).astype(BF16), ssm_conv_w[j], ssm_conv_b[j].reshape(1, -1),
                by_group(ssm_dt_bias[j]).reshape(1, -1), 256)
            chan_head = jnp.arange(inner // SSM_GROUPS) // SSM_HEAD_DIM
            e2 = (jnp.arange(256)[:, None] % 128 == chan_head[None, :]).astype(BF16)
            x_new, h2x, cls = _ssd(
                xbc, sz, dt, by_group(ssm_a_log[j]).reshape(SSM_GROUPS, 1, 128),
                jnp.repeat(ssm_d[j], SSM_HEAD_DIM).reshape(SSM_GROUPS, 1, -1),
                ssm_norm[j].reshape(SSM_GROUPS, 1, -1), ssm_w_out[j].astype(BF16), e2, x, g1, gf, sc2, sh2, rwt, rb)
        wgu = jnp.concatenate([exp_w_gate[layer], exp_w_up[layer]], axis=-1).astype(BF16)
        x = _moe(x_new, h2x, cls, g2, wgu, exp_w_down[layer].astype(BF16))
    return x
```

```python
import functools

import jax
import jax.numpy as jnp
from jax import lax
from jax.experimental import pallas as pl
from jax.experimental.pallas import tpu as pltpu

F32 = jnp.float32
BF16 = jnp.bfloat16
I32 = jnp.int32

NORM_EPS = 1e-6
N_MOD = 6
HEAD_DIM = 64
ATT_HEADS = 8
ATT_WIDTH = ATT_HEADS * HEAD_DIM
MOBA_BLOCK = 256
MOBA_TOPK = 3
CONV_WIDTH = 31
CONV_HALO = 32
SSM_HEAD_DIM = 64
SSM_GROUPS = 4
SSM_STATE = 128
SSM_CONV = 4
SSM_CHUNK = 256
SSM_HALO = 8
SSD_SUB = 128
SSM_IN_COLS = 512
N_EXPERTS = 16
N_EXPERT_GROUPS = 4
EXPERTS_PER_GROUP = 4
N_PAIRS = 6
N_CLASSES = N_EXPERT_GROUPS * N_PAIRS
CLASS_ROWS = 32
MOE_TILE = 256
W_LANES = 128
ROW_DMA_UNROLL = 8
LOG2E = 1.4426950408889634
NEG = -1e30
VMEM_LIMIT = 56 * 1024 * 1024


def _params(n_axes):
    return pltpu.CompilerParams(dimension_semantics=("arbitrary",) * n_axes,
                                vmem_limit_bytes=VMEM_LIMIT)


def _dot(a, b):
    return jnp.dot(a, b, preferred_element_type=F32)


def _dot_nt(a, b):
    return lax.dot_general(a, b, (((1,), (1,)), ((), ())), preferred_element_type=F32)


def _split2(x):
    hi = x.astype(BF16)
    lo = (x - hi.astype(F32)).astype(BF16)
    return hi, lo


def _split3(x):
    a = x.astype(BF16)
    r = x - a.astype(F32)
    b = r.astype(BF16)
    c = (r - b.astype(F32)).astype(BF16)
    return a, b, c


def _silu(x):
    return x * jax.nn.sigmoid(x)


def _mod_norm(x, g, sc, sh):
    ms = jnp.mean(x * x, axis=-1, keepdims=True)
    return x * lax.rsqrt(ms + NORM_EPS) * g * (1.0 + sc) + sh


def _ada_kernel(c_ref, w_ref, b_ref, o_ref):
    cond = _silu(c_ref[...])
    ch, cl = _split2(cond)
    wh, wl = _split2(w_ref[0])
    o_ref[0] = _dot(ch, wh) + _dot(ch, wl) + _dot(cl, wh) + b_ref[0]


def _ada(c8, ada_w, ada_b):
    depth, d, n = ada_w.shape
    tn = 1536
    return pl.pallas_call(
        _ada_kernel,
        grid=(depth, n // tn),
        in_specs=[pl.BlockSpec((8, d), lambda l, j: (0, 0)),
                  pl.BlockSpec((1, d, tn), lambda l, j: (l, 0, j)),
                  pl.BlockSpec((1, 1, tn), lambda l, j: (l, 0, j))],
        out_specs=pl.BlockSpec((1, 8, tn), lambda l, j: (l, 0, j)),
        out_shape=jax.ShapeDtypeStruct((depth, 8, n), F32),
        compiler_params=_params(2), name="ada_mod",
    )(c8, ada_w, ada_b.reshape(depth, 1, n))


def _hy_in_kernel(x_ref, sc_ref, sh_ref, g_ref, wk_ref, wu_ref, wqt_ref, wvt_ref, qg_ref, kg_ref, p_ref,
                  k_ref, qt_ref, vt_ref, u_ref, km_ref):
    tm = x_ref.shape[1]
    h = _mod_norm(x_ref[0], g_ref[...], sc_ref[0], sh_ref[0])
    hb = h.astype(BF16)
    k = _dot(hb, wk_ref[...])
    khi, klo = _split2(k * k)
    ms = _dot(khi, p_ref[...]) + _dot(klo, p_ref[...])
    kn = k * lax.rsqrt(ms + NORM_EPS) * kg_ref[...]
    k_ref[0] = kn.astype(BF16)
    km_ref[0, 0] = jnp.mean(kn.reshape(tm // MOBA_BLOCK, MOBA_BLOCK, ATT_WIDTH), axis=1)
    qt = _dot_nt(wqt_ref[...], hb).reshape(ATT_HEADS, HEAD_DIM, tm)
    qms = jnp.mean(qt * qt, axis=1, keepdims=True)
    qn = (qt * lax.rsqrt(qms + NORM_EPS)).reshape(ATT_WIDTH, tm) * qg_ref[...]
    qt_ref[0] = qn.astype(BF16)
    vt_ref[0] = _dot_nt(wvt_ref[...], hb).astype(BF16)
    ag = _dot(hb, wu_ref[...])
    half = ag.shape[1] // 2
    u_ref[0] = (ag[:, :half] * jax.nn.sigmoid(ag[:, half:])).astype(BF16)


def _hy_in(x, sc, sh, g, wk, wu, wqt, wvt, qg, kg, pmat, tm):
    bsz, s, d = x.shape
    cc = wu.shape[1] // 2
    full = lambda shape: pl.BlockSpec(shape, lambda b, i: (0,) * len(shape))
    return pl.pallas_call(
        _hy_in_kernel,
        grid=(bsz, s // tm),
        in_specs=[pl.BlockSpec((1, tm, d), lambda b, i: (b, i, 0)),
                  pl.BlockSpec((1, 1, d), lambda b, i: (b, 0, 0)),
                  pl.BlockSpec((1, 1, d), lambda b, i: (b, 0, 0)),
                  full((1, d)), full(wk.shape), full(wu.shape), full(wqt.shape), full(wvt.shape),
                  full(qg.shape), full(kg.shape), full(pmat.shape)],
        out_specs=[pl.BlockSpec((1, tm, ATT_WIDTH), lambda b, i: (b, i, 0)),
                   pl.BlockSpec((1, ATT_WIDTH, tm), lambda b, i: (b, 0, i)),
                   pl.BlockSpec((1, ATT_WIDTH, tm), lambda b, i: (b, 0, i)),
                   pl.BlockSpec((1, tm, cc), lambda b, i: (b, i, 0)),
                   pl.BlockSpec((1, 1, tm // MOBA_BLOCK, ATT_WIDTH), lambda b, i: (b, i, 0, 0))],
        out_shape=[jax.ShapeDtypeStruct((bsz, s, ATT_WIDTH), BF16),
                   jax.ShapeDtypeStruct((bsz, ATT_WIDTH, s), BF16),
                   jax.ShapeDtypeStruct((bsz, ATT_WIDTH, s), BF16),
                   jax.ShapeDtypeStruct((bsz, s, cc), BF16),
                   jax.ShapeDtypeStruct((bsz, s // tm, tm // MOBA_BLOCK, ATT_WIDTH), F32)],
        compiler_params=_params(2), name="hy_in_proj",
    )(x, sc, sh, g, wk, wu, wqt, wvt, qg, kg, pmat)


def _gate_kernel(km_ref, qt_ref, o_ref):
    nb = km_ref.shape[1]
    ts = qt_ref.shape[2]
    rows = ATT_HEADS * nb
    km = km_ref[0]
    kmt = jnp.broadcast_to(km[None], (ATT_HEADS, nb, ATT_WIDTH)).reshape(rows, ATT_WIDTH)
    rh = lax.broadcasted_iota(I32, (rows, ATT_WIDTH), 0) // nb
    ch = lax.broadcasted_iota(I32, (rows, ATT_WIDTH), 1) // HEAD_DIM
    kbd = jnp.where(rh == ch, kmt, 0.0)
    khi, klo = _split2(kbd)
    qt = qt_ref[0]
    gate = (_dot(khi, qt) + _dot(klo, qt)).reshape(ATT_HEADS, nb, ts)
    blk = lax.broadcasted_iota(I32, (ATT_HEADS, nb, ts), 1)
    qblk = (pl.program_id(1) * ts + lax.broadcasted_iota(I32, (ATT_HEADS, nb, ts), 2)) // MOBA_BLOCK
    g = jnp.where(blk < qblk, gate, -jnp.inf)
    sel = jnp.zeros(g.shape, jnp.bool_)
    for _ in range(MOBA_TOPK):
        m = jnp.max(g, axis=1, keepdims=True)
        first = jnp.min(jnp.where((g == m) & (m > -jnp.inf), blk, nb), axis=1, keepdims=True)
        pick = blk == first
        sel = sel | pick
        g = jnp.where(pick, -jnp.inf, g)
    o_ref[0] = jnp.where(sel, 0.0, NEG).reshape(rows, ts)


def _gate(kmean, qt, ts):
    bsz, nb, _ = kmean.shape
    s = qt.shape[2]
    rows = ATT_HEADS * nb
    return pl.pallas_call(
        _gate_kernel,
        grid=(bsz, s // ts),
        in_specs=[pl.BlockSpec((1, nb, ATT_WIDTH), lambda b, i: (b, 0, 0)),
                  pl.BlockSpec((1, ATT_WIDTH, ts), lambda b, i: (b, 0, i))],
        out_specs=pl.BlockSpec((1, rows, ts), lambda b, i: (b, 0, i)),
        out_shape=jax.ShapeDtypeStruct((bsz, rows, s), F32),
        compiler_params=_params(2), name="moba_gate",
    )(kmean, qt)


def _attn_kernel(qt_ref, k_ref, vt_ref, bias_ref, o_ref, qa_ref, acc_ref, m_ref, *, nb, hp, lag, kb):
    blk = MOBA_BLOCK
    pw = 2 * HEAD_DIM
    va = HEAD_DIM + 16
    i = pl.program_id(2)
    row = lax.broadcasted_iota(I32, (pw, blk), 0)
    for h in range(hp):
        qp = qt_ref[0, (h // 2) * pw:(h // 2 + 1) * pw, :]
        keep = (row < HEAD_DIM) if h % 2 == 0 else (row >= HEAD_DIM)
        qa_ref[h, :pw, :] = jnp.where(keep, qp, jnp.zeros_like(qp))
        qa_ref[h, pw:pw + nb, :] = bias_ref[0, h * nb:(h + 1) * nb, :].astype(BF16)
        qa_ref[h, pw + nb:, :] = jnp.zeros((blk - pw - nb, blk), BF16)
        m_ref[h] = jnp.full((1, blk), NEG, F32)
        acc_ref[h] = jnp.zeros((va, blk), F32)
    causal = lax.broadcasted_iota(I32, (blk, blk), 0) <= lax.broadcasted_iota(I32, (blk, blk), 1)
    lane = lax.broadcasted_iota(I32, (blk, pw), 1)
    ones = jnp.ones((va - HEAD_DIM, blk), BF16)

    def step(blocks):
        units = [(b, h) for b in range(len(blocks)) for h in range(hp)]
        offs = [pl.multiple_of(j * blk, blk) for j, _, _ in blocks]

        def scores(u):
            b, h = units[u]
            pp = h // 2
            ka = jnp.concatenate([k_ref[0, pl.ds(offs[b], blk), pp * pw:(pp + 1) * pw], blocks[b][1]], axis=1)
            return blocks[b][2](_dot(ka, qa_ref[h]))

        def softmax(u, s):
            h = units[u][1]
            m_old = m_ref[h]
            m_new = jnp.maximum(m_old, jnp.max(s, axis=0, keepdims=True))
            m_ref[h] = m_new
            return jnp.exp2(s - m_new).astype(BF16), jnp.exp2(m_old - m_new)

        def accumulate(u, p, alpha):
            b, h = units[u]
            vj = jnp.concatenate([vt_ref[0, h * HEAD_DIM:(h + 1) * HEAD_DIM, pl.ds(offs[b], blk)], ones], axis=0)
            acc_ref[h] = alpha * acc_ref[h] + _dot(vj, p)

        s, pa = {}, {}
        for t in range(len(units) + lag):
            if t < len(units):
                s[t] = scores(t)
            if 1 <= t <= len(units):
                pa[t - 1] = softmax(t - 1, s.pop(t - 1))
            if t >= lag:
                accumulate(t - lag, *pa.pop(t - lag))

    def past(j):
        return (j, jnp.where(lane == j, 1.0, 0.0).astype(BF16), lambda s: s)

    step([(i, jnp.zeros((blk, pw), BF16), lambda s: jnp.where(causal, s, NEG))])
    rem = i % kb
    base = jnp.int32(0)
    size = 1
    while size < kb:
        take = (rem & size) != 0

        @pl.when(take)
        def _(base=base, size=size):
            step([past(base + t) for t in range(size)])

        base = base + jnp.where(take, size, 0)
        size *= 2

    def body(r, carry):
        step([past(rem + kb * r + t) for t in range(kb)])
        return carry

    lax.fori_loop(0, i // kb, body, 0)
    for h in range(hp):
        acc = acc_ref[h]
        o_ref[0, h * HEAD_DIM:(h + 1) * HEAD_DIM, :] = (acc[:HEAD_DIM] / acc[HEAD_DIM:HEAD_DIM + 1]).astype(BF16)


def _attn(qt, k, vt, bias, hp):
    bsz, s, _ = k.shape
    nb = s // MOBA_BLOCK
    assert nb <= MOBA_BLOCK - 2 * HEAD_DIM and nb % 16 == 0
    hw = hp * HEAD_DIM
    return pl.pallas_call(
        functools.partial(_attn_kernel, nb=nb, hp=hp, lag=6, kb=4),
        grid=(bsz, ATT_HEADS // hp, nb),
        in_specs=[pl.BlockSpec((1, hw, MOBA_BLOCK), lambda b, p, i: (b, p, i)),
                  pl.BlockSpec((1, s, hw), lambda b, p, i: (b, 0, p)),
                  pl.BlockSpec((1, hw, s), lambda b, p, i: (b, p, 0)),
                  pl.BlockSpec((1, hp * nb, MOBA_BLOCK), lambda b, p, i: (b, p, i))],
        out_specs=pl.BlockSpec((1, hw, MOBA_BLOCK), lambda b, p, i: (b, p, i)),
        out_shape=jax.ShapeDtypeStruct((bsz, ATT_WIDTH, s), BF16),
        scratch_shapes=[pltpu.VMEM((hp, MOBA_BLOCK, MOBA_BLOCK), BF16),
                        pltpu.VMEM((hp, HEAD_DIM + 16, MOBA_BLOCK), F32),
                        pltpu.VMEM((hp, 1, MOBA_BLOCK), F32)],
        compiler_params=_params(3), name="moba_attn",
    )(qt, k, vt, bias)


def _route(logit_t, rb):
    aff = jax.nn.sigmoid(logit_t)
    score = aff + rb
    s = [score[e:e + 1, :] for e in range(N_EXPERTS)]
    a = [aff[e:e + 1, :] for e in range(N_EXPERTS)]
    n = EXPERTS_PER_GROUP

    def top2_sum(v):
        best = None
        for x in range(n):
            for y in range(x + 1, n):
                t = v[x] + v[y]
                best = t if best is None else jnp.maximum(best, t)
        return best

    def first_argmax(v):
        idx = jnp.zeros(v[0].shape, I32)
        cur = v[0]
        for x in range(1, len(v)):
            better = v[x] > cur
            idx = jnp.where(better, x, idx)
            cur = jnp.where(better, v[x], cur)
        return idx

    grp = first_argmax([top2_sum(s[n * g:n * g + n]) for g in range(N_EXPERT_GROUPS)])

    def in_group(v, x):
        out = v[(N_EXPERT_GROUPS - 1) * n + x]
        for g in range(N_EXPERT_GROUPS - 2, -1, -1):
            out = jnp.where(grp == g, v[n * g + x], out)
        return out

    sg = [in_group(s, x) for x in range(n)]
    ag = [in_group(a, x) for x in range(n)]
    l1 = first_argmax(sg)
    l2 = first_argmax([jnp.where(l1 == x, -jnp.inf, sg[x]) for x in range(n)])
    lo = jnp.minimum(l1, l2)
    hi = jnp.maximum(l1, l2)
    pair = jnp.where(lo == 0, hi - 1, jnp.where(lo == 1, hi + 1, 5))
    a_lo = jnp.zeros_like(ag[0])
    a_hi = jnp.zeros_like(ag[0])
    for x in range(n):
        a_lo = jnp.where(lo == x, ag[x], a_lo)
        a_hi = jnp.where(hi == x, ag[x], a_hi)
    tot = a_lo + a_hi
    return grp * N_PAIRS + pair, a_lo / tot, a_hi / tot


def _post_mixer(x_new, g_ref, sc_ref, sh_ref, rwt_ref, rb_ref, h2x_ref, cls_ref, cnt_ref):
    tm, d = x_new.shape
    h2 = _mod_norm(x_new, g_ref[...], sc_ref[0], sh_ref[0])
    hh, hl = _split2(h2)
    rh, rl = _split2(rwt_ref[...])
    logit_t = _dot_nt(rh, hh) + _dot_nt(rh, hl) + _dot_nt(rl, hh)
    cls, w_lo, w_hi = _route(logit_t, rb_ref[...])
    cls_ref[0] = cls

    @pl.when((pl.program_id(0) == 0) & (pl.program_id(1) == 0))
    def _():
        cnt_ref[...] = jnp.zeros(cnt_ref.shape, F32)

    onehot = lax.broadcasted_iota(I32, (CLASS_ROWS, tm), 0) == cls
    cnt_ref[...] += jnp.sum(jnp.where(onehot, 1.0, 0.0), axis=1, keepdims=True)
    wrow = lax.broadcasted_iota(I32, (W_LANES, tm), 0)
    wt = jnp.where(wrow == 0, w_lo, jnp.where(wrow == 1, w_hi, 0.0))
    h2x_ref[:, :d] = h2
    h2x_ref[:, d:] = wt.T


def _hy_out_kernel(u_ref, up_ref, at_ref, x_ref, g1_ref, wtop_ref, wbot_ref, dww_ref, dwb_ref, lng_ref, lnb_ref,
                   g_ref, sc_ref, sh_ref, rwt_ref, rb_ref,
                   xn_ref, h2x_ref, cls_ref, cnt_ref, cat_ref, shifted_ref):
    tm = u_ref.shape[1]
    m_att = _dot(at_ref[0].astype(F32).T.astype(BF16), wtop_ref[...])
    prev = up_ref[0].astype(F32)
    cat_ref[:CONV_HALO, :] = jnp.where(pl.program_id(1) == 0, 0.0, prev)
    cat_ref[CONV_HALO:, :] = u_ref[0].astype(F32)
    span = tm + CONV_HALO - 8
    for r in range(1, 8):
        shifted_ref[r - 1] = cat_ref[pl.ds(r, span), :]
    y = jnp.zeros((tm, u_ref.shape[2]), F32) + dwb_ref[...]
    for j in range(CONV_WIDTH):
        a, r = divmod(CONV_HALO - CONV_WIDTH + 1 + j, 8)
        tap = cat_ref[pl.ds(8 * a, tm), :] if r == 0 else shifted_ref[r - 1, pl.ds(8 * a, tm), :]
        y = y + dww_ref[j:j + 1, :] * tap
    mu = jnp.mean(y, axis=-1, keepdims=True)
    var = jnp.mean(jnp.square(y - mu), axis=-1, keepdims=True)
    cv = _silu((y - mu) * lax.rsqrt(var + NORM_EPS) * lng_ref[...] + lnb_ref[...])
    m = m_att + _dot(cv.astype(BF16), wbot_ref[...])
    x_new = x_ref[0] + g1_ref[0] * m
    xn_ref[0] = x_new
    _post_mixer(x_new, g_ref, sc_ref, sh_ref, rwt_ref, rb_ref, h2x_ref, cls_ref, cnt_ref)


def _hy_out(u, att_t, x, g1, wtop, wbot, dww, dwb, lng, lnb, g, sc, sh, rwt, rb, tm):
    bsz, s, d = x.shape
    cc = u.shape[2]
    nt = s // tm
    full = lambda shape: pl.BlockSpec(shape, lambda b, i: (0,) * len(shape))
    per_b = pl.BlockSpec((1, 1, d), lambda b, i: (b, 0, 0))
    halo = tm // CONV_HALO
    return pl.pallas_call(
        _hy_out_kernel,
        grid=(bsz, nt),
        in_specs=[pl.BlockSpec((1, tm, cc), lambda b, i: (b, i, 0)),
                  pl.BlockSpec((1, CONV_HALO, cc), lambda b, i: (b, jnp.maximum(i * halo - 1, 0), 0)),
                  pl.BlockSpec((1, ATT_WIDTH, tm), lambda b, i: (b, 0, i)),
                  pl.BlockSpec((1, tm, d), lambda b, i: (b, i, 0)),
                  per_b, full(wtop.shape), full(wbot.shape), full(dww.shape), full(dwb.shape),
                  full(lng.shape), full(lnb.shape), full(g.shape), per_b, per_b, full(rwt.shape), full(rb.shape)],
        out_specs=[pl.BlockSpec((1, tm, d), lambda b, i: (b, i, 0)),
                   pl.BlockSpec((tm, d + W_LANES), lambda b, i: (b * nt + i, 0)),
                   pl.BlockSpec((1, 1, tm), lambda b, i: (b * nt + i, 0, 0)),
                   pl.BlockSpec((CLASS_ROWS, 128), lambda b, i: (0, 0))],
        out_shape=[jax.ShapeDtypeStruct((bsz, s, d), F32),
                   jax.ShapeDtypeStruct((bsz * s, d + W_LANES), F32),
                   jax.ShapeDtypeStruct((bsz * nt, 1, tm), I32),
                   jax.ShapeDtypeStruct((CLASS_ROWS, 128), F32)],
        scratch_shapes=[pltpu.VMEM((tm + CONV_HALO, cc), F32),
                        pltpu.VMEM((7, tm + CONV_HALO - 8, cc), F32)],
        compiler_params=_params(2), name="hy_out_proj",
    )(u, u, att_t, x, g1, wtop, wbot, dww, dwb, lng, lnb, g, sc, sh, rwt, rb)


def _ssm_in_kernel(x_ref, sc_ref, sh_ref, g_ref, wz_ref, wx_ref, wdt_ref, cw_ref, cb_ref, dtb_ref,
                   sz_ref, xbc_ref, dt_ref, *cat_refs):
    tm = x_ref.shape[1]
    ch = SSM_IN_COLS
    h = _mod_norm(x_ref[0], g_ref[...], sc_ref[0], sh_ref[0])
    hb = h.astype(BF16)

    @pl.when(pl.program_id(1) == 0)
    def _():
        for cat_ref in cat_refs:
            cat_ref[:SSM_HALO, :] = jnp.zeros((SSM_HALO, ch), F32)

    def conv_matmul(c):
        cat_refs[c][SSM_HALO:, :] = _dot(hb, wx_ref[:, c * ch:(c + 1) * ch])

    def conv_act(c, _):
        cols = slice(c * ch, (c + 1) * ch)
        cat_ref = cat_refs[c]
        y = jnp.zeros((tm, ch), F32) + cb_ref[:, cols]
        for j in range(SSM_CONV):
            y = y + cw_ref[j:j + 1, cols] * cat_ref[pl.ds(SSM_HALO - SSM_CONV + 1 + j, tm), :]
        xbc_ref[0, :, cols] = _silu(y).astype(BF16)
        cat_ref[:SSM_HALO, :] = cat_ref[pl.ds(tm, SSM_HALO), :]

    def gate_matmul(c):
        return _dot(hb, wz_ref[:, c * ch:(c + 1) * ch])

    def gate_act(c, z):
        sz_ref[0, :, c * ch:(c + 1) * ch] = _silu(z).astype(BF16)

    stages = [(conv_matmul, conv_act, c) for c in range(wx_ref.shape[1] // ch)]
    stages += [(gate_matmul, gate_act, c) for c in range(wz_ref.shape[1] // ch)]
    pending = None
    for matmul, act, c in stages:
        out = matmul(c)
        if pending is not None:
            pending[0](pending[1], pending[2])
        pending = (act, c, out)
    t = _dot(hb, wdt_ref[...]) + dtb_ref[...]
    pending[0](pending[1], pending[2])
    dt_ref[0] = jnp.maximum(t, 0.0) + jnp.log(1.0 + jnp.exp(-jnp.abs(t)))


def _ssm_in(x, sc, sh, g, wz, wx, wdt, cw, cb, dtb, tm):
    bsz, s, d = x.shape
    full = lambda shape: pl.BlockSpec(shape, lambda b, i: (0,) * len(shape))
    per_b = pl.BlockSpec((1, 1, d), lambda b, i: (b, 0, 0))
    tile = lambda n: pl.BlockSpec((1, tm, n), lambda b, i: (b, i, 0))
    return pl.pallas_call(
        _ssm_in_kernel,
        grid=(bsz, s // tm),
        in_specs=[tile(d), per_b, per_b, full(g.shape), full(wz.shape), full(wx.shape), full(wdt.shape),
                  full(cw.shape), full(cb.shape), full(dtb.shape)],
        out_specs=[tile(wz.shape[1]), tile(wx.shape[1]), tile(wdt.shape[1])],
        out_shape=[jax.ShapeDtypeStruct((bsz, s, wz.shape[1]), BF16),
                   jax.ShapeDtypeStruct((bsz, s, wx.shape[1]), BF16),
                   jax.ShapeDtypeStruct((bsz, s, wdt.shape[1]), F32)],
        scratch_shapes=[pltpu.VMEM((tm + SSM_HALO, SSM_IN_COLS), F32)] * (wx.shape[1] // SSM_IN_COLS),
        compiler_params=_params(2), name="ssm_in_proj",
    )(x, sc, sh, g, wz, wx, wdt, cw, cb, dtb)


def _ssd_kernel(xbc_ref, sz_ref, dt_ref, alog_ref, dsk_ref, ng_ref, wout_ref, e2_ref, x_ref, g1_ref,
                g_ref, sc_ref, sh_ref, rwt_ref, rb_ref,
                xn_ref, h2x_ref, cls_ref, cnt_ref, state_ref, macc_ref, y_ref):
    L = SSD_SUB
    inner = sz_ref.shape[2]
    gw = inner // SSM_GROUPS
    pw = 2 * SSM_HEAD_DIM

    @pl.when(pl.program_id(1) == 0)
    def _():
        state_ref[...] = jnp.zeros(state_ref.shape, F32)

    subs = [slice(s * L, (s + 1) * L) for s in range(SSM_CHUNK // L)]
    brow = lax.broadcasted_iota(I32, (SSM_CHUNK, SSM_CHUNK), 0)
    bcol = lax.broadcasted_iota(I32, (SSM_CHUNK, SSM_CHUNK), 1)
    tri = jnp.where((brow // L == bcol // L) & (brow >= bcol), 1.0, 0.0).astype(BF16)
    lower = lax.broadcasted_iota(I32, (L, L), 0) >= lax.broadcasted_iota(I32, (L, L), 1)
    lo_half = lax.broadcasted_iota(I32, (L, pw), 1) < SSM_HEAD_DIM

    def expand(v):
        return _dot(jnp.concatenate(_split2(v), axis=1), e2_ref[...])

    macc_ref[...] = jnp.zeros(macc_ref.shape, F32)

    def group(g, carry):
        xcol = pl.ds(pl.multiple_of(g * gw, gw), gw)
        bcolumns = pl.ds(pl.multiple_of(inner + g * SSM_STATE, SSM_STATE), SSM_STATE)
        ccolumns = pl.ds(pl.multiple_of(inner + (SSM_GROUPS + g) * SSM_STATE, SSM_STATE), SSM_STATE)
        rate = -jnp.exp(alog_ref[g]) * LOG2E
        dt = dt_ref[0, :, pl.ds(pl.multiple_of(g * 128, 128), 128)]
        t3 = _dot(tri, jnp.concatenate(_split3(dt * rate), axis=1))
        a_cs = t3[:, :128] + t3[:, 128:256] + t3[:, 256:]
        xg = xbc_ref[0, :, xcol].astype(F32)
        xdt = xg * expand(dt)
        xdt_b = xdt.astype(BF16)
        bs = [xbc_ref[0, r, bcolumns] for r in subs]
        cs = [xbc_ref[0, r, ccolumns] for r in subs]
        cbm = [jnp.where(lower, _dot_nt(cm, bm), 0.0) for cm, bm in zip(cs, bs)]
        a_end = jnp.concatenate([jnp.broadcast_to(a_cs[r.stop - 1:r.stop, :], (L, 128)) for r in subs], axis=0)
        xw = (xdt * expand(jnp.exp2(a_end - a_cs))).astype(BF16)
        grow = expand(jnp.exp2(a_cs))
        keep = expand(jnp.exp2(jnp.concatenate([a_end[r.start:r.start + 16, :] for r in subs], axis=0)))
        st = state_ref[g]
        y_off = []
        for s, r in enumerate(subs):
            y_off.append(_dot(cs[s], st.astype(BF16)) * grow[r, :])
            st = st * keep[16 * s:16 * s + 1, :] + _dot(bs[s].astype(F32).T.astype(BF16), xw[r, :])
        state_ref[g] = st
        for s, r in enumerate(subs):
            a_sub = a_cs[r, :]
            a_sub_t = a_sub.T
            for q in range(gw // pw):
                x2 = xdt_b[r, q * pw:(q + 1) * pw]
                yp = y_off[s][:, q * pw:(q + 1) * pw]
                for e in range(2):
                    hd = 2 * q + e
                    dec = jnp.exp2(jnp.minimum(a_sub[:, hd:hd + 1] - a_sub_t[hd:hd + 1, :], 0.0))
                    xm = jnp.where(lo_half if e == 0 else jnp.logical_not(lo_half), x2, jnp.zeros_like(x2))
                    yp = yp + _dot((cbm[s] * dec).astype(BF16), xm)
                y_ref[r, q * pw:(q + 1) * pw] = yp
        gt = (y_ref[...] + xg * dsk_ref[g]) * sz_ref[0, :, xcol].astype(F32)
        ms = jnp.mean(gt * gt, axis=-1, keepdims=True)
        gn = gt * lax.rsqrt(ms + NORM_EPS) * ng_ref[g]
        macc_ref[...] += _dot(gn.astype(BF16), wout_ref[xcol, :])
        return carry

    lax.fori_loop(0, SSM_GROUPS, group, 0)
    x_new = x_ref[0] + g1_ref[0] * macc_ref[...]
    xn_ref[0] = x_new
    _post_mixer(x_new, g_ref, sc_ref, sh_ref, rwt_ref, rb_ref, h2x_ref, cls_ref, cnt_ref)


def _ssd(xbc, sz, dt, alog, dskip, ng, wout, e2, x, g1, g, sc, sh, rwt, rb):
    bsz, s, d = x.shape
    L = SSM_CHUNK
    nc = s // L
    gw = sz.shape[2] // SSM_GROUPS
    full = lambda shape: pl.BlockSpec(shape, lambda b, c: (0,) * len(shape))
    per_b = pl.BlockSpec((1, 1, d), lambda b, c: (b, 0, 0))
    tile = lambda n: pl.BlockSpec((1, L, n), lambda b, c: (b, c, 0))
    return pl.pallas_call(
        _ssd_kernel,
        grid=(bsz, nc),
        in_specs=[tile(xbc.shape[2]), tile(sz.shape[2]), tile(dt.shape[2]),
                  full(alog.shape), full(dskip.shape), full(ng.shape), full(wout.shape), full(e2.shape),
                  tile(d), per_b, full(g.shape), per_b, per_b, full(rwt.shape), full(rb.shape)],
        out_specs=[tile(d),
                   pl.BlockSpec((L, d + W_LANES), lambda b, c: (b * nc + c, 0)),
                   pl.BlockSpec((1, 1, L), lambda b, c: (b * nc + c, 0, 0)),
                   pl.BlockSpec((CLASS_ROWS, 128), lambda b, c: (0, 0))],
        out_shape=[jax.ShapeDtypeStruct((bsz, s, d), F32),
                   jax.ShapeDtypeStruct((bsz * s, d + W_LANES), F32),
                   jax.ShapeDtypeStruct((bsz * nc, 1, L), I32),
                   jax.ShapeDtypeStruct((CLASS_ROWS, 128), F32)],
        scratch_shapes=[pltpu.VMEM((SSM_GROUPS, SSM_STATE, gw), F32),
                        pltpu.VMEM((L, d), F32),
                        pltpu.VMEM((L, gw), F32)],
        compiler_params=_params(2), name="ssd_out_proj",
    )(xbc, sz, dt, alog, dskip, ng, wout, e2, x, g1, g, sc, sh, rwt, rb)


def _dest_kernel(cls_ref, start_ref, dest_ref, run_ref):
    @pl.when(pl.program_id(0) == 0)
    def _():
        run_ref[...] = start_ref[...]

    tr = cls_ref.shape[2]
    upper = lax.broadcasted_iota(I32, (tr, tr), 0) <= lax.broadcasted_iota(I32, (tr, tr), 1)
    upper = jnp.where(upper, 1.0, 0.0).astype(BF16)
    for k in range(cls_ref.shape[0]):
        onehot = lax.broadcasted_iota(I32, (CLASS_ROWS, tr), 0) == cls_ref[k]
        oh = jnp.where(onehot, 1.0, 0.0)
        prefix = _dot(oh.astype(BF16), upper)
        dest = jnp.sum(oh * (prefix - 1.0 + run_ref[...]), axis=0, keepdims=True)
        dest_ref[k] = dest.astype(I32)
        run_ref[...] += jnp.sum(oh, axis=1, keepdims=True)


def _moe_plan(cls, cnt, tr):
    nt = cls.shape[0]
    t = nt * tr
    reps = 4 if nt % 4 == 0 else 1
    padded = jnp.ceil(cnt[:, 0] / MOE_TILE) * MOE_TILE
    end = jnp.cumsum(padded)
    start = end - padded
    dest = pl.pallas_call(
        _dest_kernel, grid=(nt // reps,),
        in_specs=[pl.BlockSpec((reps, 1, tr), lambda i: (i, 0, 0)),
                  pl.BlockSpec((CLASS_ROWS, 1), lambda i: (0, 0))],
        out_specs=pl.BlockSpec((reps, 1, tr), lambda i: (i, 0, 0)),
        out_shape=jax.ShapeDtypeStruct((nt, 1, tr), I32),
        scratch_shapes=[pltpu.VMEM((CLASS_ROWS, 1), F32)],
        compiler_params=_params(1), name="moe_dest",
    )(cls, start.reshape(CLASS_ROWS, 1))
    n_tiles = t // MOE_TILE + N_CLASSES
    tile_row = jnp.arange(n_tiles, dtype=F32) * MOE_TILE
    total = end[N_CLASSES - 1]
    valid = tile_row < total
    tcls = jnp.sum((tile_row[:, None] >= end[None, :N_CLASSES]).astype(I32), axis=1)
    last = jnp.sum((total - MOE_TILE >= end[:N_CLASSES]).astype(I32))
    tcls = jnp.where(valid, tcls, last)
    grp = tcls // N_PAIRS
    pair = tcls % N_PAIRS
    lo = jnp.where(pair < 3, 0, jnp.where(pair < 5, 1, 2))
    hi = jnp.where(pair < 3, pair + 1, jnp.where(pair < 5, pair - 1, 3))
    meta = jnp.stack([grp * EXPERTS_PER_GROUP + lo, grp * EXPERTS_PER_GROUP + hi, valid.astype(I32)]).astype(I32)
    return dest.reshape(t), meta


def _scatter_kernel(dest_ref, src_ref, init_ref, out_ref, sem):
    del init_ref
    tm = src_ref.shape[0]
    base = pl.program_id(0) * tm

    def issue(r0, carry):
        for u in range(ROW_DMA_UNROLL):
            r = r0 * ROW_DMA_UNROLL + u
            pltpu.make_async_copy(src_ref.at[pl.ds(r, 1), :], out_ref.at[pl.ds(dest_ref[base + r], 1), :],
                                  sem).start(priority=u % 2)
        return carry

    lax.fori_loop(0, tm // ROW_DMA_UNROLL, issue, 0)
    pltpu.make_async_copy(src_ref, out_ref.at[pl.ds(0, tm), :], sem).wait()


def _scatter_rows(dest, src, n_slots, tm):
    t, w = src.shape
    init = jnp.zeros((n_slots, w), src.dtype)
    return pl.pallas_call(
        _scatter_kernel,
        grid_spec=pltpu.PrefetchScalarGridSpec(
            num_scalar_prefetch=1, grid=(t // tm,),
            in_specs=[pl.BlockSpec((tm, w), lambda i, d: (i, 0)),
                      pl.BlockSpec(memory_space=pl.ANY)],
            out_specs=pl.BlockSpec(memory_space=pl.ANY),
            scratch_shapes=[pltpu.SemaphoreType.DMA(())]),
        out_shape=jax.ShapeDtypeStruct((n_slots, w), src.dtype),
        input_output_aliases={2: 0},
        compiler_params=_params(1), name="moe_scatter",
    )(dest, src, init)


def _expert_kernel(meta_ref, xs_ref, wgu_a_ref, wgu_b_ref, wd_a_ref, wd_b_ref, y_ref):
    i = pl.program_id(0)
    d = y_ref.shape[1]

    @pl.when(meta_ref[2, i] == 0)
    def _():
        y_ref[...] = jnp.zeros(y_ref.shape, F32)

    @pl.when(meta_ref[2, i] != 0)
    def _():
        xb = xs_ref[:, :d].astype(BF16)
        wts = xs_ref[:, d:]
        y = jnp.zeros(y_ref.shape, F32)
        for e, (wgu_ref, wd_ref) in enumerate(((wgu_a_ref, wd_a_ref), (wgu_b_ref, wd_b_ref))):
            gu = _dot(xb, wgu_ref[0])
            ff = gu.shape[1] // 2
            act = _silu(gu[:, :ff]) * gu[:, ff:]
            y = y + wts[:, e:e + 1] * _dot(act.astype(BF16), wd_ref[0])
        y_ref[...] = y


def _experts(meta, xs, wgu, wd):
    n_slots, w = xs.shape
    d = w - W_LANES
    n_tiles = n_slots // MOE_TILE
    return pl.pallas_call(
        _expert_kernel,
        grid_spec=pltpu.PrefetchScalarGridSpec(
            num_scalar_prefetch=1, grid=(n_tiles,),
            in_specs=[pl.BlockSpec((MOE_TILE, w), lambda i, m: (i, 0)),
                      pl.BlockSpec((1,) + wgu.shape[1:], lambda i, m: (m[0, i], 0, 0)),
                      pl.BlockSpec((1,) + wgu.shape[1:], lambda i, m: (m[1, i], 0, 0)),
                      pl.BlockSpec((1,) + wd.shape[1:], lambda i, m: (m[0, i], 0, 0)),
                      pl.BlockSpec((1,) + wd.shape[1:], lambda i, m: (m[1, i], 0, 0))],
            out_specs=pl.BlockSpec((MOE_TILE, d), lambda i, m: (i, 0))),
        out_shape=jax.ShapeDtypeStruct((n_slots, d), F32),
        compiler_params=_params(1), name="moe_experts",
    )(meta, xs, wgu, wgu, wd, wd)


def _combine_kernel(dest_ref, ys_ref, x_ref, g2_ref, o_ref, buf_ref, sem):
    tm = x_ref.shape[1]
    base = (pl.program_id(0) * pl.num_programs(1) + pl.program_id(1)) * tm

    def issue(r0, carry):
        for u in range(ROW_DMA_UNROLL):
            r = r0 * ROW_DMA_UNROLL + u
            pltpu.make_async_copy(ys_ref.at[pl.ds(dest_ref[base + r], 1), :], buf_ref.at[pl.ds(r, 1), :],
                                  sem).start(priority=u % 2)
        return carry

    lax.fori_loop(0, tm // ROW_DMA_UNROLL, issue, 0)
    pltpu.make_async_copy(ys_ref.at[pl.ds(0, tm), :], buf_ref, sem).wait()
    o_ref[0] = x_ref[0] + g2_ref[0] * buf_ref[...]


def _combine(dest, ys, x, g2, tm):
    bsz, s, d = x.shape
    return pl.pallas_call(
        _combine_kernel,
        grid_spec=pltpu.PrefetchScalarGridSpec(
            num_scalar_prefetch=1, grid=(bsz, s // tm),
            in_specs=[pl.BlockSpec(memory_space=pl.ANY),
                      pl.BlockSpec((1, tm, d), lambda b, i, dd: (b, i, 0)),
                      pl.BlockSpec((1, 1, d), lambda b, i, dd: (b, 0, 0))],
            out_specs=pl.BlockSpec((1, tm, d), lambda b, i, dd: (b, i, 0)),
            scratch_shapes=[pltpu.VMEM((tm, d), F32), pltpu.SemaphoreType.DMA(())]),
        out_shape=jax.ShapeDtypeStruct((bsz, s, d), F32),
        compiler_params=_params(2), name="moe_combine",
    )(dest, ys, x, g2)


def _moe(x_new, h2x, cls, cnt, g2, wgu, wd):
    bsz, s, d = x_new.shape
    t = bsz * s
    tr = min(512, t)
    dest, meta = _moe_plan(cls.reshape(t // tr, 1, tr), cnt, tr)
    n_slots = t + N_CLASSES * MOE_TILE
    xs = _scatter_rows(dest, h2x, n_slots, min(256, t))
    ys = _experts(meta, xs, wgu, wd)
    return _combine(dest, ys, x_new, g2, min(256, s))


def kernel(x, c, ada_w, ada_b, norm_mix, norm_ffn, hy_w_in, hy_q_norm, hy_k_norm, hy_dw_w, hy_dw_b, hy_ln_g, hy_ln_b, hy_w_out, ssm_w_in, ssm_conv_w, ssm_conv_b, ssm_dt_bias, ssm_a_log, ssm_d, ssm_norm, ssm_w_out, router_w, router_bias, exp_w_gate, exp_w_up, exp_w_down):
    bsz, s, d = x.shape
    depth = ada_w.shape[0]
    assert s % 512 == 0 and bsz <= 8
    c8 = jnp.zeros((8, d), F32).at[:bsz].set(c)
    mod = _ada(c8, ada_w, ada_b)
    rwt = router_w.T
    rb = router_bias.reshape(N_EXPERTS, 1)
    lane_head = jnp.arange(ATT_WIDTH) // HEAD_DIM
    pmat = ((lane_head[:, None] == lane_head[None, :]).astype(F32) / HEAD_DIM).astype(BF16)

    for layer in range(depth):
        sh1, sc1, g1, sh2, sc2, g2 = (mod[layer, :bsz, i * d:(i + 1) * d].reshape(bsz, 1, d) for i in range(N_MOD))
        gm = norm_mix[layer].reshape(1, d)
        gf = norm_ffn[layer].reshape(1, d)
        j = layer // 2
        if layer % 2 == 0:
            w_in = hy_w_in[j].astype(BF16)
            aw = ATT_WIDTH
            k, qt, vt, u, kmean = _hy_in(
                x, sc1, sh1, gm, w_in[:, aw:2 * aw], w_in[:, 3 * aw:], w_in[:, :aw].T, w_in[:, 2 * aw:3 * aw].T,
                (jnp.tile(hy_q_norm[j], ATT_HEADS) * (HEAD_DIM ** -0.5 * LOG2E)).reshape(aw, 1),
                jnp.tile(hy_k_norm[j], ATT_HEADS).reshape(1, aw), pmat, 512)
            bias = _gate(kmean.reshape(bsz, s // MOBA_BLOCK, aw), qt, min(1024, s))
            att_t = _attn(qt, k, vt, bias, 8)
            w_out = hy_w_out[j].astype(BF16)
            dww = jnp.zeros((CONV_HALO, hy_dw_w.shape[2]), F32).at[:CONV_WIDTH].set(hy_dw_w[j])
            x_new, h2x, cls, cnt = _hy_out(
                u, att_t, x, g1, w_out[:aw], w_out[aw:], dww, hy_dw_b[j].reshape(1, -1),
                hy_ln_g[j].reshape(1, -1), hy_ln_b[j].reshape(1, -1), gf, sc2, sh2, rwt, rb, 256)
        else:
            w_in = ssm_w_in[j]
            inner = ssm_norm.shape[1]
            heads = ssm_a_log.shape[1]
            hpg = heads // SSM_GROUPS
            conv_dim = ssm_conv_w.shape[2]

            def by_group(v):
                v = v.reshape(v.shape[:-1] + (SSM_GROUPS, hpg))
                pad = [(0, 0)] * (v.ndim - 1) + [(0, 128 - hpg)]
                return jnp.pad(v, pad).reshape(v.shape[:-2] + (SSM_GROUPS * 128,))

            sz, xbc, dt = _ssm_in(
                x, sc1, sh1, gm, w_in[:, :inner].astype(BF16), w_in[:, inner:inner + conv_dim].astype(BF16),
                by_group(w_in[:, inner + conv_dim:]).astype(BF16), ssm_conv_w[j], ssm_conv_b[j].reshape(1, -1),
                by_group(ssm_dt_bias[j]).reshape(1, -1), 256)
            chan_head = jnp.arange(inner // SSM_GROUPS) // SSM_HEAD_DIM
            e2 = (jnp.arange(256)[:, None] % 128 == chan_head[None, :]).astype(BF16)
            x_new, h2x, cls, cnt = _ssd(
                xbc, sz, dt, by_group(ssm_a_log[j]).reshape(SSM_GROUPS, 1, 128),
                jnp.repeat(ssm_d[j], SSM_HEAD_DIM).reshape(SSM_GROUPS, 1, -1),
                ssm_norm[j].reshape(SSM_GROUPS, 1, -1), ssm_w_out[j].astype(BF16), e2, x, g1, gf, sc2, sh2, rwt, rb)
        wgu = jnp.concatenate([exp_w_gate[layer], exp_w_up[layer]], axis=-1).astype(BF16)
        x = _moe(x_new, h2x, cls, cnt, g2, wgu, exp_w_down[layer].astype(BF16))
    return x
```

```python
import functools

import jax
import jax.numpy as jnp
from jax import lax
from jax.experimental import pallas as pl
from jax.experimental.pallas import tpu as pltpu

F32 = jnp.float32
BF16 = jnp.bfloat16
I32 = jnp.int32

NORM_EPS = 1e-6
N_MOD = 6
HEAD_DIM = 64
ATT_HEADS = 8
ATT_WIDTH = ATT_HEADS * HEAD_DIM
MOBA_BLOCK = 256
MOBA_TOPK = 3
CONV_WIDTH = 31
CONV_HALO = 32
SSM_HEAD_DIM = 64
SSM_GROUPS = 4
SSM_STATE = 128
SSM_CONV = 4
SSM_CHUNK = 256
SSM_HALO = 8
SSD_SUB = 128
SSM_IN_COLS = 512
N_EXPERTS = 16
N_EXPERT_GROUPS = 4
EXPERTS_PER_GROUP = 4
N_PAIRS = 6
N_CLASSES = N_EXPERT_GROUPS * N_PAIRS
CLASS_ROWS = 32
MOE_TILE = 256
W_LANES = 128
ROW_DMA_UNROLL = 8
LOG2E = 1.4426950408889634
NEG = -1e30
VMEM_LIMIT = 56 * 1024 * 1024


def _params(n_axes):
    return pltpu.CompilerParams(dimension_semantics=("arbitrary",) * n_axes,
                                vmem_limit_bytes=VMEM_LIMIT)


def _dot(a, b):
    return jnp.dot(a, b, preferred_element_type=F32)


def _dot_nt(a, b):
    return lax.dot_general(a, b, (((1,), (1,)), ((), ())), preferred_element_type=F32)


def _split2(x):
    hi = x.astype(BF16)
    lo = (x - hi.astype(F32)).astype(BF16)
    return hi, lo


def _split3(x):
    a = x.astype(BF16)
    r = x - a.astype(F32)
    b = r.astype(BF16)
    c = (r - b.astype(F32)).astype(BF16)
    return a, b, c


def _silu(x):
    return x * jax.nn.sigmoid(x)


def _mod_norm(x, g, sc, sh):
    ms = jnp.mean(x * x, axis=-1, keepdims=True)
    return x * lax.rsqrt(ms + NORM_EPS) * g * (1.0 + sc) + sh


def _ada_kernel(c_ref, w_ref, b_ref, o_ref):
    cond = _silu(c_ref[...])
    ch, cl = _split2(cond)
    wh, wl = _split2(w_ref[0])
    o_ref[0] = _dot(ch, wh) + _dot(ch, wl) + _dot(cl, wh) + b_ref[0]


def _ada(c8, ada_w, ada_b):
    depth, d, n = ada_w.shape
    tn = 1536
    return pl.pallas_call(
        _ada_kernel,
        grid=(depth, n // tn),
        in_specs=[pl.BlockSpec((8, d), lambda l, j: (0, 0)),
                  pl.BlockSpec((1, d, tn), lambda l, j: (l, 0, j)),
                  pl.BlockSpec((1, 1, tn), lambda l, j: (l, 0, j))],
        out_specs=pl.BlockSpec((1, 8, tn), lambda l, j: (l, 0, j)),
        out_shape=jax.ShapeDtypeStruct((depth, 8, n), F32),
        compiler_params=_params(2), name="ada_mod",
    )(c8, ada_w, ada_b.reshape(depth, 1, n))


def _hy_in_kernel(x_ref, sc_ref, sh_ref, g_ref, wk_ref, wu_ref, wqt_ref, wvt_ref, qg_ref, kg_ref, p_ref,
                  k_ref, qt_ref, vt_ref, u_ref, km_ref):
    tm = x_ref.shape[1]
    h = _mod_norm(x_ref[0], g_ref[...], sc_ref[0], sh_ref[0])
    hb = h.astype(BF16)
    k = _dot(hb, wk_ref[...])
    khi, klo = _split2(k * k)
    ms = _dot(khi, p_ref[...]) + _dot(klo, p_ref[...])
    kn = k * lax.rsqrt(ms + NORM_EPS) * kg_ref[...]
    k_ref[0] = kn.astype(BF16)
    km_ref[0, 0] = jnp.mean(kn.reshape(tm // MOBA_BLOCK, MOBA_BLOCK, ATT_WIDTH), axis=1)
    qt = _dot_nt(wqt_ref[...], hb).reshape(ATT_HEADS, HEAD_DIM, tm)
    qms = jnp.mean(qt * qt, axis=1, keepdims=True)
    qn = (qt * lax.rsqrt(qms + NORM_EPS)).reshape(ATT_WIDTH, tm) * qg_ref[...]
    qt_ref[0] = qn.astype(BF16)
    vt_ref[0] = _dot_nt(wvt_ref[...], hb).astype(BF16)
    ag = _dot(hb, wu_ref[...])
    half = ag.shape[1] // 2
    u_ref[0] = (ag[:, :half] * jax.nn.sigmoid(ag[:, half:])).astype(BF16)


def _hy_in(x, sc, sh, g, wk, wu, wqt, wvt, qg, kg, pmat, tm):
    bsz, s, d = x.shape
    cc = wu.shape[1] // 2
    full = lambda shape: pl.BlockSpec(shape, lambda b, i: (0,) * len(shape))
    return pl.pallas_call(
        _hy_in_kernel,
        grid=(bsz, s // tm),
        in_specs=[pl.BlockSpec((1, tm, d), lambda b, i: (b, i, 0)),
                  pl.BlockSpec((1, 1, d), lambda b, i: (b, 0, 0)),
                  pl.BlockSpec((1, 1, d), lambda b, i: (b, 0, 0)),
                  full((1, d)), full(wk.shape), full(wu.shape), full(wqt.shape), full(wvt.shape),
                  full(qg.shape), full(kg.shape), full(pmat.shape)],
        out_specs=[pl.BlockSpec((1, tm, ATT_WIDTH), lambda b, i: (b, i, 0)),
                   pl.BlockSpec((1, ATT_WIDTH, tm), lambda b, i: (b, 0, i)),
                   pl.BlockSpec((1, ATT_WIDTH, tm), lambda b, i: (b, 0, i)),
                   pl.BlockSpec((1, tm, cc), lambda b, i: (b, i, 0)),
                   pl.BlockSpec((1, 1, tm // MOBA_BLOCK, ATT_WIDTH), lambda b, i: (b, i, 0, 0))],
        out_shape=[jax.ShapeDtypeStruct((bsz, s, ATT_WIDTH), BF16),
                   jax.ShapeDtypeStruct((bsz, ATT_WIDTH, s), BF16),
                   jax.ShapeDtypeStruct((bsz, ATT_WIDTH, s), BF16),
                   jax.ShapeDtypeStruct((bsz, s, cc), BF16),
                   jax.ShapeDtypeStruct((bsz, s // tm, tm // MOBA_BLOCK, ATT_WIDTH), F32)],
        compiler_params=_params(2), name="hy_in_proj",
    )(x, sc, sh, g, wk, wu, wqt, wvt, qg, kg, pmat)


def _gate_kernel(km_ref, qt_ref, o_ref):
    nb = km_ref.shape[1]
    ts = qt_ref.shape[2]
    rows = ATT_HEADS * nb
    km = km_ref[0]
    kmt = jnp.broadcast_to(km[None], (ATT_HEADS, nb, ATT_WIDTH)).reshape(rows, ATT_WIDTH)
    rh = lax.broadcasted_iota(I32, (rows, ATT_WIDTH), 0) // nb
    ch = lax.broadcasted_iota(I32, (rows, ATT_WIDTH), 1) // HEAD_DIM
    kbd = jnp.where(rh == ch, kmt, 0.0)
    khi, klo = _split2(kbd)
    qt = qt_ref[0]
    gate = (_dot(khi, qt) + _dot(klo, qt)).reshape(ATT_HEADS, nb, ts)
    blk = lax.broadcasted_iota(I32, (ATT_HEADS, nb, ts), 1)
    qblk = (pl.program_id(1) * ts + lax.broadcasted_iota(I32, (ATT_HEADS, nb, ts), 2)) // MOBA_BLOCK
    g = jnp.where(blk < qblk, gate, -jnp.inf)
    sel = jnp.zeros(g.shape, jnp.bool_)
    for _ in range(MOBA_TOPK):
        m = jnp.max(g, axis=1, keepdims=True)
        first = jnp.min(jnp.where((g == m) & (m > -jnp.inf), blk, nb), axis=1, keepdims=True)
        pick = blk == first
        sel = sel | pick
        g = jnp.where(pick, -jnp.inf, g)
    o_ref[0] = jnp.where(sel, 0.0, NEG).reshape(rows, ts)


def _gate(kmean, qt, ts):
    bsz, nb, _ = kmean.shape
    s = qt.shape[2]
    rows = ATT_HEADS * nb
    return pl.pallas_call(
        _gate_kernel,
        grid=(bsz, s // ts),
        in_specs=[pl.BlockSpec((1, nb, ATT_WIDTH), lambda b, i: (b, 0, 0)),
                  pl.BlockSpec((1, ATT_WIDTH, ts), lambda b, i: (b, 0, i))],
        out_specs=pl.BlockSpec((1, rows, ts), lambda b, i: (b, 0, i)),
        out_shape=jax.ShapeDtypeStruct((bsz, rows, s), F32),
        compiler_params=_params(2), name="moba_gate",
    )(kmean, qt)


def _attn_kernel(qt_ref, k_ref, vt_ref, bias_ref, o_ref, qa_ref, acc_ref, m_ref, *, nb, hp, lag, kb):
    blk = MOBA_BLOCK
    pw = 2 * HEAD_DIM
    va = HEAD_DIM + 16
    i = pl.program_id(2)
    row = lax.broadcasted_iota(I32, (pw, blk), 0)
    for h in range(hp):
        qp = qt_ref[0, (h // 2) * pw:(h // 2 + 1) * pw, :]
        keep = (row < HEAD_DIM) if h % 2 == 0 else (row >= HEAD_DIM)
        qa_ref[h, :pw, :] = jnp.where(keep, qp, jnp.zeros_like(qp))
        qa_ref[h, pw:pw + nb, :] = bias_ref[0, h * nb:(h + 1) * nb, :].astype(BF16)
        qa_ref[h, pw + nb:, :] = jnp.zeros((blk - pw - nb, blk), BF16)
        m_ref[h] = jnp.full((1, blk), NEG, F32)
        acc_ref[h] = jnp.zeros((va, blk), F32)
    causal = lax.broadcasted_iota(I32, (blk, blk), 0) <= lax.broadcasted_iota(I32, (blk, blk), 1)
    lane = lax.broadcasted_iota(I32, (blk, pw), 1)
    ones = jnp.ones((va - HEAD_DIM, blk), BF16)

    def step(blocks):
        units = [(b, h) for b in range(len(blocks)) for h in range(hp)]
        offs = [pl.multiple_of(j * blk, blk) for j, _, _ in blocks]

        def scores(u):
            b, h = units[u]
            pp = h // 2
            ka = jnp.concatenate([k_ref[0, pl.ds(offs[b], blk), pp * pw:(pp + 1) * pw], blocks[b][1]], axis=1)
            return blocks[b][2](_dot(ka, qa_ref[h]))

        def softmax(u, s):
            h = units[u][1]
            m_old = m_ref[h]
            m_new = jnp.maximum(m_old, jnp.max(s, axis=0, keepdims=True))
            m_ref[h] = m_new
            return jnp.exp2(s - m_new).astype(BF16), jnp.exp2(m_old - m_new)

        def accumulate(u, p, alpha):
            b, h = units[u]
            vj = jnp.concatenate([vt_ref[0, h * HEAD_DIM:(h + 1) * HEAD_DIM, pl.ds(offs[b], blk)], ones], axis=0)
            acc_ref[h] = alpha * acc_ref[h] + _dot(vj, p)

        s, pa = {}, {}
        for t in range(len(units) + lag):
            if t < len(units):
                s[t] = scores(t)
            if 1 <= t <= len(units):
                pa[t - 1] = softmax(t - 1, s.pop(t - 1))
            if t >= lag:
                accumulate(t - lag, *pa.pop(t - lag))

    def past(j):
        return (j, jnp.where(lane == j, 1.0, 0.0).astype(BF16), lambda s: s)

    step([(i, jnp.zeros((blk, pw), BF16), lambda s: jnp.where(causal, s, NEG))])
    rem = i % kb
    base = jnp.int32(0)
    size = 1
    while size < kb:
        take = (rem & size) != 0

        @pl.when(take)
        def _(base=base, size=size):
            step([past(base + t) for t in range(size)])

        base = base + jnp.where(take, size, 0)
        size *= 2

    def body(r, carry):
        step([past(rem + kb * r + t) for t in range(kb)])
        return carry

    lax.fori_loop(0, i // kb, body, 0)
    for h in range(hp):
        acc = acc_ref[h]
        o_ref[0, h * HEAD_DIM:(h + 1) * HEAD_DIM, :] = (acc[:HEAD_DIM] / acc[HEAD_DIM:HEAD_DIM + 1]).astype(BF16)


def _attn(qt, k, vt, bias, hp):
    bsz, s, _ = k.shape
    nb = s // MOBA_BLOCK
    assert nb <= MOBA_BLOCK - 2 * HEAD_DIM and nb % 16 == 0
    hw = hp * HEAD_DIM
    return pl.pallas_call(
        functools.partial(_attn_kernel, nb=nb, hp=hp, lag=6, kb=4),
        grid=(bsz, ATT_HEADS // hp, nb),
        in_specs=[pl.BlockSpec((1, hw, MOBA_BLOCK), lambda b, p, i: (b, p, i)),
                  pl.BlockSpec((1, s, hw), lambda b, p, i: (b, 0, p)),
                  pl.BlockSpec((1, hw, s), lambda b, p, i: (b, p, 0)),
                  pl.BlockSpec((1, hp * nb, MOBA_BLOCK), lambda b, p, i: (b, p, i))],
        out_specs=pl.BlockSpec((1, hw, MOBA_BLOCK), lambda b, p, i: (b, p, i)),
        out_shape=jax.ShapeDtypeStruct((bsz, ATT_WIDTH, s), BF16),
        scratch_shapes=[pltpu.VMEM((hp, MOBA_BLOCK, MOBA_BLOCK), BF16),
                        pltpu.VMEM((hp, HEAD_DIM + 16, MOBA_BLOCK), F32),
                        pltpu.VMEM((hp, 1, MOBA_BLOCK), F32)],
        compiler_params=_params(3), name="moba_attn",
    )(qt, k, vt, bias)


def _route(logit_t, rb):
    aff = jax.nn.sigmoid(logit_t)
    score = aff + rb
    s = [score[e:e + 1, :] for e in range(N_EXPERTS)]
    a = [aff[e:e + 1, :] for e in range(N_EXPERTS)]
    n = EXPERTS_PER_GROUP

    def top2_sum(v):
        best = None
        for x in range(n):
            for y in range(x + 1, n):
                t = v[x] + v[y]
                best = t if best is None else jnp.maximum(best, t)
        return best

    def first_argmax(v):
        idx = jnp.zeros(v[0].shape, I32)
        cur = v[0]
        for x in range(1, len(v)):
            better = v[x] > cur
            idx = jnp.where(better, x, idx)
            cur = jnp.where(better, v[x], cur)
        return idx

    grp = first_argmax([top2_sum(s[n * g:n * g + n]) for g in range(N_EXPERT_GROUPS)])

    def in_group(v, x):
        out = v[(N_EXPERT_GROUPS - 1) * n + x]
        for g in range(N_EXPERT_GROUPS - 2, -1, -1):
            out = jnp.where(grp == g, v[n * g + x], out)
        return out

    sg = [in_group(s, x) for x in range(n)]
    ag = [in_group(a, x) for x in range(n)]
    l1 = first_argmax(sg)
    l2 = first_argmax([jnp.where(l1 == x, -jnp.inf, sg[x]) for x in range(n)])
    lo = jnp.minimum(l1, l2)
    hi = jnp.maximum(l1, l2)
    pair = jnp.where(lo == 0, hi - 1, jnp.where(lo == 1, hi + 1, 5))
    a_lo = jnp.zeros_like(ag[0])
    a_hi = jnp.zeros_like(ag[0])
    for x in range(n):
        a_lo = jnp.where(lo == x, ag[x], a_lo)
        a_hi = jnp.where(hi == x, ag[x], a_hi)
    tot = a_lo + a_hi
    return grp * N_PAIRS + pair, a_lo / tot, a_hi / tot


def _post_mixer(x_new, g_ref, sc_ref, sh_ref, rwt_ref, rb_ref, h2x_ref, cls_ref, cnt_ref):
    tm, d = x_new.shape
    h2 = _mod_norm(x_new, g_ref[...], sc_ref[0], sh_ref[0])
    hh, hl = _split2(h2)
    rh, rl = _split2(rwt_ref[...])
    logit_t = _dot_nt(rh, hh) + _dot_nt(rh, hl) + _dot_nt(rl, hh)
    cls, w_lo, w_hi = _route(logit_t, rb_ref[...])
    cls_ref[0] = cls

    @pl.when((pl.program_id(0) == 0) & (pl.program_id(1) == 0))
    def _():
        cnt_ref[...] = jnp.zeros(cnt_ref.shape, F32)

    onehot = lax.broadcasted_iota(I32, (CLASS_ROWS, tm), 0) == cls
    cnt_ref[...] += jnp.sum(jnp.where(onehot, 1.0, 0.0), axis=1, keepdims=True)
    wrow = lax.broadcasted_iota(I32, (W_LANES, tm), 0)
    wt = jnp.where(wrow == 0, w_lo, jnp.where(wrow == 1, w_hi, 0.0))
    h2x_ref[:, :d] = h2
    h2x_ref[:, d:] = wt.T


def _hy_out_kernel(u_ref, up_ref, at_ref, x_ref, g1_ref, wtop_ref, wbot_ref, dww_ref, dwb_ref, lng_ref, lnb_ref,
                   g_ref, sc_ref, sh_ref, rwt_ref, rb_ref,
                   xn_ref, h2x_ref, cls_ref, cnt_ref, cat_ref, shifted_ref):
    tm = u_ref.shape[1]
    m_att = _dot(at_ref[0].astype(F32).T.astype(BF16), wtop_ref[...])
    prev = up_ref[0].astype(F32)
    cat_ref[:CONV_HALO, :] = jnp.where(pl.program_id(1) == 0, 0.0, prev)
    cat_ref[CONV_HALO:, :] = u_ref[0].astype(F32)
    span = tm + CONV_HALO - 8
    for r in range(1, 8):
        shifted_ref[r - 1] = cat_ref[pl.ds(r, span), :]
    y = jnp.zeros((tm, u_ref.shape[2]), F32) + dwb_ref[...]
    for j in range(CONV_WIDTH):
        a, r = divmod(CONV_HALO - CONV_WIDTH + 1 + j, 8)
        tap = cat_ref[pl.ds(8 * a, tm), :] if r == 0 else shifted_ref[r - 1, pl.ds(8 * a, tm), :]
        y = y + dww_ref[j:j + 1, :] * tap
    mu = jnp.mean(y, axis=-1, keepdims=True)
    var = jnp.mean(jnp.square(y - mu), axis=-1, keepdims=True)
    cv = _silu((y - mu) * lax.rsqrt(var + NORM_EPS) * lng_ref[...] + lnb_ref[...])
    m = m_att + _dot(cv.astype(BF16), wbot_ref[...])
    x_new = x_ref[0] + g1_ref[0] * m
    xn_ref[0] = x_new
    _post_mixer(x_new, g_ref, sc_ref, sh_ref, rwt_ref, rb_ref, h2x_ref, cls_ref, cnt_ref)


def _hy_out(u, att_t, x, g1, wtop, wbot, dww, dwb, lng, lnb, g, sc, sh, rwt, rb, tm):
    bsz, s, d = x.shape
    cc = u.shape[2]
    nt = s // tm
    full = lambda shape: pl.BlockSpec(shape, lambda b, i: (0,) * len(shape))
    per_b = pl.BlockSpec((1, 1, d), lambda b, i: (b, 0, 0))
    halo = tm // CONV_HALO
    return pl.pallas_call(
        _hy_out_kernel,
        grid=(bsz, nt),
        in_specs=[pl.BlockSpec((1, tm, cc), lambda b, i: (b, i, 0)),
                  pl.BlockSpec((1, CONV_HALO, cc), lambda b, i: (b, jnp.maximum(i * halo - 1, 0), 0)),
                  pl.BlockSpec((1, ATT_WIDTH, tm), lambda b, i: (b, 0, i)),
                  pl.BlockSpec((1, tm, d), lambda b, i: (b, i, 0)),
                  per_b, full(wtop.shape), full(wbot.shape), full(dww.shape), full(dwb.shape),
                  full(lng.shape), full(lnb.shape), full(g.shape), per_b, per_b, full(rwt.shape), full(rb.shape)],
        out_specs=[pl.BlockSpec((1, tm, d), lambda b, i: (b, i, 0)),
                   pl.BlockSpec((tm, d + W_LANES), lambda b, i: (b * nt + i, 0)),
                   pl.BlockSpec((1, 1, tm), lambda b, i: (b * nt + i, 0, 0)),
                   pl.BlockSpec((CLASS_ROWS, 128), lambda b, i: (0, 0))],
        out_shape=[jax.ShapeDtypeStruct((bsz, s, d), F32),
                   jax.ShapeDtypeStruct((bsz * s, d + W_LANES), F32),
                   jax.ShapeDtypeStruct((bsz * nt, 1, tm), I32),
                   jax.ShapeDtypeStruct((CLASS_ROWS, 128), F32)],
        scratch_shapes=[pltpu.VMEM((tm + CONV_HALO, cc), F32),
                        pltpu.VMEM((7, tm + CONV_HALO - 8, cc), F32)],
        compiler_params=_params(2), name="hy_out_proj",
    )(u, u, att_t, x, g1, wtop, wbot, dww, dwb, lng, lnb, g, sc, sh, rwt, rb)


def _ssm_in_kernel(x_ref, sc_ref, sh_ref, g_ref, wz_ref, wx_ref, wdt_ref, cw_ref, cb_ref, dtb_ref,
                   sz_ref, xbc_ref, dt_ref, *cat_refs):
    tm = x_ref.shape[1]
    ch = SSM_IN_COLS
    h = _mod_norm(x_ref[0], g_ref[...], sc_ref[0], sh_ref[0])
    hb = h.astype(BF16)

    @pl.when(pl.program_id(1) == 0)
    def _():
        for cat_ref in cat_refs:
            cat_ref[:SSM_HALO, :] = jnp.zeros((SSM_HALO, ch), F32)

    def conv_matmul(c):
        cat_refs[c][SSM_HALO:, :] = _dot(hb, wx_ref[:, c * ch:(c + 1) * ch])

    def conv_act(c, _):
        cols = slice(c * ch, (c + 1) * ch)
        cat_ref = cat_refs[c]
        y = jnp.zeros((tm, ch), F32) + cb_ref[:, cols]
        for j in range(SSM_CONV):
            y = y + cw_ref[j:j + 1, cols] * cat_ref[pl.ds(SSM_HALO - SSM_CONV + 1 + j, tm), :]
        xbc_ref[0, :, cols] = _silu(y).astype(BF16)
        cat_ref[:SSM_HALO, :] = cat_ref[pl.ds(tm, SSM_HALO), :]

    def gate_matmul(c):
        return _dot(hb, wz_ref[:, c * ch:(c + 1) * ch])

    def gate_act(c, z):
        sz_ref[0, :, c * ch:(c + 1) * ch] = _silu(z).astype(BF16)

    stages = [(conv_matmul, conv_act, c) for c in range(wx_ref.shape[1] // ch)]
    stages += [(gate_matmul, gate_act, c) for c in range(wz_ref.shape[1] // ch)]
    pending = None
    for matmul, act, c in stages:
        out = matmul(c)
        if pending is not None:
            pending[0](pending[1], pending[2])
        pending = (act, c, out)
    t = _dot(hb, wdt_ref[...]) + dtb_ref[...]
    pending[0](pending[1], pending[2])
    dt_ref[0] = jnp.maximum(t, 0.0) + jnp.log(1.0 + jnp.exp(-jnp.abs(t)))


def _ssm_in(x, sc, sh, g, wz, wx, wdt, cw, cb, dtb, tm):
    bsz, s, d = x.shape
    full = lambda shape: pl.BlockSpec(shape, lambda b, i: (0,) * len(shape))
    per_b = pl.BlockSpec((1, 1, d), lambda b, i: (b, 0, 0))
    tile = lambda n: pl.BlockSpec((1, tm, n), lambda b, i: (b, i, 0))
    return pl.pallas_call(
        _ssm_in_kernel,
        grid=(bsz, s // tm),
        in_specs=[tile(d), per_b, per_b, full(g.shape), full(wz.shape), full(wx.shape), full(wdt.shape),
                  full(cw.shape), full(cb.shape), full(dtb.shape)],
        out_specs=[tile(wz.shape[1]), tile(wx.shape[1]), tile(wdt.shape[1])],
        out_shape=[jax.ShapeDtypeStruct((bsz, s, wz.shape[1]), BF16),
                   jax.ShapeDtypeStruct((bsz, s, wx.shape[1]), BF16),
                   jax.ShapeDtypeStruct((bsz, s, wdt.shape[1]), F32)],
        scratch_shapes=[pltpu.VMEM((tm + SSM_HALO, SSM_IN_COLS), F32)] * (wx.shape[1] // SSM_IN_COLS),
        compiler_params=_params(2), name="ssm_in_proj",
    )(x, sc, sh, g, wz, wx, wdt, cw, cb, dtb)


def _ssd_kernel(xbc_ref, sz_ref, dt_ref, alog_ref, dsk_ref, ng_ref, wout_ref, e2_ref, x_ref, g1_ref,
                g_ref, sc_ref, sh_ref, rwt_ref, rb_ref,
                xn_ref, h2x_ref, cls_ref, cnt_ref, state_ref, macc_ref, y_ref):
    L = SSD_SUB
    inner = sz_ref.shape[2]
    gw = inner // SSM_GROUPS
    pw = 2 * SSM_HEAD_DIM

    @pl.when(pl.program_id(1) == 0)
    def _():
        state_ref[...] = jnp.zeros(state_ref.shape, F32)

    subs = [slice(s * L, (s + 1) * L) for s in range(SSM_CHUNK // L)]
    brow = lax.broadcasted_iota(I32, (SSM_CHUNK, SSM_CHUNK), 0)
    bcol = lax.broadcasted_iota(I32, (SSM_CHUNK, SSM_CHUNK), 1)
    tri = jnp.where((brow // L == bcol // L) & (brow >= bcol), 1.0, 0.0).astype(BF16)
    lower = lax.broadcasted_iota(I32, (L, L), 0) >= lax.broadcasted_iota(I32, (L, L), 1)
    lo_half = lax.broadcasted_iota(I32, (L, pw), 1) < SSM_HEAD_DIM

    def expand(v):
        return _dot(jnp.concatenate(_split2(v), axis=1), e2_ref[...])

    macc_ref[...] = jnp.zeros(macc_ref.shape, F32)

    def group(g, carry):
        xcol = pl.ds(pl.multiple_of(g * gw, gw), gw)
        bcolumns = pl.ds(pl.multiple_of(inner + g * SSM_STATE, SSM_STATE), SSM_STATE)
        ccolumns = pl.ds(pl.multiple_of(inner + (SSM_GROUPS + g) * SSM_STATE, SSM_STATE), SSM_STATE)
        rate = -jnp.exp(alog_ref[g]) * LOG2E
        dt = dt_ref[0, :, pl.ds(pl.multiple_of(g * 128, 128), 128)]
        t3 = _dot(tri, jnp.concatenate(_split3(dt * rate), axis=1))
        a_cs = t3[:, :128] + t3[:, 128:256] + t3[:, 256:]
        xg = xbc_ref[0, :, xcol].astype(F32)
        xdt = xg * expand(dt)
        xdt_b = xdt.astype(BF16)
        bs = [xbc_ref[0, r, bcolumns] for r in subs]
        cs = [xbc_ref[0, r, ccolumns] for r in subs]
        cbm = [jnp.where(lower, _dot_nt(cm, bm), 0.0) for cm, bm in zip(cs, bs)]
        a_end = jnp.concatenate([jnp.broadcast_to(a_cs[r.stop - 1:r.stop, :], (L, 128)) for r in subs], axis=0)
        xw = (xdt * expand(jnp.exp2(a_end - a_cs))).astype(BF16)
        grow = expand(jnp.exp2(a_cs))
        keep = expand(jnp.exp2(jnp.concatenate([a_end[r.start:r.start + 16, :] for r in subs], axis=0)))
        st = state_ref[g]
        y_off = []
        for s, r in enumerate(subs):
            y_off.append(_dot(cs[s], st.astype(BF16)) * grow[r, :])
            st = st * keep[16 * s:16 * s + 1, :] + _dot(bs[s].astype(F32).T.astype(BF16), xw[r, :])
        state_ref[g] = st
        for s, r in enumerate(subs):
            a_sub = a_cs[r, :]
            a_sub_t = a_sub.T
            for q in range(gw // pw):
                x2 = xdt_b[r, q * pw:(q + 1) * pw]
                yp = y_off[s][:, q * pw:(q + 1) * pw]
                for e in range(2):
                    hd = 2 * q + e
                    dec = jnp.exp2(jnp.minimum(a_sub[:, hd:hd + 1] - a_sub_t[hd:hd + 1, :], 0.0))
                    xm = jnp.where(lo_half if e == 0 else jnp.logical_not(lo_half), x2, jnp.zeros_like(x2))
                    yp = yp + _dot((cbm[s] * dec).astype(BF16), xm)
                y_ref[r, q * pw:(q + 1) * pw] = yp
        gt = (y_ref[...] + xg * dsk_ref[g]) * sz_ref[0, :, xcol].astype(F32)
        ms = jnp.mean(gt * gt, axis=-1, keepdims=True)
        gn = gt * lax.rsqrt(ms + NORM_EPS) * ng_ref[g]
        macc_ref[...] += _dot(gn.astype(BF16), wout_ref[xcol, :])
        return carry

    lax.fori_loop(0, SSM_GROUPS, group, 0)
    x_new = x_ref[0] + g1_ref[0] * macc_ref[...]
    xn_ref[0] = x_new
    _post_mixer(x_new, g_ref, sc_ref, sh_ref, rwt_ref, rb_ref, h2x_ref, cls_ref, cnt_ref)


def _ssd(xbc, sz, dt, alog, dskip, ng, wout, e2, x, g1, g, sc, sh, rwt, rb):
    bsz, s, d = x.shape
    L = SSM_CHUNK
    nc = s // L
    gw = sz.shape[2] // SSM_GROUPS
    full = lambda shape: pl.BlockSpec(shape, lambda b, c: (0,) * len(shape))
    per_b = pl.BlockSpec((1, 1, d), lambda b, c: (b, 0, 0))
    tile = lambda n: pl.BlockSpec((1, L, n), lambda b, c: (b, c, 0))
    return pl.pallas_call(
        _ssd_kernel,
        grid=(bsz, nc),
        in_specs=[tile(xbc.shape[2]), tile(sz.shape[2]), tile(dt.shape[2]),
                  full(alog.shape), full(dskip.shape), full(ng.shape), full(wout.shape), full(e2.shape),
                  tile(d), per_b, full(g.shape), per_b, per_b, full(rwt.shape), full(rb.shape)],
        out_specs=[tile(d),
                   pl.BlockSpec((L, d + W_LANES), lambda b, c: (b * nc + c, 0)),
                   pl.BlockSpec((1, 1, L), lambda b, c: (b * nc + c, 0, 0)),
                   pl.BlockSpec((CLASS_ROWS, 128), lambda b, c: (0, 0))],
        out_shape=[jax.ShapeDtypeStruct((bsz, s, d), F32),
                   jax.ShapeDtypeStruct((bsz * s, d + W_LANES), F32),
                   jax.ShapeDtypeStruct((bsz * nc, 1, L), I32),
                   jax.ShapeDtypeStruct((CLASS_ROWS, 128), F32)],
        scratch_shapes=[pltpu.VMEM((SSM_GROUPS, SSM_STATE, gw), F32),
                        pltpu.VMEM((L, d), F32),
                        pltpu.VMEM((L, gw), F32)],
        compiler_params=_params(2), name="ssd_out_proj",
    )(xbc, sz, dt, alog, dskip, ng, wout, e2, x, g1, g, sc, sh, rwt, rb)


def _dest_kernel(cls_ref, start_ref, dest_ref, run_ref):
    @pl.when(pl.program_id(0) == 0)
    def _():
        run_ref[...] = start_ref[...]

    tr = cls_ref.shape[2]
    upper = lax.broadcasted_iota(I32, (tr, tr), 0) <= lax.broadcasted_iota(I32, (tr, tr), 1)
    upper = jnp.where(upper, 1.0, 0.0).astype(BF16)
    for k in range(cls_ref.shape[0]):
        onehot = lax.broadcasted_iota(I32, (CLASS_ROWS, tr), 0) == cls_ref[k]
        oh = jnp.where(onehot, 1.0, 0.0)
        prefix = _dot(oh.astype(BF16), upper)
        dest = jnp.sum(oh * (prefix - 1.0 + run_ref[...]), axis=0, keepdims=True)
        dest_ref[k] = dest.astype(I32)
        run_ref[...] += jnp.sum(oh, axis=1, keepdims=True)


def _moe_plan(cls, cnt, tr):
    nt = cls.shape[0]
    t = nt * tr
    reps = 4 if nt % 4 == 0 else 1
    padded = jnp.ceil(cnt[:, 0] / MOE_TILE) * MOE_TILE
    end = jnp.cumsum(padded)
    start = end - padded
    dest = pl.pallas_call(
        _dest_kernel, grid=(nt // reps,),
        in_specs=[pl.BlockSpec((reps, 1, tr), lambda i: (i, 0, 0)),
                  pl.BlockSpec((CLASS_ROWS, 1), lambda i: (0, 0))],
        out_specs=pl.BlockSpec((reps, 1, tr), lambda i: (i, 0, 0)),
        out_shape=jax.ShapeDtypeStruct((nt, 1, tr), I32),
        scratch_shapes=[pltpu.VMEM((CLASS_ROWS, 1), F32)],
        compiler_params=_params(1), name="moe_dest",
    )(cls, start.reshape(CLASS_ROWS, 1))
    n_tiles = t // MOE_TILE + N_CLASSES
    tile_row = jnp.arange(n_tiles, dtype=F32) * MOE_TILE
    total = end[N_CLASSES - 1]
    valid = tile_row < total
    tcls = jnp.sum((tile_row[:, None] >= end[None, :N_CLASSES]).astype(I32), axis=1)
    last = jnp.sum((total - MOE_TILE >= end[:N_CLASSES]).astype(I32))
    tcls = jnp.where(valid, tcls, last)
    grp = tcls // N_PAIRS
    pair = tcls % N_PAIRS
    lo = jnp.where(pair < 3, 0, jnp.where(pair < 5, 1, 2))
    hi = jnp.where(pair < 3, pair + 1, jnp.where(pair < 5, pair - 1, 3))
    meta = jnp.stack([grp * EXPERTS_PER_GROUP + lo, grp * EXPERTS_PER_GROUP + hi, valid.astype(I32)]).astype(I32)
    return dest.reshape(t), meta


def _invert_kernel(dest_ref, src_ref):
    def clear(s0, carry):
        for u in range(ROW_DMA_UNROLL):
            src_ref[s0 * ROW_DMA_UNROLL + u] = 0
        return carry

    lax.fori_loop(0, src_ref.shape[0] // ROW_DMA_UNROLL, clear, 0)

    def put(t0, carry):
        for u in range(ROW_DMA_UNROLL):
            t = t0 * ROW_DMA_UNROLL + u
            src_ref[dest_ref[t]] = t
        return carry

    lax.fori_loop(0, dest_ref.shape[0] // ROW_DMA_UNROLL, put, 0)


def _invert(dest, n_slots):
    smem = pl.BlockSpec(memory_space=pltpu.SMEM)
    return pl.pallas_call(
        _invert_kernel, in_specs=[smem], out_specs=smem,
        out_shape=jax.ShapeDtypeStruct((n_slots,), I32), name="moe_invert",
    )(dest)


def _expert_kernel(meta_ref, src_ref, h2x_ref, wgu_a_ref, wgu_b_ref, wd_a_ref, wd_b_ref, y_ref,
                   buf0_ref, buf1_ref, sem):
    i = pl.program_id(0)
    d = y_ref.shape[1]
    bufs = (buf0_ref, buf1_ref)

    def gather(tile, slot):
        for r in range(MOE_TILE):
            tok = src_ref[tile * MOE_TILE + r]
            pltpu.make_async_copy(h2x_ref.at[pl.ds(tok, 1), :], bufs[slot].at[pl.ds(r, 1), :],
                                  sem.at[slot]).start(priority=r % 2)

    def wait(slot):
        pltpu.make_async_copy(h2x_ref.at[pl.ds(0, MOE_TILE), :], bufs[slot], sem.at[slot]).wait()

    @pl.when(i == 0)
    def _():
        gather(0, 0)

    valid = meta_ref[2, i] != 0
    for cur in range(2):
        @pl.when(valid & (i % 2 == cur))
        def _(cur=cur):
            wait(cur)
            gather(i + 1, 1 - cur)
            xb = bufs[cur][:, :d].astype(BF16)
            wts = bufs[cur][:, d:]
            y = jnp.zeros(y_ref.shape, F32)
            for e, (wgu_ref, wd_ref) in enumerate(((wgu_a_ref, wd_a_ref), (wgu_b_ref, wd_b_ref))):
                gu = _dot(xb, wgu_ref[0])
                ff = gu.shape[1] // 2
                act = _silu(gu[:, :ff]) * gu[:, ff:]
                y = y + wts[:, e:e + 1] * _dot(act.astype(BF16), wd_ref[0])
            y_ref[...] = y

        @pl.when(jnp.logical_not(valid) & (meta_ref[2, jnp.maximum(i - 1, 0)] != 0) & (i % 2 == cur))
        def _(cur=cur):
            wait(cur)

    @pl.when(jnp.logical_not(valid))
    def _():
        y_ref[...] = jnp.zeros(y_ref.shape, F32)


def _experts(meta, src, h2x, wgu, wd):
    n_slots = src.shape[0]
    w = h2x.shape[1]
    d = w - W_LANES
    n_tiles = n_slots // MOE_TILE
    return pl.pallas_call(
        _expert_kernel,
        grid_spec=pltpu.PrefetchScalarGridSpec(
            num_scalar_prefetch=2, grid=(n_tiles,),
            in_specs=[pl.BlockSpec(memory_space=pl.ANY),
                      pl.BlockSpec((1,) + wgu.shape[1:], lambda i, m, s: (m[0, i], 0, 0)),
                      pl.BlockSpec((1,) + wgu.shape[1:], lambda i, m, s: (m[1, i], 0, 0)),
                      pl.BlockSpec((1,) + wd.shape[1:], lambda i, m, s: (m[0, i], 0, 0)),
                      pl.BlockSpec((1,) + wd.shape[1:], lambda i, m, s: (m[1, i], 0, 0))],
            out_specs=pl.BlockSpec((MOE_TILE, d), lambda i, m, s: (i, 0)),
            scratch_shapes=[pltpu.VMEM((MOE_TILE, w), F32), pltpu.VMEM((MOE_TILE, w), F32),
                            pltpu.SemaphoreType.DMA((2,))]),
        out_shape=jax.ShapeDtypeStruct((n_slots, d), F32),
        compiler_params=_params(1), name="moe_experts",
    )(meta, src, h2x, wgu, wgu, wd, wd)


def _combine_kernel(dest_ref, ys_ref, x_ref, g2_ref, o_ref, buf0_ref, buf1_ref, sem):
    tm = x_ref.shape[1]
    step = pl.program_id(0) * pl.num_programs(1) + pl.program_id(1)
    n_steps = pl.num_programs(0) * pl.num_programs(1)
    bufs = (buf0_ref, buf1_ref)

    def gather(st, slot):
        for r in range(tm):
            pltpu.make_async_copy(ys_ref.at[pl.ds(dest_ref[st * tm + r], 1), :], bufs[slot].at[pl.ds(r, 1), :],
                                  sem.at[slot]).start(priority=r % 2)

    @pl.when(step == 0)
    def _():
        gather(0, 0)

    for cur in range(2):
        @pl.when(step % 2 == cur)
        def _(cur=cur):
            @pl.when(step + 1 < n_steps)
            def _():
                gather(step + 1, 1 - cur)

            pltpu.make_async_copy(ys_ref.at[pl.ds(0, tm), :], bufs[cur], sem.at[cur]).wait()
            o_ref[0] = x_ref[0] + g2_ref[0] * bufs[cur][...]


def _combine(dest, ys, x, g2, tm):
    bsz, s, d = x.shape
    return pl.pallas_call(
        _combine_kernel,
        grid_spec=pltpu.PrefetchScalarGridSpec(
            num_scalar_prefetch=1, grid=(bsz, s // tm),
            in_specs=[pl.BlockSpec(memory_space=pl.ANY),
                      pl.BlockSpec((1, tm, d), lambda b, i, dd: (b, i, 0)),
                      pl.BlockSpec((1, 1, d), lambda b, i, dd: (b, 0, 0))],
            out_specs=pl.BlockSpec((1, tm, d), lambda b, i, dd: (b, i, 0)),
            scratch_shapes=[pltpu.VMEM((tm, d), F32), pltpu.VMEM((tm, d), F32), pltpu.SemaphoreType.DMA((2,))]),
        out_shape=jax.ShapeDtypeStruct((bsz, s, d), F32),
        compiler_params=_params(2), name="moe_combine",
    )(dest, ys, x, g2)


def _moe(x_new, h2x, cls, cnt, g2, wgu, wd):
    bsz, s, d = x_new.shape
    t = bsz * s
    tr = min(512, t)
    dest, meta = _moe_plan(cls.reshape(t // tr, 1, tr), cnt, tr)
    n_slots = t + N_CLASSES * MOE_TILE
    ys = _experts(meta, _invert(dest, n_slots), h2x, wgu, wd)
    return _combine(dest, ys, x_new, g2, min(256, s))


def kernel(x, c, ada_w, ada_b, norm_mix, norm_ffn, hy_w_in, hy_q_norm, hy_k_norm, hy_dw_w, hy_dw_b, hy_ln_g, hy_ln_b, hy_w_out, ssm_w_in, ssm_conv_w, ssm_conv_b, ssm_dt_bias, ssm_a_log, ssm_d, ssm_norm, ssm_w_out, router_w, router_bias, exp_w_gate, exp_w_up, exp_w_down):
    bsz, s, d = x.shape
    depth = ada_w.shape[0]
    assert s % 512 == 0 and bsz <= 8
    c8 = jnp.zeros((8, d), F32).at[:bsz].set(c)
    mod = _ada(c8, ada_w, ada_b)
    rwt = router_w.T
    rb = router_bias.reshape(N_EXPERTS, 1)
    lane_head = jnp.arange(ATT_WIDTH) // HEAD_DIM
    pmat = ((lane_head[:, None] == lane_head[None, :]).astype(F32) / HEAD_DIM).astype(BF16)

    for layer in range(depth):
        sh1, sc1, g1, sh2, sc2, g2 = (mod[layer, :bsz, i * d:(i + 1) * d].reshape(bsz, 1, d) for i in range(N_MOD))
        gm = norm_mix[layer].reshape(1, d)
        gf = norm_ffn[layer].reshape(1, d)
        j = layer // 2
        if layer % 2 == 0:
            w_in = hy_w_in[j].astype(BF16)
            aw = ATT_WIDTH
            k, qt, vt, u, kmean = _hy_in(
                x, sc1, sh1, gm, w_in[:, aw:2 * aw], w_in[:, 3 * aw:], w_in[:, :aw].T, w_in[:, 2 * aw:3 * aw].T,
                (jnp.tile(hy_q_norm[j], ATT_HEADS) * (HEAD_DIM ** -0.5 * LOG2E)).reshape(aw, 1),
                jnp.tile(hy_k_norm[j], ATT_HEADS).reshape(1, aw), pmat, 512)
            bias = _gate(kmean.reshape(bsz, s // MOBA_BLOCK, aw), qt, min(1024, s))
            att_t = _attn(qt, k, vt, bias, 8)
            w_out = hy_w_out[j].astype(BF16)
            dww = jnp.zeros((CONV_HALO, hy_dw_w.shape[2]), F32).at[:CONV_WIDTH].set(hy_dw_w[j])
            x_new, h2x, cls, cnt = _hy_out(
                u, att_t, x, g1, w_out[:aw], w_out[aw:], dww, hy_dw_b[j].reshape(1, -1),
                hy_ln_g[j].reshape(1, -1), hy_ln_b[j].reshape(1, -1), gf, sc2, sh2, rwt, rb, 256)
        else:
            w_in = ssm_w_in[j]
            inner = ssm_norm.shape[1]
            heads = ssm_a_log.shape[1]
            hpg = heads // SSM_GROUPS
            conv_dim = ssm_conv_w.shape[2]

            def by_group(v):
                v = v.reshape(v.shape[:-1] + (SSM_GROUPS, hpg))
                pad = [(0, 0)] * (v.ndim - 1) + [(0, 128 - hpg)]
                return jnp.pad(v, pad).reshape(v.shape[:-2] + (SSM_GROUPS * 128,))

            sz, xbc, dt = _ssm_in(
                x, sc1, sh1, gm, w_in[:, :inner].astype(BF16), w_in[:, inner:inner + conv_dim].astype(BF16),
                by_group(w_in[:, inner + conv_dim:]).astype(BF16), ssm_conv_w[j], ssm_conv_b[j].reshape(1, -1),
                by_group(ssm_dt_bias[j]).reshape(1, -1), 256)
            chan_head = jnp.arange(inner // SSM_GROUPS) // SSM_HEAD_DIM
            e2 = (jnp.arange(256)[:, None] % 128 == chan_head[None, :]).astype(BF16)
            x_new, h2x, cls, cnt = _ssd(
                xbc, sz, dt, by_group(ssm_a_log[j]).reshape(SSM_GROUPS, 1, 128),
                jnp.repeat(ssm_d[j], SSM_HEAD_DIM).reshape(SSM_GROUPS, 1, -1),
                ssm_norm[j].reshape(SSM_GROUPS, 1, -1), ssm_w_out[j].astype(BF16), e2, x, g1, gf, sc2, sh2, rwt, rb)
        wgu = jnp.concatenate([exp_w_gate[layer], exp_w_up[layer]], axis=-1).astype(BF16)
        x = _moe(x_new, h2x, cls, cnt, g2, wgu, exp_w_down[layer].astype(BF16))
    return x
```

```python
import functools

import jax
import jax.numpy as jnp
from jax import lax
from jax.experimental import pallas as pl
from jax.experimental.pallas import tpu as pltpu

F32 = jnp.float32
BF16 = jnp.bfloat16
I32 = jnp.int32

NORM_EPS = 1e-6
N_MOD = 6
HEAD_DIM = 64
ATT_HEADS = 8
ATT_WIDTH = ATT_HEADS * HEAD_DIM
MOBA_BLOCK = 256
MOBA_TOPK = 3
CONV_WIDTH = 31
CONV_HALO = 32
SSM_HEAD_DIM = 64
SSM_GROUPS = 4
SSM_STATE = 128
SSM_CONV = 4
SSM_CHUNK = 256
SSM_HALO = 8
SSD_SUB = 128
SSM_IN_COLS = 512
N_EXPERTS = 16
N_EXPERT_GROUPS = 4
EXPERTS_PER_GROUP = 4
N_PAIRS = 6
N_CLASSES = N_EXPERT_GROUPS * N_PAIRS
CLASS_ROWS = 32
MOE_TILE = 256
EXPERT_COLS = 256
W_LANES = 128
ROW_DMA_UNROLL = 8
LOG2E = 1.4426950408889634
NEG = -1e30
VMEM_LIMIT = 56 * 1024 * 1024


def _params(n_axes):
    return pltpu.CompilerParams(dimension_semantics=("arbitrary",) * n_axes,
                                vmem_limit_bytes=VMEM_LIMIT)


def _dot(a, b):
    return jnp.dot(a, b, preferred_element_type=F32)


def _dot_nt(a, b):
    return lax.dot_general(a, b, (((1,), (1,)), ((), ())), preferred_element_type=F32)


def _split2(x):
    hi = x.astype(BF16)
    lo = (x - hi.astype(F32)).astype(BF16)
    return hi, lo


def _split3(x):
    a = x.astype(BF16)
    r = x - a.astype(F32)
    b = r.astype(BF16)
    c = (r - b.astype(F32)).astype(BF16)
    return a, b, c


def _silu(x):
    return x * jax.nn.sigmoid(x)


def _mod_norm(x, g, sc, sh):
    ms = jnp.mean(x * x, axis=-1, keepdims=True)
    return x * lax.rsqrt(ms + NORM_EPS) * g * (1.0 + sc) + sh


def _ada_kernel(c_ref, w_ref, b_ref, o_ref):
    cond = _silu(c_ref[...])
    ch, cl = _split2(cond)
    wh, wl = _split2(w_ref[0])
    o_ref[0] = _dot(ch, wh) + _dot(ch, wl) + _dot(cl, wh) + b_ref[0]


def _ada(c8, ada_w, ada_b):
    depth, d, n = ada_w.shape
    tn = 1536
    return pl.pallas_call(
        _ada_kernel,
        grid=(depth, n // tn),
        in_specs=[pl.BlockSpec((8, d), lambda l, j: (0, 0)),
                  pl.BlockSpec((1, d, tn), lambda l, j: (l, 0, j)),
                  pl.BlockSpec((1, 1, tn), lambda l, j: (l, 0, j))],
        out_specs=pl.BlockSpec((1, 8, tn), lambda l, j: (l, 0, j)),
        out_shape=jax.ShapeDtypeStruct((depth, 8, n), F32),
        compiler_params=_params(2), name="ada_mod",
    )(c8, ada_w, ada_b.reshape(depth, 1, n))


def _hy_in_kernel(x_ref, sc_ref, sh_ref, g_ref, wk_ref, wu_ref, wqt_ref, wvt_ref, qg_ref, kg_ref, p_ref,
                  k_ref, qt_ref, vt_ref, u_ref, km_ref):
    tm = x_ref.shape[1]
    h = _mod_norm(x_ref[0], g_ref[...], sc_ref[0], sh_ref[0])
    hb = h.astype(BF16)
    k = _dot(hb, wk_ref[...])
    khi, klo = _split2(k * k)
    ms = _dot(khi, p_ref[...]) + _dot(klo, p_ref[...])
    kn = k * lax.rsqrt(ms + NORM_EPS) * kg_ref[...]
    k_ref[0] = kn.astype(BF16)
    km_ref[0, 0] = jnp.mean(kn.reshape(tm // MOBA_BLOCK, MOBA_BLOCK, ATT_WIDTH), axis=1)
    qt = _dot_nt(wqt_ref[...], hb).reshape(ATT_HEADS, HEAD_DIM, tm)
    qms = jnp.mean(qt * qt, axis=1, keepdims=True)
    qn = (qt * lax.rsqrt(qms + NORM_EPS)).reshape(ATT_WIDTH, tm) * qg_ref[...]
    qt_ref[0] = qn.astype(BF16)
    vt_ref[0] = _dot_nt(wvt_ref[...], hb).astype(BF16)
    ag = _dot(hb, wu_ref[...])
    half = ag.shape[1] // 2
    u_ref[0] = (ag[:, :half] * jax.nn.sigmoid(ag[:, half:])).astype(BF16)


def _hy_in(x, sc, sh, g, wk, wu, wqt, wvt, qg, kg, pmat, tm):
    bsz, s, d = x.shape
    cc = wu.shape[1] // 2
    full = lambda shape: pl.BlockSpec(shape, lambda b, i: (0,) * len(shape))
    return pl.pallas_call(
        _hy_in_kernel,
        grid=(bsz, s // tm),
        in_specs=[pl.BlockSpec((1, tm, d), lambda b, i: (b, i, 0)),
                  pl.BlockSpec((1, 1, d), lambda b, i: (b, 0, 0)),
                  pl.BlockSpec((1, 1, d), lambda b, i: (b, 0, 0)),
                  full((1, d)), full(wk.shape), full(wu.shape), full(wqt.shape), full(wvt.shape),
                  full(qg.shape), full(kg.shape), full(pmat.shape)],
        out_specs=[pl.BlockSpec((1, tm, ATT_WIDTH), lambda b, i: (b, i, 0)),
                   pl.BlockSpec((1, ATT_WIDTH, tm), lambda b, i: (b, 0, i)),
                   pl.BlockSpec((1, ATT_WIDTH, tm), lambda b, i: (b, 0, i)),
                   pl.BlockSpec((1, tm, cc), lambda b, i: (b, i, 0)),
                   pl.BlockSpec((1, 1, tm // MOBA_BLOCK, ATT_WIDTH), lambda b, i: (b, i, 0, 0))],
        out_shape=[jax.ShapeDtypeStruct((bsz, s, ATT_WIDTH), BF16),
                   jax.ShapeDtypeStruct((bsz, ATT_WIDTH, s), BF16),
                   jax.ShapeDtypeStruct((bsz, ATT_WIDTH, s), BF16),
                   jax.ShapeDtypeStruct((bsz, s, cc), BF16),
                   jax.ShapeDtypeStruct((bsz, s // tm, tm // MOBA_BLOCK, ATT_WIDTH), F32)],
        compiler_params=_params(2), name="hy_in_proj",
    )(x, sc, sh, g, wk, wu, wqt, wvt, qg, kg, pmat)


def _gate_kernel(km_ref, qt_ref, o_ref):
    nb = km_ref.shape[1]
    ts = qt_ref.shape[2]
    rows = ATT_HEADS * nb
    km = km_ref[0]
    kmt = jnp.broadcast_to(km[None], (ATT_HEADS, nb, ATT_WIDTH)).reshape(rows, ATT_WIDTH)
    rh = lax.broadcasted_iota(I32, (rows, ATT_WIDTH), 0) // nb
    ch = lax.broadcasted_iota(I32, (rows, ATT_WIDTH), 1) // HEAD_DIM
    kbd = jnp.where(rh == ch, kmt, 0.0)
    khi, klo = _split2(kbd)
    qt = qt_ref[0]
    gate = (_dot(khi, qt) + _dot(klo, qt)).reshape(ATT_HEADS, nb, ts)
    blk = lax.broadcasted_iota(I32, (ATT_HEADS, nb, ts), 1)
    qblk = (pl.program_id(1) * ts + lax.broadcasted_iota(I32, (ATT_HEADS, nb, ts), 2)) // MOBA_BLOCK
    g = jnp.where(blk < qblk, gate, -jnp.inf)
    sel = jnp.zeros(g.shape, jnp.bool_)
    for _ in range(MOBA_TOPK):
        m = jnp.max(g, axis=1, keepdims=True)
        first = jnp.min(jnp.where((g == m) & (m > -jnp.inf), blk, nb), axis=1, keepdims=True)
        pick = blk == first
        sel = sel | pick
        g = jnp.where(pick, -jnp.inf, g)
    o_ref[0] = jnp.where(sel, 0.0, NEG).reshape(rows, ts)


def _gate(kmean, qt, ts):
    bsz, nb, _ = kmean.shape
    s = qt.shape[2]
    rows = ATT_HEADS * nb
    return pl.pallas_call(
        _gate_kernel,
        grid=(bsz, s // ts),
        in_specs=[pl.BlockSpec((1, nb, ATT_WIDTH), lambda b, i: (b, 0, 0)),
                  pl.BlockSpec((1, ATT_WIDTH, ts), lambda b, i: (b, 0, i))],
        out_specs=pl.BlockSpec((1, rows, ts), lambda b, i: (b, 0, i)),
        out_shape=jax.ShapeDtypeStruct((bsz, rows, s), F32),
        compiler_params=_params(2), name="moba_gate",
    )(kmean, qt)


def _attn_kernel(qt_ref, k_ref, vt_ref, bias_ref, o_ref, qa_ref, acc_ref, m_ref, *, nb, hp, lag, kb):
    blk = MOBA_BLOCK
    pw = 2 * HEAD_DIM
    va = HEAD_DIM + 16
    i = pl.program_id(2)
    row = lax.broadcasted_iota(I32, (pw, blk), 0)
    for h in range(hp):
        qp = qt_ref[0, (h // 2) * pw:(h // 2 + 1) * pw, :]
        keep = (row < HEAD_DIM) if h % 2 == 0 else (row >= HEAD_DIM)
        qa_ref[h, :pw, :] = jnp.where(keep, qp, jnp.zeros_like(qp))
        qa_ref[h, pw:pw + nb, :] = bias_ref[0, h * nb:(h + 1) * nb, :].astype(BF16)
        qa_ref[h, pw + nb:, :] = jnp.zeros((blk - pw - nb, blk), BF16)
        m_ref[h] = jnp.full((1, blk), NEG, F32)
        acc_ref[h] = jnp.zeros((va, blk), F32)
    causal = lax.broadcasted_iota(I32, (blk, blk), 0) <= lax.broadcasted_iota(I32, (blk, blk), 1)
    lane = lax.broadcasted_iota(I32, (blk, pw), 1)
    ones = jnp.ones((va - HEAD_DIM, blk), BF16)

    def step(blocks):
        units = [(b, h) for b in range(len(blocks)) for h in range(hp)]
        offs = [pl.multiple_of(j * blk, blk) for j, _, _ in blocks]

        def scores(u):
            b, h = units[u]
            pp = h // 2
            ka = jnp.concatenate([k_ref[0, pl.ds(offs[b], blk), pp * pw:(pp + 1) * pw], blocks[b][1]], axis=1)
            return blocks[b][2](_dot(ka, qa_ref[h]))

        def softmax(u, s):
            h = units[u][1]
            m_old = m_ref[h]
            m_new = jnp.maximum(m_old, jnp.max(s, axis=0, keepdims=True))
            m_ref[h] = m_new
            return jnp.exp2(s - m_new).astype(BF16), jnp.exp2(m_old - m_new)

        def accumulate(u, p, alpha):
            b, h = units[u]
            vj = jnp.concatenate([vt_ref[0, h * HEAD_DIM:(h + 1) * HEAD_DIM, pl.ds(offs[b], blk)], ones], axis=0)
            acc_ref[h] = alpha * acc_ref[h] + _dot(vj, p)

        s, pa = {}, {}
        for t in range(len(units) + lag):
            if t < len(units):
                s[t] = scores(t)
            if 1 <= t <= len(units):
                pa[t - 1] = softmax(t - 1, s.pop(t - 1))
            if t >= lag:
                accumulate(t - lag, *pa.pop(t - lag))

    def past(j):
        return (j, jnp.where(lane == j, 1.0, 0.0).astype(BF16), lambda s: s)

    step([(i, jnp.zeros((blk, pw), BF16), lambda s: jnp.where(causal, s, NEG))])
    rem = i % kb
    base = jnp.int32(0)
    size = 1
    while size < kb:
        take = (rem & size) != 0

        @pl.when(take)
        def _(base=base, size=size):
            step([past(base + t) for t in range(size)])

        base = base + jnp.where(take, size, 0)
        size *= 2

    def body(r, carry):
        step([past(rem + kb * r + t) for t in range(kb)])
        return carry

    lax.fori_loop(0, i // kb, body, 0)
    for h in range(hp):
        acc = acc_ref[h]
        o_ref[0, h * HEAD_DIM:(h + 1) * HEAD_DIM, :] = (acc[:HEAD_DIM] / acc[HEAD_DIM:HEAD_DIM + 1]).astype(BF16)


def _attn(qt, k, vt, bias, hp):
    bsz, s, _ = k.shape
    nb = s // MOBA_BLOCK
    assert nb <= MOBA_BLOCK - 2 * HEAD_DIM and nb % 16 == 0
    hw = hp * HEAD_DIM
    return pl.pallas_call(
        functools.partial(_attn_kernel, nb=nb, hp=hp, lag=6, kb=4),
        grid=(bsz, ATT_HEADS // hp, nb),
        in_specs=[pl.BlockSpec((1, hw, MOBA_BLOCK), lambda b, p, i: (b, p, i)),
                  pl.BlockSpec((1, s, hw), lambda b, p, i: (b, 0, p)),
                  pl.BlockSpec((1, hw, s), lambda b, p, i: (b, p, 0)),
                  pl.BlockSpec((1, hp * nb, MOBA_BLOCK), lambda b, p, i: (b, p, i))],
        out_specs=pl.BlockSpec((1, hw, MOBA_BLOCK), lambda b, p, i: (b, p, i)),
        out_shape=jax.ShapeDtypeStruct((bsz, ATT_WIDTH, s), BF16),
        scratch_shapes=[pltpu.VMEM((hp, MOBA_BLOCK, MOBA_BLOCK), BF16),
                        pltpu.VMEM((hp, HEAD_DIM + 16, MOBA_BLOCK), F32),
                        pltpu.VMEM((hp, 1, MOBA_BLOCK), F32)],
        compiler_params=_params(3), name="moba_attn",
    )(qt, k, vt, bias)


def _route(logit_t, rb):
    aff = jax.nn.sigmoid(logit_t)
    score = aff + rb
    s = [score[e:e + 1, :] for e in range(N_EXPERTS)]
    a = [aff[e:e + 1, :] for e in range(N_EXPERTS)]
    n = EXPERTS_PER_GROUP

    def top2_sum(v):
        best = None
        for x in range(n):
            for y in range(x + 1, n):
                t = v[x] + v[y]
                best = t if best is None else jnp.maximum(best, t)
        return best

    def first_argmax(v):
        idx = jnp.zeros(v[0].shape, I32)
        cur = v[0]
        for x in range(1, len(v)):
            better = v[x] > cur
            idx = jnp.where(better, x, idx)
            cur = jnp.where(better, v[x], cur)
        return idx

    grp = first_argmax([top2_sum(s[n * g:n * g + n]) for g in range(N_EXPERT_GROUPS)])

    def in_group(v, x):
        out = v[(N_EXPERT_GROUPS - 1) * n + x]
        for g in range(N_EXPERT_GROUPS - 2, -1, -1):
            out = jnp.where(grp == g, v[n * g + x], out)
        return out

    sg = [in_group(s, x) for x in range(n)]
    ag = [in_group(a, x) for x in range(n)]
    l1 = first_argmax(sg)
    l2 = first_argmax([jnp.where(l1 == x, -jnp.inf, sg[x]) for x in range(n)])
    lo = jnp.minimum(l1, l2)
    hi = jnp.maximum(l1, l2)
    pair = jnp.where(lo == 0, hi - 1, jnp.where(lo == 1, hi + 1, 5))
    a_lo = jnp.zeros_like(ag[0])
    a_hi = jnp.zeros_like(ag[0])
    for x in range(n):
        a_lo = jnp.where(lo == x, ag[x], a_lo)
        a_hi = jnp.where(hi == x, ag[x], a_hi)
    tot = a_lo + a_hi
    return grp * N_PAIRS + pair, a_lo / tot, a_hi / tot


def _post_mixer(x_new, g_ref, sc_ref, sh_ref, rwt_ref, rb_ref, h2x_ref, cls_ref, cnt_ref):
    tm, d = x_new.shape
    h2 = _mod_norm(x_new, g_ref[...], sc_ref[0], sh_ref[0])
    hh, hl = _split2(h2)
    rh, rl = _split2(rwt_ref[...])
    logit_t = _dot_nt(rh, hh) + _dot_nt(rh, hl) + _dot_nt(rl, hh)
    cls, w_lo, w_hi = _route(logit_t, rb_ref[...])
    cls_ref[0] = cls

    @pl.when((pl.program_id(0) == 0) & (pl.program_id(1) == 0))
    def _():
        cnt_ref[...] = jnp.zeros(cnt_ref.shape, F32)

    onehot = lax.broadcasted_iota(I32, (CLASS_ROWS, tm), 0) == cls
    cnt_ref[...] += jnp.sum(jnp.where(onehot, 1.0, 0.0), axis=1, keepdims=True)
    wrow = lax.broadcasted_iota(I32, (W_LANES, tm), 0)
    wt = jnp.where(wrow == 0, w_lo, jnp.where(wrow == 1, w_hi, 0.0))
    h2x_ref[:, :d] = h2
    h2x_ref[:, d:] = wt.T


def _hy_out_kernel(u_ref, up_ref, at_ref, x_ref, g1_ref, wtop_ref, wbot_ref, dww_ref, dwb_ref, lng_ref, lnb_ref,
                   g_ref, sc_ref, sh_ref, rwt_ref, rb_ref,
                   xn_ref, h2x_ref, cls_ref, cnt_ref, cat_ref, shifted_ref):
    tm = u_ref.shape[1]
    m_att = _dot(at_ref[0].astype(F32).T.astype(BF16), wtop_ref[...])
    prev = up_ref[0].astype(F32)
    cat_ref[:CONV_HALO, :] = jnp.where(pl.program_id(1) == 0, 0.0, prev)
    cat_ref[CONV_HALO:, :] = u_ref[0].astype(F32)
    span = tm + CONV_HALO - 8
    for r in range(1, 8):
        shifted_ref[r - 1] = cat_ref[pl.ds(r, span), :]
    y = jnp.zeros((tm, u_ref.shape[2]), F32) + dwb_ref[...]
    for j in range(CONV_WIDTH):
        a, r = divmod(CONV_HALO - CONV_WIDTH + 1 + j, 8)
        tap = cat_ref[pl.ds(8 * a, tm), :] if r == 0 else shifted_ref[r - 1, pl.ds(8 * a, tm), :]
        y = y + dww_ref[j:j + 1, :] * tap
    mu = jnp.mean(y, axis=-1, keepdims=True)
    var = jnp.mean(jnp.square(y - mu), axis=-1, keepdims=True)
    cv = _silu((y - mu) * lax.rsqrt(var + NORM_EPS) * lng_ref[...] + lnb_ref[...])
    m = m_att + _dot(cv.astype(BF16), wbot_ref[...])
    x_new = x_ref[0] + g1_ref[0] * m
    xn_ref[0] = x_new
    _post_mixer(x_new, g_ref, sc_ref, sh_ref, rwt_ref, rb_ref, h2x_ref, cls_ref, cnt_ref)


def _hy_out(u, att_t, x, g1, wtop, wbot, dww, dwb, lng, lnb, g, sc, sh, rwt, rb, tm):
    bsz, s, d = x.shape
    cc = u.shape[2]
    nt = s // tm
    full = lambda shape: pl.BlockSpec(shape, lambda b, i: (0,) * len(shape))
    per_b = pl.BlockSpec((1, 1, d), lambda b, i: (b, 0, 0))
    halo = tm // CONV_HALO
    return pl.pallas_call(
        _hy_out_kernel,
        grid=(bsz, nt),
        in_specs=[pl.BlockSpec((1, tm, cc), lambda b, i: (b, i, 0)),
                  pl.BlockSpec((1, CONV_HALO, cc), lambda b, i: (b, jnp.maximum(i * halo - 1, 0), 0)),
                  pl.BlockSpec((1, ATT_WIDTH, tm), lambda b, i: (b, 0, i)),
                  pl.BlockSpec((1, tm, d), lambda b, i: (b, i, 0)),
                  per_b, full(wtop.shape), full(wbot.shape), full(dww.shape), full(dwb.shape),
                  full(lng.shape), full(lnb.shape), full(g.shape), per_b, per_b, full(rwt.shape), full(rb.shape)],
        out_specs=[pl.BlockSpec((1, tm, d), lambda b, i: (b, i, 0)),
                   pl.BlockSpec((tm, d + W_LANES), lambda b, i: (b * nt + i, 0)),
                   pl.BlockSpec((1, 1, tm), lambda b, i: (b * nt + i, 0, 0)),
                   pl.BlockSpec((CLASS_ROWS, 128), lambda b, i: (0, 0))],
        out_shape=[jax.ShapeDtypeStruct((bsz, s, d), F32),
                   jax.ShapeDtypeStruct((bsz * s, d + W_LANES), F32),
                   jax.ShapeDtypeStruct((bsz * nt, 1, tm), I32),
                   jax.ShapeDtypeStruct((CLASS_ROWS, 128), F32)],
        scratch_shapes=[pltpu.VMEM((tm + CONV_HALO, cc), F32),
                        pltpu.VMEM((7, tm + CONV_HALO - 8, cc), F32)],
        compiler_params=_params(2), name="hy_out_proj",
    )(u, u, att_t, x, g1, wtop, wbot, dww, dwb, lng, lnb, g, sc, sh, rwt, rb)


def _ssm_in_kernel(x_ref, sc_ref, sh_ref, g_ref, wz_ref, wx_ref, wdt_ref, cw_ref, cb_ref, dtb_ref,
                   sz_ref, xbc_ref, dt_ref, *cat_refs):
    tm = x_ref.shape[1]
    ch = SSM_IN_COLS
    h = _mod_norm(x_ref[0], g_ref[...], sc_ref[0], sh_ref[0])
    hb = h.astype(BF16)

    @pl.when(pl.program_id(1) == 0)
    def _():
        for cat_ref in cat_refs:
            cat_ref[:SSM_HALO, :] = jnp.zeros((SSM_HALO, ch), F32)

    def conv_matmul(c):
        cat_refs[c][SSM_HALO:, :] = _dot(hb, wx_ref[:, c * ch:(c + 1) * ch])

    def conv_act(c, _):
        cols = slice(c * ch, (c + 1) * ch)
        cat_ref = cat_refs[c]
        y = jnp.zeros((tm, ch), F32) + cb_ref[:, cols]
        for j in range(SSM_CONV):
            y = y + cw_ref[j:j + 1, cols] * cat_ref[pl.ds(SSM_HALO - SSM_CONV + 1 + j, tm), :]
        xbc_ref[0, :, cols] = _silu(y).astype(BF16)
        cat_ref[:SSM_HALO, :] = cat_ref[pl.ds(tm, SSM_HALO), :]

    def gate_matmul(c):
        return _dot(hb, wz_ref[:, c * ch:(c + 1) * ch])

    def gate_act(c, z):
        sz_ref[0, :, c * ch:(c + 1) * ch] = _silu(z).astype(BF16)

    stages = [(conv_matmul, conv_act, c) for c in range(wx_ref.shape[1] // ch)]
    stages += [(gate_matmul, gate_act, c) for c in range(wz_ref.shape[1] // ch)]
    pending = None
    for matmul, act, c in stages:
        out = matmul(c)
        if pending is not None:
            pending[0](pending[1], pending[2])
        pending = (act, c, out)
    t = _dot(hb, wdt_ref[...]) + dtb_ref[...]
    pending[0](pending[1], pending[2])
    dt_ref[0] = jnp.maximum(t, 0.0) + jnp.log(1.0 + jnp.exp(-jnp.abs(t)))


def _ssm_in(x, sc, sh, g, wz, wx, wdt, cw, cb, dtb, tm):
    bsz, s, d = x.shape
    full = lambda shape: pl.BlockSpec(shape, lambda b, i: (0,) * len(shape))
    per_b = pl.BlockSpec((1, 1, d), lambda b, i: (b, 0, 0))
    tile = lambda n: pl.BlockSpec((1, tm, n), lambda b, i: (b, i, 0))
    return pl.pallas_call(
        _ssm_in_kernel,
        grid=(bsz, s // tm),
        in_specs=[tile(d), per_b, per_b, full(g.shape), full(wz.shape), full(wx.shape), full(wdt.shape),
                  full(cw.shape), full(cb.shape), full(dtb.shape)],
        out_specs=[tile(wz.shape[1]), tile(wx.shape[1]), tile(wdt.shape[1])],
        out_shape=[jax.ShapeDtypeStruct((bsz, s, wz.shape[1]), BF16),
                   jax.ShapeDtypeStruct((bsz, s, wx.shape[1]), BF16),
                   jax.ShapeDtypeStruct((bsz, s, wdt.shape[1]), F32)],
        scratch_shapes=[pltpu.VMEM((tm + SSM_HALO, SSM_IN_COLS), F32)] * (wx.shape[1] // SSM_IN_COLS),
        compiler_params=_params(2), name="ssm_in_proj",
    )(x, sc, sh, g, wz, wx, wdt, cw, cb, dtb)


def _ssd_kernel(xbc_ref, sz_ref, dt_ref, alog_ref, dsk_ref, ng_ref, wout_ref, e2_ref, x_ref, g1_ref,
                g_ref, sc_ref, sh_ref, rwt_ref, rb_ref,
                xn_ref, h2x_ref, cls_ref, cnt_ref, state_ref, macc_ref, y_ref):
    L = SSD_SUB
    inner = sz_ref.shape[2]
    gw = inner // SSM_GROUPS
    pw = 2 * SSM_HEAD_DIM

    @pl.when(pl.program_id(1) == 0)
    def _():
        state_ref[...] = jnp.zeros(state_ref.shape, F32)

    subs = [slice(s * L, (s + 1) * L) for s in range(SSM_CHUNK // L)]
    brow = lax.broadcasted_iota(I32, (SSM_CHUNK, SSM_CHUNK), 0)
    bcol = lax.broadcasted_iota(I32, (SSM_CHUNK, SSM_CHUNK), 1)
    tri = jnp.where((brow // L == bcol // L) & (brow >= bcol), 1.0, 0.0).astype(BF16)
    lower = lax.broadcasted_iota(I32, (L, L), 0) >= lax.broadcasted_iota(I32, (L, L), 1)
    lo_half = lax.broadcasted_iota(I32, (L, pw), 1) < SSM_HEAD_DIM

    def expand(v):
        return _dot(jnp.concatenate(_split2(v), axis=1), e2_ref[...])

    macc_ref[...] = jnp.zeros(macc_ref.shape, F32)

    def group(g, carry):
        xcol = pl.ds(pl.multiple_of(g * gw, gw), gw)
        bcolumns = pl.ds(pl.multiple_of(inner + g * SSM_STATE, SSM_STATE), SSM_STATE)
        ccolumns = pl.ds(pl.multiple_of(inner + (SSM_GROUPS + g) * SSM_STATE, SSM_STATE), SSM_STATE)
        rate = -jnp.exp(alog_ref[g]) * LOG2E
        dt = dt_ref[0, :, pl.ds(pl.multiple_of(g * 128, 128), 128)]
        t3 = _dot(tri, jnp.concatenate(_split3(dt * rate), axis=1))
        a_cs = t3[:, :128] + t3[:, 128:256] + t3[:, 256:]
        xg = xbc_ref[0, :, xcol].astype(F32)
        xdt = xg * expand(dt)
        xdt_b = xdt.astype(BF16)
        bs = [xbc_ref[0, r, bcolumns] for r in subs]
        cs = [xbc_ref[0, r, ccolumns] for r in subs]
        cbm = [jnp.where(lower, _dot_nt(cm, bm), 0.0) for cm, bm in zip(cs, bs)]
        a_end = jnp.concatenate([jnp.broadcast_to(a_cs[r.stop - 1:r.stop, :], (L, 128)) for r in subs], axis=0)
        xw = (xdt * expand(jnp.exp2(a_end - a_cs))).astype(BF16)
        grow = expand(jnp.exp2(a_cs))
        keep = expand(jnp.exp2(jnp.concatenate([a_end[r.start:r.start + 16, :] for r in subs], axis=0)))
        st = state_ref[g]
        y_off = []
        for s, r in enumerate(subs):
            y_off.append(_dot(cs[s], st.astype(BF16)) * grow[r, :])
            st = st * keep[16 * s:16 * s + 1, :] + _dot(bs[s].astype(F32).T.astype(BF16), xw[r, :])
        state_ref[g] = st
        for s, r in enumerate(subs):
            a_sub = a_cs[r, :]
            a_sub_t = a_sub.T
            for q in range(gw // pw):
                x2 = xdt_b[r, q * pw:(q + 1) * pw]
                yp = y_off[s][:, q * pw:(q + 1) * pw]
                for e in range(2):
                    hd = 2 * q + e
                    dec = jnp.exp2(jnp.minimum(a_sub[:, hd:hd + 1] - a_sub_t[hd:hd + 1, :], 0.0))
                    xm = jnp.where(lo_half if e == 0 else jnp.logical_not(lo_half), x2, jnp.zeros_like(x2))
                    yp = yp + _dot((cbm[s] * dec).astype(BF16), xm)
                y_ref[r, q * pw:(q + 1) * pw] = yp
        gt = (y_ref[...] + xg * dsk_ref[g]) * sz_ref[0, :, xcol].astype(F32)
        ms = jnp.mean(gt * gt, axis=-1, keepdims=True)
        gn = gt * lax.rsqrt(ms + NORM_EPS) * ng_ref[g]
        macc_ref[...] += _dot(gn.astype(BF16), wout_ref[xcol, :])
        return carry

    lax.fori_loop(0, SSM_GROUPS, group, 0)
    x_new = x_ref[0] + g1_ref[0] * macc_ref[...]
    xn_ref[0] = x_new
    _post_mixer(x_new, g_ref, sc_ref, sh_ref, rwt_ref, rb_ref, h2x_ref, cls_ref, cnt_ref)


def _ssd(xbc, sz, dt, alog, dskip, ng, wout, e2, x, g1, g, sc, sh, rwt, rb):
    bsz, s, d = x.shape
    L = SSM_CHUNK
    nc = s // L
    gw = sz.shape[2] // SSM_GROUPS
    full = lambda shape: pl.BlockSpec(shape, lambda b, c: (0,) * len(shape))
    per_b = pl.BlockSpec((1, 1, d), lambda b, c: (b, 0, 0))
    tile = lambda n: pl.BlockSpec((1, L, n), lambda b, c: (b, c, 0))
    return pl.pallas_call(
        _ssd_kernel,
        grid=(bsz, nc),
        in_specs=[tile(xbc.shape[2]), tile(sz.shape[2]), tile(dt.shape[2]),
                  full(alog.shape), full(dskip.shape), full(ng.shape), full(wout.shape), full(e2.shape),
                  tile(d), per_b, full(g.shape), per_b, per_b, full(rwt.shape), full(rb.shape)],
        out_specs=[tile(d),
                   pl.BlockSpec((L, d + W_LANES), lambda b, c: (b * nc + c, 0)),
                   pl.BlockSpec((1, 1, L), lambda b, c: (b * nc + c, 0, 0)),
                   pl.BlockSpec((CLASS_ROWS, 128), lambda b, c: (0, 0))],
        out_shape=[jax.ShapeDtypeStruct((bsz, s, d), F32),
                   jax.ShapeDtypeStruct((bsz * s, d + W_LANES), F32),
                   jax.ShapeDtypeStruct((bsz * nc, 1, L), I32),
                   jax.ShapeDtypeStruct((CLASS_ROWS, 128), F32)],
        scratch_shapes=[pltpu.VMEM((SSM_GROUPS, SSM_STATE, gw), F32),
                        pltpu.VMEM((L, d), F32),
                        pltpu.VMEM((L, gw), F32)],
        compiler_params=_params(2), name="ssd_out_proj",
    )(xbc, sz, dt, alog, dskip, ng, wout, e2, x, g1, g, sc, sh, rwt, rb)


def _dest_kernel(cls_ref, start_ref, dest_ref, run_ref):
    @pl.when(pl.program_id(0) == 0)
    def _():
        run_ref[...] = start_ref[...]

    tr = cls_ref.shape[2]
    upper = lax.broadcasted_iota(I32, (tr, tr), 0) <= lax.broadcasted_iota(I32, (tr, tr), 1)
    upper = jnp.where(upper, 1.0, 0.0).astype(BF16)
    for k in range(cls_ref.shape[0]):
        onehot = lax.broadcasted_iota(I32, (CLASS_ROWS, tr), 0) == cls_ref[k]
        oh = jnp.where(onehot, 1.0, 0.0)
        prefix = _dot(oh.astype(BF16), upper)
        dest = jnp.sum(oh * (prefix - 1.0 + run_ref[...]), axis=0, keepdims=True)
        dest_ref[k] = dest.astype(I32)
        run_ref[...] += jnp.sum(oh, axis=1, keepdims=True)


def _moe_plan(cls, cnt, tr):
    nt = cls.shape[0]
    t = nt * tr
    reps = 4 if nt % 4 == 0 else 1
    padded = jnp.ceil(cnt[:, 0] / MOE_TILE) * MOE_TILE
    end = jnp.cumsum(padded)
    start = end - padded
    dest = pl.pallas_call(
        _dest_kernel, grid=(nt // reps,),
        in_specs=[pl.BlockSpec((reps, 1, tr), lambda i: (i, 0, 0)),
                  pl.BlockSpec((CLASS_ROWS, 1), lambda i: (0, 0))],
        out_specs=pl.BlockSpec((reps, 1, tr), lambda i: (i, 0, 0)),
        out_shape=jax.ShapeDtypeStruct((nt, 1, tr), I32),
        scratch_shapes=[pltpu.VMEM((CLASS_ROWS, 1), F32)],
        compiler_params=_params(1), name="moe_dest",
    )(cls, start.reshape(CLASS_ROWS, 1))
    n_tiles = t // MOE_TILE + N_CLASSES
    tile_row = jnp.arange(n_tiles, dtype=F32) * MOE_TILE
    total = end[N_CLASSES - 1]
    valid = tile_row < total
    tcls = jnp.sum((tile_row[:, None] >= end[None, :N_CLASSES]).astype(I32), axis=1)
    last = jnp.sum((total - MOE_TILE >= end[:N_CLASSES]).astype(I32))
    tcls = jnp.where(valid, tcls, last)
    grp = tcls // N_PAIRS
    pair = tcls % N_PAIRS
    lo = jnp.where(pair < 3, 0, jnp.where(pair < 5, 1, 2))
    hi = jnp.where(pair < 3, pair + 1, jnp.where(pair < 5, pair - 1, 3))
    meta = jnp.stack([grp * EXPERTS_PER_GROUP + lo, grp * EXPERTS_PER_GROUP + hi, valid.astype(I32),
                      jnp.zeros_like(tcls)]).astype(I32)
    return dest.reshape(t), meta


def _invert_kernel(dest_ref, zeros_ref, src_ref, sem):
    clear = pltpu.make_async_copy(zeros_ref, src_ref, sem)
    clear.start()
    clear.wait()

    def put(t0, carry):
        for u in range(ROW_DMA_UNROLL):
            t = t0 * ROW_DMA_UNROLL + u
            src_ref[dest_ref[t]] = t
        return carry

    lax.fori_loop(0, dest_ref.shape[0] // ROW_DMA_UNROLL, put, 0)


def _invert(dest, n_slots):
    smem = pl.BlockSpec(memory_space=pltpu.SMEM)
    return pl.pallas_call(
        _invert_kernel, in_specs=[smem, pl.BlockSpec(memory_space=pl.ANY)], out_specs=smem,
        out_shape=jax.ShapeDtypeStruct((n_slots,), I32),
        scratch_shapes=[pltpu.SemaphoreType.DMA(())], name="moe_invert",
    )(dest, jnp.zeros((n_slots,), I32))


def _expert_kernel(meta_ref, src_ref, h2x_ref, wgu_a_ref, wgu_b_ref, wd_a_ref, wd_b_ref, y_ref,
                   buf0_ref, buf1_ref, sem):
    i = pl.program_id(0)
    d = y_ref.shape[1]
    bufs = (buf0_ref, buf1_ref)

    def gather(tile, slot, rows=range(MOE_TILE), after=0):
        for r in rows:
            tok = src_ref[tile * MOE_TILE + r] + after
            pltpu.make_async_copy(h2x_ref.at[pl.ds(tok, 1), :], bufs[slot].at[pl.ds(r, 1), :],
                                  sem.at[slot]).start(priority=r % 2)

    def wait(slot):
        pltpu.make_async_copy(h2x_ref.at[pl.ds(0, MOE_TILE), :], bufs[slot], sem.at[slot]).wait()

    @pl.when(i == 0)
    def _():
        gather(0, 0)

    valid = meta_ref[2, i] != 0
    cw = EXPERT_COLS
    for cur in range(2):
        @pl.when(valid & (i % 2 == cur))
        def _(cur=cur):
            wait(cur)
            xb = bufs[cur][:, :d].astype(BF16)
            wts = bufs[cur][:, d:]
            n_parts = 2 * (wgu_a_ref.shape[2] + d) // cw
            per_part = MOE_TILE // n_parts
            part = [0]

            def pace(result):
                zero = result[0, 0].astype(I32) * meta_ref[3, i]
                gather(i + 1, 1 - cur, range(part[0] * per_part, (part[0] + 1) * per_part), zero)
                part[0] += 1

            def dot_cols(a, w_ref):
                chunks = []
                for c in range(w_ref.shape[2] // cw):
                    chunks.append(_dot(a, w_ref[0, :, c * cw:(c + 1) * cw]))
                    pace(chunks[-1])
                return jnp.concatenate(chunks, axis=1)

            y = jnp.zeros(y_ref.shape, F32)
            for e, (wgu_ref, wd_ref) in enumerate(((wgu_a_ref, wd_a_ref), (wgu_b_ref, wd_b_ref))):
                gu = dot_cols(xb, wgu_ref)
                ff = gu.shape[1] // 2
                act = _silu(gu[:, :ff]) * gu[:, ff:]
                y = y + wts[:, e:e + 1] * dot_cols(act.astype(BF16), wd_ref)
            y_ref[...] = y

        @pl.when(jnp.logical_not(valid) & (meta_ref[2, jnp.maximum(i - 1, 0)] != 0) & (i % 2 == cur))
        def _(cur=cur):
            wait(cur)

    @pl.when(jnp.logical_not(valid))
    def _():
        y_ref[...] = jnp.zeros(y_ref.shape, F32)


def _experts(meta, src, h2x, wgu, wd):
    n_slots = src.shape[0]
    w = h2x.shape[1]
    d = w - W_LANES
    n_tiles = n_slots // MOE_TILE
    return pl.pallas_call(
        _expert_kernel,
        grid_spec=pltpu.PrefetchScalarGridSpec(
            num_scalar_prefetch=2, grid=(n_tiles,),
            in_specs=[pl.BlockSpec(memory_space=pl.ANY),
                      pl.BlockSpec((1,) + wgu.shape[1:], lambda i, m, s: (m[0, i], 0, 0)),
                      pl.BlockSpec((1,) + wgu.shape[1:], lambda i, m, s: (m[1, i], 0, 0)),
                      pl.BlockSpec((1,) + wd.shape[1:], lambda i, m, s: (m[0, i], 0, 0)),
                      pl.BlockSpec((1,) + wd.shape[1:], lambda i, m, s: (m[1, i], 0, 0))],
            out_specs=pl.BlockSpec((MOE_TILE, d), lambda i, m, s: (i, 0)),
            scratch_shapes=[pltpu.VMEM((MOE_TILE, w), F32), pltpu.VMEM((MOE_TILE, w), F32),
                            pltpu.SemaphoreType.DMA((2,))]),
        out_shape=jax.ShapeDtypeStruct((n_slots, d), F32),
        compiler_params=_params(1), name="moe_experts",
    )(meta, src, h2x, wgu, wgu, wd, wd)


def _combine_kernel(dest_ref, ys_ref, x_ref, g2_ref, o_ref, buf0_ref, buf1_ref, sem):
    tm = x_ref.shape[1]
    step = pl.program_id(0) * pl.num_programs(1) + pl.program_id(1)
    n_steps = pl.num_programs(0) * pl.num_programs(1)
    bufs = (buf0_ref, buf1_ref)

    def gather(st, slot):
        for r in range(tm):
            pltpu.make_async_copy(ys_ref.at[pl.ds(dest_ref[st * tm + r], 1), :], bufs[slot].at[pl.ds(r, 1), :],
                                  sem.at[slot]).start(priority=r % 2)

    @pl.when(step == 0)
    def _():
        gather(0, 0)

    for cur in range(2):
        @pl.when(step % 2 == cur)
        def _(cur=cur):
            @pl.when(step + 1 < n_steps)
            def _():
                gather(step + 1, 1 - cur)

            pltpu.make_async_copy(ys_ref.at[pl.ds(0, tm), :], bufs[cur], sem.at[cur]).wait()
            o_ref[0] = x_ref[0] + g2_ref[0] * bufs[cur][...]


def _combine(dest, ys, x, g2, tm):
    bsz, s, d = x.shape
    return pl.pallas_call(
        _combine_kernel,
        grid_spec=pltpu.PrefetchScalarGridSpec(
            num_scalar_prefetch=1, grid=(bsz, s // tm),
            in_specs=[pl.BlockSpec(memory_space=pl.ANY),
                      pl.BlockSpec((1, tm, d), lambda b, i, dd: (b, i, 0)),
                      pl.BlockSpec((1, 1, d), lambda b, i, dd: (b, 0, 0))],
            out_specs=pl.BlockSpec((1, tm, d), lambda b, i, dd: (b, i, 0)),
            scratch_shapes=[pltpu.VMEM((tm, d), F32), pltpu.VMEM((tm, d), F32), pltpu.SemaphoreType.DMA((2,))]),
        out_shape=jax.ShapeDtypeStruct((bsz, s, d), F32),
        compiler_params=_params(2), name="moe_combine",
    )(dest, ys, x, g2)


def _moe(x_new, h2x, cls, cnt, g2, wgu, wd):
    bsz, s, d = x_new.shape
    t = bsz * s
    tr = min(512, t)
    dest, meta = _moe_plan(cls.reshape(t // tr, 1, tr), cnt, tr)
    n_slots = t + N_CLASSES * MOE_TILE
    ys = _experts(meta, _invert(dest, n_slots), h2x, wgu, wd)
    return _combine(dest, ys, x_new, g2, min(256, s))


def kernel(x, c, ada_w, ada_b, norm_mix, norm_ffn, hy_w_in, hy_q_norm, hy_k_norm, hy_dw_w, hy_dw_b, hy_ln_g, hy_ln_b, hy_w_out, ssm_w_in, ssm_conv_w, ssm_conv_b, ssm_dt_bias, ssm_a_log, ssm_d, ssm_norm, ssm_w_out, router_w, router_bias, exp_w_gate, exp_w_up, exp_w_down):
    bsz, s, d = x.shape
    depth = ada_w.shape[0]
    assert s % 512 == 0 and bsz <= 8
    c8 = jnp.zeros((8, d), F32).at[:bsz].set(c)
    mod = _ada(c8, ada_w, ada_b)
    rwt = router_w.T
    rb = router_bias.reshape(N_EXPERTS, 1)
    lane_head = jnp.arange(ATT_WIDTH) // HEAD_DIM
    pmat = ((lane_head[:, None] == lane_head[None, :]).astype(F32) / HEAD_DIM).astype(BF16)

    for layer in range(depth):
        sh1, sc1, g1, sh2, sc2, g2 = (mod[layer, :bsz, i * d:(i + 1) * d].reshape(bsz, 1, d) for i in range(N_MOD))
        gm = norm_mix[layer].reshape(1, d)
        gf = norm_ffn[layer].reshape(1, d)
        j = layer // 2
        if layer % 2 == 0:
            w_in = hy_w_in[j].astype(BF16)
            aw = ATT_WIDTH
            k, qt, vt, u, kmean = _hy_in(
                x, sc1, sh1, gm, w_in[:, aw:2 * aw], w_in[:, 3 * aw:], w_in[:, :aw].T, w_in[:, 2 * aw:3 * aw].T,
                (jnp.tile(hy_q_norm[j], ATT_HEADS) * (HEAD_DIM ** -0.5 * LOG2E)).reshape(aw, 1),
                jnp.tile(hy_k_norm[j], ATT_HEADS).reshape(1, aw), pmat, 512)
            bias = _gate(kmean.reshape(bsz, s // MOBA_BLOCK, aw), qt, min(1024, s))
            att_t = _attn(qt, k, vt, bias, 8)
            w_out = hy_w_out[j].astype(BF16)
            dww = jnp.zeros((CONV_HALO, hy_dw_w.shape[2]), F32).at[:CONV_WIDTH].set(hy_dw_w[j])
            x_new, h2x, cls, cnt = _hy_out(
                u, att_t, x, g1, w_out[:aw], w_out[aw:], dww, hy_dw_b[j].reshape(1, -1),
                hy_ln_g[j].reshape(1, -1), hy_ln_b[j].reshape(1, -1), gf, sc2, sh2, rwt, rb, 256)
        else:
            w_in = ssm_w_in[j]
            inner = ssm_norm.shape[1]
            heads = ssm_a_log.shape[1]
            hpg = heads // SSM_GROUPS
            conv_dim = ssm_conv_w.shape[2]

            def by_group(v):
                v = v.reshape(v.shape[:-1] + (SSM_GROUPS, hpg))
                pad = [(0, 0)] * (v.ndim - 1) + [(0, 128 - hpg)]
                return jnp.pad(v, pad).reshape(v.shape[:-2] + (SSM_GROUPS * 128,))

            sz, xbc, dt = _ssm_in(
                x, sc1, sh1, gm, w_in[:, :inner].astype(BF16), w_in[:, inner:inner + conv_dim].astype(BF16),
                by_group(w_in[:, inner + conv_dim:]).astype(BF16), ssm_conv_w[j], ssm_conv_b[j].reshape(1, -1),
                by_group(ssm_dt_bias[j]).reshape(1, -1), 256)
            chan_head = jnp.arange(inner // SSM_GROUPS) // SSM_HEAD_DIM
            e2 = (jnp.arange(256)[:, None] % 128 == chan_head[None, :]).astype(BF16)
            x_new, h2x, cls, cnt = _ssd(
                xbc, sz, dt, by_group(ssm_a_log[j]).reshape(SSM_GROUPS, 1, 128),
                jnp.repeat(ssm_d[j], SSM_HEAD_DIM).reshape(SSM_GROUPS, 1, -1),
                ssm_norm[j].reshape(SSM_GROUPS, 1, -1), ssm_w_out[j].astype(BF16), e2, x, g1, gf, sc2, sh2, rwt, rb)
        wgu = jnp.concatenate([exp_w_gate[layer], exp_w_up[layer]], axis=-1).astype(BF16)
        x = _moe(x_new, h2x, cls, cnt, g2, wgu, exp_w_down[layer].astype(BF16))
    return x
```

```python
import functools

import jax
import jax.numpy as jnp
from jax import lax
from jax.experimental import pallas as pl
from jax.experimental.pallas import tpu as pltpu

F32 = jnp.float32
BF16 = jnp.bfloat16
I32 = jnp.int32

NORM_EPS = 1e-6
N_MOD = 6
HEAD_DIM = 64
ATT_HEADS = 8
ATT_WIDTH = ATT_HEADS * HEAD_DIM
MOBA_BLOCK = 256
MOBA_TOPK = 3
CONV_WIDTH = 31
CONV_HALO = 32
SSM_HEAD_DIM = 64
SSM_GROUPS = 4
SSM_STATE = 128
SSM_CONV = 4
SSM_CHUNK = 256
SSM_HALO = 8
SSD_SUB = 128
SSM_IN_COLS = 512
N_EXPERTS = 16
N_EXPERT_GROUPS = 4
EXPERTS_PER_GROUP = 4
N_PAIRS = 6
N_CLASSES = N_EXPERT_GROUPS * N_PAIRS
CLASS_ROWS = 32
MOE_TILE = 256
EXPERT_COLS = 256
W_LANES = 128
ROW_DMA_UNROLL = 8
LOG2E = 1.4426950408889634
NEG = -1e30
VMEM_LIMIT = 56 * 1024 * 1024


def _params(n_axes):
    return pltpu.CompilerParams(dimension_semantics=("arbitrary",) * n_axes,
                                vmem_limit_bytes=VMEM_LIMIT)


def _dot(a, b):
    return jnp.dot(a, b, preferred_element_type=F32)


def _dot_nt(a, b):
    return lax.dot_general(a, b, (((1,), (1,)), ((), ())), preferred_element_type=F32)


def _split2(x):
    hi = x.astype(BF16)
    lo = (x - hi.astype(F32)).astype(BF16)
    return hi, lo


def _split3(x):
    a = x.astype(BF16)
    r = x - a.astype(F32)
    b = r.astype(BF16)
    c = (r - b.astype(F32)).astype(BF16)
    return a, b, c


def _silu(x):
    return x * jax.nn.sigmoid(x)


def _mod_norm(x, g, sc, sh):
    ms = jnp.mean(x * x, axis=-1, keepdims=True)
    return x * lax.rsqrt(ms + NORM_EPS) * g * (1.0 + sc) + sh


def _ada_kernel(c_ref, w_ref, b_ref, o_ref):
    cond = _silu(c_ref[...])
    ch, cl = _split2(cond)
    wh, wl = _split2(w_ref[0])
    o_ref[0] = _dot(ch, wh) + _dot(ch, wl) + _dot(cl, wh) + b_ref[0]


def _ada(c8, ada_w, ada_b):
    depth, d, n = ada_w.shape
    tn = 1536
    return pl.pallas_call(
        _ada_kernel,
        grid=(depth, n // tn),
        in_specs=[pl.BlockSpec((8, d), lambda l, j: (0, 0)),
                  pl.BlockSpec((1, d, tn), lambda l, j: (l, 0, j)),
                  pl.BlockSpec((1, 1, tn), lambda l, j: (l, 0, j))],
        out_specs=pl.BlockSpec((1, 8, tn), lambda l, j: (l, 0, j)),
        out_shape=jax.ShapeDtypeStruct((depth, 8, n), F32),
        compiler_params=_params(2), name="ada_mod",
    )(c8, ada_w, ada_b.reshape(depth, 1, n))


def _hy_in_kernel(x_ref, sc_ref, sh_ref, g_ref, wk_ref, wu_ref, wqt_ref, wvt_ref, qg_ref, kg_ref, p_ref,
                  k_ref, qt_ref, vt_ref, u_ref, km_ref):
    tm = x_ref.shape[1]
    h = _mod_norm(x_ref[0], g_ref[...], sc_ref[0], sh_ref[0])
    hb = h.astype(BF16)
    k = _dot(hb, wk_ref[...])
    khi, klo = _split2(k * k)
    ms = _dot(khi, p_ref[...]) + _dot(klo, p_ref[...])
    kn = k * lax.rsqrt(ms + NORM_EPS) * kg_ref[...]
    k_ref[0] = kn.astype(BF16)
    km_ref[0, 0] = jnp.mean(kn.reshape(tm // MOBA_BLOCK, MOBA_BLOCK, ATT_WIDTH), axis=1)
    qt = _dot_nt(wqt_ref[...], hb).reshape(ATT_HEADS, HEAD_DIM, tm)
    qms = jnp.mean(qt * qt, axis=1, keepdims=True)
    qn = (qt * lax.rsqrt(qms + NORM_EPS)).reshape(ATT_WIDTH, tm) * qg_ref[...]
    qt_ref[0] = qn.astype(BF16)
    vt_ref[0] = _dot_nt(wvt_ref[...], hb).astype(BF16)
    ag = _dot(hb, wu_ref[...])
    half = ag.shape[1] // 2
    u_ref[0] = (ag[:, :half] * jax.nn.sigmoid(ag[:, half:])).astype(BF16)


def _hy_in(x, sc, sh, g, wk, wu, wqt, wvt, qg, kg, pmat, tm):
    bsz, s, d = x.shape
    cc = wu.shape[1] // 2
    full = lambda shape: pl.BlockSpec(shape, lambda b, i: (0,) * len(shape))
    return pl.pallas_call(
        _hy_in_kernel,
        grid=(bsz, s // tm),
        in_specs=[pl.BlockSpec((1, tm, d), lambda b, i: (b, i, 0)),
                  pl.BlockSpec((1, 1, d), lambda b, i: (b, 0, 0)),
                  pl.BlockSpec((1, 1, d), lambda b, i: (b, 0, 0)),
                  full((1, d)), full(wk.shape), full(wu.shape), full(wqt.shape), full(wvt.shape),
                  full(qg.shape), full(kg.shape), full(pmat.shape)],
        out_specs=[pl.BlockSpec((1, tm, ATT_WIDTH), lambda b, i: (b, i, 0)),
                   pl.BlockSpec((1, ATT_WIDTH, tm), lambda b, i: (b, 0, i)),
                   pl.BlockSpec((1, ATT_WIDTH, tm), lambda b, i: (b, 0, i)),
                   pl.BlockSpec((1, tm, cc), lambda b, i: (b, i, 0)),
                   pl.BlockSpec((1, 1, tm // MOBA_BLOCK, ATT_WIDTH), lambda b, i: (b, i, 0, 0))],
        out_shape=[jax.ShapeDtypeStruct((bsz, s, ATT_WIDTH), BF16),
                   jax.ShapeDtypeStruct((bsz, ATT_WIDTH, s), BF16),
                   jax.ShapeDtypeStruct((bsz, ATT_WIDTH, s), BF16),
                   jax.ShapeDtypeStruct((bsz, s, cc), BF16),
                   jax.ShapeDtypeStruct((bsz, s // tm, tm // MOBA_BLOCK, ATT_WIDTH), F32)],
        compiler_params=_params(2), name="hy_in_proj",
    )(x, sc, sh, g, wk, wu, wqt, wvt, qg, kg, pmat)


def _gate_kernel(km_ref, qt_ref, o_ref):
    nb = km_ref.shape[1]
    ts = qt_ref.shape[2]
    rows = ATT_HEADS * nb
    km = km_ref[0]
    kmt = jnp.broadcast_to(km[None], (ATT_HEADS, nb, ATT_WIDTH)).reshape(rows, ATT_WIDTH)
    rh = lax.broadcasted_iota(I32, (rows, ATT_WIDTH), 0) // nb
    ch = lax.broadcasted_iota(I32, (rows, ATT_WIDTH), 1) // HEAD_DIM
    kbd = jnp.where(rh == ch, kmt, 0.0)
    khi, klo = _split2(kbd)
    qt = qt_ref[0]
    gate = (_dot(khi, qt) + _dot(klo, qt)).reshape(ATT_HEADS, nb, ts)
    blk = lax.broadcasted_iota(I32, (ATT_HEADS, nb, ts), 1)
    qblk = (pl.program_id(1) * ts + lax.broadcasted_iota(I32, (ATT_HEADS, nb, ts), 2)) // MOBA_BLOCK
    g = jnp.where(blk < qblk, gate, -jnp.inf)
    sel = jnp.zeros(g.shape, jnp.bool_)
    for _ in range(MOBA_TOPK):
        m = jnp.max(g, axis=1, keepdims=True)
        first = jnp.min(jnp.where((g == m) & (m > -jnp.inf), blk, nb), axis=1, keepdims=True)
        pick = blk == first
        sel = sel | pick
        g = jnp.where(pick, -jnp.inf, g)
    o_ref[0] = jnp.where(sel, 0.0, NEG).reshape(rows, ts)


def _gate(kmean, qt, ts):
    bsz, nb, _ = kmean.shape
    s = qt.shape[2]
    rows = ATT_HEADS * nb
    return pl.pallas_call(
        _gate_kernel,
        grid=(bsz, s // ts),
        in_specs=[pl.BlockSpec((1, nb, ATT_WIDTH), lambda b, i: (b, 0, 0)),
                  pl.BlockSpec((1, ATT_WIDTH, ts), lambda b, i: (b, 0, i))],
        out_specs=pl.BlockSpec((1, rows, ts), lambda b, i: (b, 0, i)),
        out_shape=jax.ShapeDtypeStruct((bsz, rows, s), F32),
        compiler_params=_params(2), name="moba_gate",
    )(kmean, qt)


def _attn_kernel(qt_ref, k_ref, vt_ref, bias_ref, o_ref, qa_ref, acc_ref, m_ref, *, nb, hp, lag, kb):
    blk = MOBA_BLOCK
    pw = 2 * HEAD_DIM
    va = HEAD_DIM + 16
    i = pl.program_id(2)
    row = lax.broadcasted_iota(I32, (pw, blk), 0)
    for h in range(hp):
        qp = qt_ref[0, (h // 2) * pw:(h // 2 + 1) * pw, :]
        keep = (row < HEAD_DIM) if h % 2 == 0 else (row >= HEAD_DIM)
        qa_ref[h, :pw, :] = jnp.where(keep, qp, jnp.zeros_like(qp))
        qa_ref[h, pw:pw + nb, :] = bias_ref[0, h * nb:(h + 1) * nb, :].astype(BF16)
        qa_ref[h, pw + nb:, :] = jnp.zeros((blk - pw - nb, blk), BF16)
        m_ref[h] = jnp.full((1, blk), NEG, F32)
        acc_ref[h] = jnp.zeros((va, blk), F32)
    causal = lax.broadcasted_iota(I32, (blk, blk), 0) <= lax.broadcasted_iota(I32, (blk, blk), 1)
    lane = lax.broadcasted_iota(I32, (blk, pw), 1)
    ones = jnp.ones((va - HEAD_DIM, blk), BF16)

    def step(blocks):
        units = [(b, h) for b in range(len(blocks)) for h in range(hp)]
        offs = [pl.multiple_of(j * blk, blk) for j, _, _ in blocks]

        def scores(u):
            b, h = units[u]
            pp = h // 2
            ka = jnp.concatenate([k_ref[0, pl.ds(offs[b], blk), pp * pw:(pp + 1) * pw], blocks[b][1]], axis=1)
            return blocks[b][2](_dot(ka, qa_ref[h]))

        def softmax(u, s):
            h = units[u][1]
            m_old = m_ref[h]
            m_new = jnp.maximum(m_old, jnp.max(s, axis=0, keepdims=True))
            m_ref[h] = m_new
            return jnp.exp2(s - m_new).astype(BF16), jnp.exp2(m_old - m_new)

        def accumulate(u, p, alpha):
            b, h = units[u]
            vj = jnp.concatenate([vt_ref[0, h * HEAD_DIM:(h + 1) * HEAD_DIM, pl.ds(offs[b], blk)], ones], axis=0)
            acc_ref[h] = alpha * acc_ref[h] + _dot(vj, p)

        s, pa = {}, {}
        for t in range(len(units) + lag):
            if t < len(units):
                s[t] = scores(t)
            if 1 <= t <= len(units):
                pa[t - 1] = softmax(t - 1, s.pop(t - 1))
            if t >= lag:
                accumulate(t - lag, *pa.pop(t - lag))

    def past(j):
        return (j, jnp.where(lane == j, 1.0, 0.0).astype(BF16), lambda s: s)

    step([(i, jnp.zeros((blk, pw), BF16), lambda s: jnp.where(causal, s, NEG))])
    rem = i % kb
    base = jnp.int32(0)
    size = 1
    while size < kb:
        take = (rem & size) != 0

        @pl.when(take)
        def _(base=base, size=size):
            step([past(base + t) for t in range(size)])

        base = base + jnp.where(take, size, 0)
        size *= 2

    def body(r, carry):
        step([past(rem + kb * r + t) for t in range(kb)])
        return carry

    lax.fori_loop(0, i // kb, body, 0)
    for h in range(hp):
        acc = acc_ref[h]
        o_ref[0, h * HEAD_DIM:(h + 1) * HEAD_DIM, :] = (acc[:HEAD_DIM] / acc[HEAD_DIM:HEAD_DIM + 1]).astype(BF16)


def _attn(qt, k, vt, bias, hp):
    bsz, s, _ = k.shape
    nb = s // MOBA_BLOCK
    assert nb <= MOBA_BLOCK - 2 * HEAD_DIM and nb % 16 == 0
    hw = hp * HEAD_DIM
    return pl.pallas_call(
        functools.partial(_attn_kernel, nb=nb, hp=hp, lag=6, kb=4),
        grid=(bsz, ATT_HEADS // hp, nb),
        in_specs=[pl.BlockSpec((1, hw, MOBA_BLOCK), lambda b, p, i: (b, p, i)),
                  pl.BlockSpec((1, s, hw), lambda b, p, i: (b, 0, p)),
                  pl.BlockSpec((1, hw, s), lambda b, p, i: (b, p, 0)),
                  pl.BlockSpec((1, hp * nb, MOBA_BLOCK), lambda b, p, i: (b, p, i))],
        out_specs=pl.BlockSpec((1, hw, MOBA_BLOCK), lambda b, p, i: (b, p, i)),
        out_shape=jax.ShapeDtypeStruct((bsz, ATT_WIDTH, s), BF16),
        scratch_shapes=[pltpu.VMEM((hp, MOBA_BLOCK, MOBA_BLOCK), BF16),
                        pltpu.VMEM((hp, HEAD_DIM + 16, MOBA_BLOCK), F32),
                        pltpu.VMEM((hp, 1, MOBA_BLOCK), F32)],
        compiler_params=_params(3), name="moba_attn",
    )(qt, k, vt, bias)


def _route(logit_t, rb):
    aff = jax.nn.sigmoid(logit_t)
    score = aff + rb
    s = [score[e:e + 1, :] for e in range(N_EXPERTS)]
    a = [aff[e:e + 1, :] for e in range(N_EXPERTS)]
    n = EXPERTS_PER_GROUP

    def top2_sum(v):
        best = None
        for x in range(n):
            for y in range(x + 1, n):
                t = v[x] + v[y]
                best = t if best is None else jnp.maximum(best, t)
        return best

    def first_argmax(v):
        idx = jnp.zeros(v[0].shape, I32)
        cur = v[0]
        for x in range(1, len(v)):
            better = v[x] > cur
            idx = jnp.where(better, x, idx)
            cur = jnp.where(better, v[x], cur)
        return idx

    grp = first_argmax([top2_sum(s[n * g:n * g + n]) for g in range(N_EXPERT_GROUPS)])

    def in_group(v, x):
        out = v[(N_EXPERT_GROUPS - 1) * n + x]
        for g in range(N_EXPERT_GROUPS - 2, -1, -1):
            out = jnp.where(grp == g, v[n * g + x], out)
        return out

    sg = [in_group(s, x) for x in range(n)]
    ag = [in_group(a, x) for x in range(n)]
    l1 = first_argmax(sg)
    l2 = first_argmax([jnp.where(l1 == x, -jnp.inf, sg[x]) for x in range(n)])
    lo = jnp.minimum(l1, l2)
    hi = jnp.maximum(l1, l2)
    pair = jnp.where(lo == 0, hi - 1, jnp.where(lo == 1, hi + 1, 5))
    a_lo = jnp.zeros_like(ag[0])
    a_hi = jnp.zeros_like(ag[0])
    for x in range(n):
        a_lo = jnp.where(lo == x, ag[x], a_lo)
        a_hi = jnp.where(hi == x, ag[x], a_hi)
    tot = a_lo + a_hi
    return grp * N_PAIRS + pair, a_lo / tot, a_hi / tot


def _post_mixer(x_new, g_ref, sc_ref, sh_ref, rwt_ref, rb_ref, h2x_ref, cls_ref, cnt_ref):
    tm, d = x_new.shape
    h2 = _mod_norm(x_new, g_ref[...], sc_ref[0], sh_ref[0])
    hh, hl = _split2(h2)
    rh, rl = _split2(rwt_ref[...])
    logit_t = _dot_nt(rh, hh) + _dot_nt(rh, hl) + _dot_nt(rl, hh)
    cls, w_lo, w_hi = _route(logit_t, rb_ref[...])
    cls_ref[0] = cls

    @pl.when((pl.program_id(0) == 0) & (pl.program_id(1) == 0))
    def _():
        cnt_ref[...] = jnp.zeros(cnt_ref.shape, F32)

    onehot = lax.broadcasted_iota(I32, (CLASS_ROWS, tm), 0) == cls
    cnt_ref[...] += jnp.sum(jnp.where(onehot, 1.0, 0.0), axis=1, keepdims=True)
    wrow = lax.broadcasted_iota(I32, (W_LANES, tm), 0)
    wt = jnp.where(wrow == 0, w_lo, jnp.where(wrow == 1, w_hi, 0.0))
    h2x_ref[:, :d] = h2
    h2x_ref[:, d:] = wt.T


def _hy_out_kernel(u_ref, up_ref, at_ref, x_ref, g1_ref, wtop_ref, wbot_ref, dww_ref, dwb_ref, lng_ref, lnb_ref,
                   g_ref, sc_ref, sh_ref, rwt_ref, rb_ref,
                   xn_ref, h2x_ref, cls_ref, cnt_ref, cat_ref, shifted_ref):
    tm = u_ref.shape[1]
    m_att = _dot(at_ref[0].astype(F32).T.astype(BF16), wtop_ref[...])
    prev = up_ref[0].astype(F32)
    cat_ref[:CONV_HALO, :] = jnp.where(pl.program_id(1) == 0, 0.0, prev)
    cat_ref[CONV_HALO:, :] = u_ref[0].astype(F32)
    span = tm + CONV_HALO - 8
    for r in range(1, 8):
        shifted_ref[r - 1] = cat_ref[pl.ds(r, span), :]
    y = jnp.zeros((tm, u_ref.shape[2]), F32) + dwb_ref[...]
    for j in range(CONV_WIDTH):
        a, r = divmod(CONV_HALO - CONV_WIDTH + 1 + j, 8)
        tap = cat_ref[pl.ds(8 * a, tm), :] if r == 0 else shifted_ref[r - 1, pl.ds(8 * a, tm), :]
        y = y + dww_ref[j:j + 1, :] * tap
    mu = jnp.mean(y, axis=-1, keepdims=True)
    var = jnp.mean(jnp.square(y - mu), axis=-1, keepdims=True)
    cv = _silu((y - mu) * lax.rsqrt(var + NORM_EPS) * lng_ref[...] + lnb_ref[...])
    m = m_att + _dot(cv.astype(BF16), wbot_ref[...])
    x_new = x_ref[0] + g1_ref[0] * m
    xn_ref[0] = x_new
    _post_mixer(x_new, g_ref, sc_ref, sh_ref, rwt_ref, rb_ref, h2x_ref, cls_ref, cnt_ref)


def _hy_out(u, att_t, x, g1, wtop, wbot, dww, dwb, lng, lnb, g, sc, sh, rwt, rb, tm):
    bsz, s, d = x.shape
    cc = u.shape[2]
    nt = s // tm
    full = lambda shape: pl.BlockSpec(shape, lambda b, i: (0,) * len(shape))
    per_b = pl.BlockSpec((1, 1, d), lambda b, i: (b, 0, 0))
    halo = tm // CONV_HALO
    return pl.pallas_call(
        _hy_out_kernel,
        grid=(bsz, nt),
        in_specs=[pl.BlockSpec((1, tm, cc), lambda b, i: (b, i, 0)),
                  pl.BlockSpec((1, CONV_HALO, cc), lambda b, i: (b, jnp.maximum(i * halo - 1, 0), 0)),
                  pl.BlockSpec((1, ATT_WIDTH, tm), lambda b, i: (b, 0, i)),
                  pl.BlockSpec((1, tm, d), lambda b, i: (b, i, 0)),
                  per_b, full(wtop.shape), full(wbot.shape), full(dww.shape), full(dwb.shape),
                  full(lng.shape), full(lnb.shape), full(g.shape), per_b, per_b, full(rwt.shape), full(rb.shape)],
        out_specs=[pl.BlockSpec((1, tm, d), lambda b, i: (b, i, 0)),
                   pl.BlockSpec((tm, d + W_LANES), lambda b, i: (b * nt + i, 0)),
                   pl.BlockSpec((1, 1, tm), lambda b, i: (b * nt + i, 0, 0)),
                   pl.BlockSpec((CLASS_ROWS, 128), lambda b, i: (0, 0))],
        out_shape=[jax.ShapeDtypeStruct((bsz, s, d), F32),
                   jax.ShapeDtypeStruct((bsz * s, d + W_LANES), F32),
                   jax.ShapeDtypeStruct((bsz * nt, 1, tm), I32),
                   jax.ShapeDtypeStruct((CLASS_ROWS, 128), F32)],
        scratch_shapes=[pltpu.VMEM((tm + CONV_HALO, cc), F32),
                        pltpu.VMEM((7, tm + CONV_HALO - 8, cc), F32)],
        compiler_params=_params(2), name="hy_out_proj",
    )(u, u, att_t, x, g1, wtop, wbot, dww, dwb, lng, lnb, g, sc, sh, rwt, rb)


def _ssm_in_kernel(dest_ref, opaque_ref, ys_ref, xn_ref, g2_ref, sc_ref, sh_ref, g_ref, wz_ref, wx_ref, wdt_ref,
                   cw_ref, cb_ref, dtb_ref, x_ref, sz_ref, xbc_ref, dt_ref, buf_ref, sem, *cat_refs):
    tm = xn_ref.shape[1]
    ch = SSM_IN_COLS
    step = pl.program_id(0) * pl.num_programs(1) + pl.program_id(1)
    last = pl.num_programs(0) * pl.num_programs(1) - 1
    cur = step % 2
    ahead = jnp.minimum(step + 1, last)

    def gather(tile, slot, rows, after=0):
        for r in rows:
            row = dest_ref[tile * tm + r] + after
            pltpu.make_async_copy(ys_ref.at[pl.ds(row, 1), :], buf_ref.at[slot, pl.ds(r, 1), :],
                                  sem.at[slot]).start(priority=r % 2)

    def wait(slot):
        pltpu.make_async_copy(ys_ref.at[pl.ds(0, tm), :], buf_ref.at[slot], sem.at[slot]).wait()

    @pl.when(step == 0)
    def _():
        gather(0, 0, range(tm))

    @pl.when(pl.program_id(1) == 0)
    def _():
        for cat_ref in cat_refs:
            cat_ref[:SSM_HALO, :] = jnp.zeros((SSM_HALO, ch), F32)

    wait(cur)
    x = xn_ref[0] + g2_ref[0] * buf_ref[cur]
    x_ref[0] = x
    h = _mod_norm(x, g_ref[...], sc_ref[0], sh_ref[0])
    hb = h.astype(BF16)

    def conv_matmul(c):
        v = _dot(hb, wx_ref[:, c * ch:(c + 1) * ch])
        cat_refs[c][SSM_HALO:, :] = v
        return v

    def conv_act(c, _):
        cols = slice(c * ch, (c + 1) * ch)
        cat_ref = cat_refs[c]
        y = jnp.zeros((tm, ch), F32) + cb_ref[:, cols]
        for j in range(SSM_CONV):
            y = y + cw_ref[j:j + 1, cols] * cat_ref[pl.ds(SSM_HALO - SSM_CONV + 1 + j, tm), :]
        xbc_ref[0, :, cols] = _silu(y).astype(BF16)
        cat_ref[:SSM_HALO, :] = cat_ref[pl.ds(tm, SSM_HALO), :]

    def gate_matmul(c):
        return _dot(hb, wz_ref[:, c * ch:(c + 1) * ch])

    def gate_act(c, z):
        sz_ref[0, :, c * ch:(c + 1) * ch] = _silu(z).astype(BF16)

    stages = [(conv_matmul, conv_act, c) for c in range(wx_ref.shape[1] // ch)]
    stages += [(gate_matmul, gate_act, c) for c in range(wz_ref.shape[1] // ch)]
    shares = 8
    per_share = tm // shares
    gather(ahead, 1 - cur, range(per_share))
    pending = None
    for k, (matmul, act, c) in enumerate(stages):
        out = matmul(c)
        if k + 1 < shares:
            zero = out[0, 0].astype(I32) * opaque_ref[0]
            gather(ahead, 1 - cur, range((k + 1) * per_share, (k + 2) * per_share), zero)
        if pending is not None:
            pending[0](pending[1], pending[2])
        pending = (act, c, out)
    t = _dot(hb, wdt_ref[...]) + dtb_ref[...]
    pending[0](pending[1], pending[2])
    dt_ref[0] = jnp.maximum(t, 0.0) + jnp.log(1.0 + jnp.exp(-jnp.abs(t)))

    @pl.when(step == last)
    def _():
        wait(1 - cur)


def _ssm_in(dest, ys, x_new, g2, sc, sh, g, wz, wx, wdt, cw, cb, dtb, tm):
    bsz, s, d = x_new.shape
    full = lambda shape: pl.BlockSpec(shape, lambda b, i, *_: (0,) * len(shape))
    per_b = pl.BlockSpec((1, 1, d), lambda b, i, *_: (b, 0, 0))
    tile = lambda n: pl.BlockSpec((1, tm, n), lambda b, i, *_: (b, i, 0))
    return pl.pallas_call(
        _ssm_in_kernel,
        grid_spec=pltpu.PrefetchScalarGridSpec(
            num_scalar_prefetch=2, grid=(bsz, s // tm),
            in_specs=[pl.BlockSpec(memory_space=pl.ANY), tile(d), per_b, per_b, per_b, full(g.shape),
                      full(wz.shape), full(wx.shape), full(wdt.shape), full(cw.shape), full(cb.shape),
                      full(dtb.shape)],
            out_specs=[tile(d), tile(wz.shape[1]), tile(wx.shape[1]), tile(wdt.shape[1])],
            scratch_shapes=[pltpu.VMEM((2, tm, d), F32), pltpu.SemaphoreType.DMA((2,))]
            + [pltpu.VMEM((tm + SSM_HALO, SSM_IN_COLS), F32)] * (wx.shape[1] // SSM_IN_COLS)),
        out_shape=[jax.ShapeDtypeStruct((bsz, s, d), F32),
                   jax.ShapeDtypeStruct((bsz, s, wz.shape[1]), BF16),
                   jax.ShapeDtypeStruct((bsz, s, wx.shape[1]), BF16),
                   jax.ShapeDtypeStruct((bsz, s, wdt.shape[1]), F32)],
        compiler_params=_params(2), name="ssm_in_proj",
    )(dest, jnp.zeros((1,), I32), ys, x_new, g2, sc, sh, g, wz, wx, wdt, cw, cb, dtb)


def _ssd_kernel(xbc_ref, sz_ref, dt_ref, alog_ref, dsk_ref, ng_ref, wout_ref, e2_ref, x_ref, g1_ref,
                g_ref, sc_ref, sh_ref, rwt_ref, rb_ref,
                xn_ref, h2x_ref, cls_ref, cnt_ref, state_ref, macc_ref, y_ref):
    L = SSD_SUB
    inner = sz_ref.shape[2]
    gw = inner // SSM_GROUPS
    pw = 2 * SSM_HEAD_DIM

    @pl.when(pl.program_id(1) == 0)
    def _():
        state_ref[...] = jnp.zeros(state_ref.shape, F32)

    subs = [slice(s * L, (s + 1) * L) for s in range(SSM_CHUNK // L)]
    brow = lax.broadcasted_iota(I32, (SSM_CHUNK, SSM_CHUNK), 0)
    bcol = lax.broadcasted_iota(I32, (SSM_CHUNK, SSM_CHUNK), 1)
    tri = jnp.where((brow // L == bcol // L) & (brow >= bcol), 1.0, 0.0).astype(BF16)
    lower = lax.broadcasted_iota(I32, (L, L), 0) >= lax.broadcasted_iota(I32, (L, L), 1)
    lo_half = lax.broadcasted_iota(I32, (L, pw), 1) < SSM_HEAD_DIM

    def expand(v):
        return _dot(jnp.concatenate(_split2(v), axis=1), e2_ref[...])

    macc_ref[...] = jnp.zeros(macc_ref.shape, F32)

    def group(g, carry):
        xcol = pl.ds(pl.multiple_of(g * gw, gw), gw)
        bcolumns = pl.ds(pl.multiple_of(inner + g * SSM_STATE, SSM_STATE), SSM_STATE)
        ccolumns = pl.ds(pl.multiple_of(inner + (SSM_GROUPS + g) * SSM_STATE, SSM_STATE), SSM_STATE)
        rate = -jnp.exp(alog_ref[g]) * LOG2E
        dt = dt_ref[0, :, pl.ds(pl.multiple_of(g * 128, 128), 128)]
        t3 = _dot(tri, jnp.concatenate(_split3(dt * rate), axis=1))
        a_cs = t3[:, :128] + t3[:, 128:256] + t3[:, 256:]
        xg = xbc_ref[0, :, xcol].astype(F32)
        xdt = xg * expand(dt)
        xdt_b = xdt.astype(BF16)
        bs = [xbc_ref[0, r, bcolumns] for r in subs]
        cs = [xbc_ref[0, r, ccolumns] for r in subs]
        cbm = [jnp.where(lower, _dot_nt(cm, bm), 0.0) for cm, bm in zip(cs, bs)]
        a_end = jnp.concatenate([jnp.broadcast_to(a_cs[r.stop - 1:r.stop, :], (L, 128)) for r in subs], axis=0)
        xw = (xdt * expand(jnp.exp2(a_end - a_cs))).astype(BF16)
        grow = expand(jnp.exp2(a_cs))
        keep = expand(jnp.exp2(jnp.concatenate([a_end[r.start:r.start + 16, :] for r in subs], axis=0)))
        st = state_ref[g]
        y_off = []
        for s, r in enumerate(subs):
            y_off.append(_dot(cs[s], st.astype(BF16)) * grow[r, :])
            st = st * keep[16 * s:16 * s + 1, :] + _dot(bs[s].astype(F32).T.astype(BF16), xw[r, :])
        state_ref[g] = st
        for s, r in enumerate(subs):
            a_sub = a_cs[r, :]
            a_sub_t = a_sub.T
            for q in range(gw // pw):
                x2 = xdt_b[r, q * pw:(q + 1) * pw]
                yp = y_off[s][:, q * pw:(q + 1) * pw]
                for e in range(2):
                    hd = 2 * q + e
                    dec = jnp.exp2(jnp.minimum(a_sub[:, hd:hd + 1] - a_sub_t[hd:hd + 1, :], 0.0))
                    xm = jnp.where(lo_half if e == 0 else jnp.logical_not(lo_half), x2, jnp.zeros_like(x2))
                    yp = yp + _dot((cbm[s] * dec).astype(BF16), xm)
                y_ref[r, q * pw:(q + 1) * pw] = yp
        gt = (y_ref[...] + xg * dsk_ref[g]) * sz_ref[0, :, xcol].astype(F32)
        ms = jnp.mean(gt * gt, axis=-1, keepdims=True)
        gn = gt * lax.rsqrt(ms + NORM_EPS) * ng_ref[g]
        macc_ref[...] += _dot(gn.astype(BF16), wout_ref[xcol, :])
        return carry

    lax.fori_loop(0, SSM_GROUPS, group, 0)
    x_new = x_ref[0] + g1_ref[0] * macc_ref[...]
    xn_ref[0] = x_new
    _post_mixer(x_new, g_ref, sc_ref, sh_ref, rwt_ref, rb_ref, h2x_ref, cls_ref, cnt_ref)


def _ssd(xbc, sz, dt, alog, dskip, ng, wout, e2, x, g1, g, sc, sh, rwt, rb):
    bsz, s, d = x.shape
    L = SSM_CHUNK
    nc = s // L
    gw = sz.shape[2] // SSM_GROUPS
    full = lambda shape: pl.BlockSpec(shape, lambda b, c: (0,) * len(shape))
    per_b = pl.BlockSpec((1, 1, d), lambda b, c: (b, 0, 0))
    tile = lambda n: pl.BlockSpec((1, L, n), lambda b, c: (b, c, 0))
    return pl.pallas_call(
        _ssd_kernel,
        grid=(bsz, nc),
        in_specs=[tile(xbc.shape[2]), tile(sz.shape[2]), tile(dt.shape[2]),
                  full(alog.shape), full(dskip.shape), full(ng.shape), full(wout.shape), full(e2.shape),
                  tile(d), per_b, full(g.shape), per_b, per_b, full(rwt.shape), full(rb.shape)],
        out_specs=[tile(d),
                   pl.BlockSpec((L, d + W_LANES), lambda b, c: (b * nc + c, 0)),
                   pl.BlockSpec((1, 1, L), lambda b, c: (b * nc + c, 0, 0)),
                   pl.BlockSpec((CLASS_ROWS, 128), lambda b, c: (0, 0))],
        out_shape=[jax.ShapeDtypeStruct((bsz, s, d), F32),
                   jax.ShapeDtypeStruct((bsz * s, d + W_LANES), F32),
                   jax.ShapeDtypeStruct((bsz * nc, 1, L), I32),
                   jax.ShapeDtypeStruct((CLASS_ROWS, 128), F32)],
        scratch_shapes=[pltpu.VMEM((SSM_GROUPS, SSM_STATE, gw), F32),
                        pltpu.VMEM((L, d), F32),
                        pltpu.VMEM((L, gw), F32)],
        compiler_params=_params(2), name="ssd_out_proj",
    )(xbc, sz, dt, alog, dskip, ng, wout, e2, x, g1, g, sc, sh, rwt, rb)


def _dest_kernel(cls_ref, start_ref, dest_ref, run_ref):
    @pl.when(pl.program_id(0) == 0)
    def _():
        run_ref[...] = start_ref[...]

    tr = cls_ref.shape[2]
    upper = lax.broadcasted_iota(I32, (tr, tr), 0) <= lax.broadcasted_iota(I32, (tr, tr), 1)
    upper = jnp.where(upper, 1.0, 0.0).astype(BF16)
    for k in range(cls_ref.shape[0]):
        onehot = lax.broadcasted_iota(I32, (CLASS_ROWS, tr), 0) == cls_ref[k]
        oh = jnp.where(onehot, 1.0, 0.0)
        prefix = _dot(oh.astype(BF16), upper)
        dest = jnp.sum(oh * (prefix - 1.0 + run_ref[...]), axis=0, keepdims=True)
        dest_ref[k] = dest.astype(I32)
        run_ref[...] += jnp.sum(oh, axis=1, keepdims=True)


def _moe_plan(cls, cnt, tr):
    nt = cls.shape[0]
    t = nt * tr
    reps = 4 if nt % 4 == 0 else 1
    padded = jnp.ceil(cnt[:, 0] / MOE_TILE) * MOE_TILE
    end = jnp.cumsum(padded)
    start = end - padded
    dest = pl.pallas_call(
        _dest_kernel, grid=(nt // reps,),
        in_specs=[pl.BlockSpec((reps, 1, tr), lambda i: (i, 0, 0)),
                  pl.BlockSpec((CLASS_ROWS, 1), lambda i: (0, 0))],
        out_specs=pl.BlockSpec((reps, 1, tr), lambda i: (i, 0, 0)),
        out_shape=jax.ShapeDtypeStruct((nt, 1, tr), I32),
        scratch_shapes=[pltpu.VMEM((CLASS_ROWS, 1), F32)],
        compiler_params=_params(1), name="moe_dest",
    )(cls, start.reshape(CLASS_ROWS, 1))
    n_tiles = t // MOE_TILE + N_CLASSES
    tile_row = jnp.arange(n_tiles, dtype=F32) * MOE_TILE
    total = end[N_CLASSES - 1]
    valid = tile_row < total
    tcls = jnp.sum((tile_row[:, None] >= end[None, :N_CLASSES]).astype(I32), axis=1)
    last = jnp.sum((total - MOE_TILE >= end[:N_CLASSES]).astype(I32))
    tcls = jnp.where(valid, tcls, last)
    grp = tcls // N_PAIRS
    pair = tcls % N_PAIRS
    lo = jnp.where(pair < 3, 0, jnp.where(pair < 5, 1, 2))
    hi = jnp.where(pair < 3, pair + 1, jnp.where(pair < 5, pair - 1, 3))
    meta = jnp.stack([grp * EXPERTS_PER_GROUP + lo, grp * EXPERTS_PER_GROUP + hi, valid.astype(I32),
                      jnp.zeros_like(tcls)]).astype(I32)
    return dest.reshape(t), meta


def _invert_kernel(dest_ref, zeros_ref, src_ref, sem):
    clear = pltpu.make_async_copy(zeros_ref, src_ref, sem)
    clear.start()
    clear.wait()

    def put(t0, carry):
        for u in range(ROW_DMA_UNROLL):
            t = t0 * ROW_DMA_UNROLL + u
            src_ref[dest_ref[t]] = t
        return carry

    lax.fori_loop(0, dest_ref.shape[0] // ROW_DMA_UNROLL, put, 0)


def _invert(dest, n_slots):
    smem = pl.BlockSpec(memory_space=pltpu.SMEM)
    return pl.pallas_call(
        _invert_kernel, in_specs=[smem, pl.BlockSpec(memory_space=pl.ANY)], out_specs=smem,
        out_shape=jax.ShapeDtypeStruct((n_slots,), I32),
        scratch_shapes=[pltpu.SemaphoreType.DMA(())], name="moe_invert",
    )(dest, jnp.zeros((n_slots,), I32))


def _expert_kernel(meta_ref, src_ref, h2x_ref, wgu_a_ref, wgu_b_ref, wd_a_ref, wd_b_ref, y_ref,
                   buf0_ref, buf1_ref, sem):
    i = pl.program_id(0)
    d = y_ref.shape[1]
    bufs = (buf0_ref, buf1_ref)

    def gather(tile, slot, rows=range(MOE_TILE), after=0):
        for r in rows:
            tok = src_ref[tile * MOE_TILE + r] + after
            pltpu.make_async_copy(h2x_ref.at[pl.ds(tok, 1), :], bufs[slot].at[pl.ds(r, 1), :],
                                  sem.at[slot]).start(priority=r % 2)

    def wait(slot):
        pltpu.make_async_copy(h2x_ref.at[pl.ds(0, MOE_TILE), :], bufs[slot], sem.at[slot]).wait()

    @pl.when(i == 0)
    def _():
        gather(0, 0)

    valid = meta_ref[2, i] != 0
    cw = EXPERT_COLS
    for cur in range(2):
        @pl.when(valid & (i % 2 == cur))
        def _(cur=cur):
            wait(cur)
            xb = bufs[cur][:, :d].astype(BF16)
            wts = bufs[cur][:, d:]
            n_parts = (wgu_a_ref.shape[2] + d) // cw
            per_part = MOE_TILE // n_parts
            gather(i + 1, 1 - cur, range(per_part))
            part = [1]

            def pace(result):
                if part[0] == n_parts:
                    return
                zero = result[0, 0].astype(I32) * meta_ref[3, i]
                gather(i + 1, 1 - cur, range(part[0] * per_part, (part[0] + 1) * per_part), zero)
                part[0] += 1

            def dot_cols(a, w_ref):
                chunks = []
                for c in range(w_ref.shape[2] // cw):
                    chunks.append(_dot(a, w_ref[0, :, c * cw:(c + 1) * cw]))
                    pace(chunks[-1])
                return jnp.concatenate(chunks, axis=1)

            y = jnp.zeros(y_ref.shape, F32)
            for e, (wgu_ref, wd_ref) in enumerate(((wgu_a_ref, wd_a_ref), (wgu_b_ref, wd_b_ref))):
                gu = dot_cols(xb, wgu_ref)
                ff = gu.shape[1] // 2
                act = _silu(gu[:, :ff]) * gu[:, ff:]
                y = y + wts[:, e:e + 1] * dot_cols(act.astype(BF16), wd_ref)
            y_ref[...] = y

        @pl.when(jnp.logical_not(valid) & (meta_ref[2, jnp.maximum(i - 1, 0)] != 0) & (i % 2 == cur))
        def _(cur=cur):
            wait(cur)

    @pl.when(jnp.logical_not(valid))
    def _():
        y_ref[...] = jnp.zeros(y_ref.shape, F32)


def _experts(meta, src, h2x, wgu, wd):
    n_slots = src.shape[0]
    w = h2x.shape[1]
    d = w - W_LANES
    n_tiles = n_slots // MOE_TILE
    return pl.pallas_call(
        _expert_kernel,
        grid_spec=pltpu.PrefetchScalarGridSpec(
            num_scalar_prefetch=2, grid=(n_tiles,),
            in_specs=[pl.BlockSpec(memory_space=pl.ANY),
                      pl.BlockSpec((1,) + wgu.shape[1:], lambda i, m, s: (m[0, i], 0, 0)),
                      pl.BlockSpec((1,) + wgu.shape[1:], lambda i, m, s: (m[1, i], 0, 0)),
                      pl.BlockSpec((1,) + wd.shape[1:], lambda i, m, s: (m[0, i], 0, 0)),
                      pl.BlockSpec((1,) + wd.shape[1:], lambda i, m, s: (m[1, i], 0, 0))],
            out_specs=pl.BlockSpec((MOE_TILE, d), lambda i, m, s: (i, 0)),
            scratch_shapes=[pltpu.VMEM((MOE_TILE, w), F32), pltpu.VMEM((MOE_TILE, w), F32),
                            pltpu.SemaphoreType.DMA((2,))]),
        out_shape=jax.ShapeDtypeStruct((n_slots, d), F32),
        compiler_params=_params(1), name="moe_experts",
    )(meta, src, h2x, wgu, wgu, wd, wd)


def _combine_kernel(dest_ref, ys_ref, x_ref, g2_ref, o_ref, buf0_ref, buf1_ref, sem):
    tm = x_ref.shape[1]
    step = pl.program_id(0) * pl.num_programs(1) + pl.program_id(1)
    n_steps = pl.num_programs(0) * pl.num_programs(1)
    bufs = (buf0_ref, buf1_ref)

    def gather(st, slot):
        for r in range(tm):
            pltpu.make_async_copy(ys_ref.at[pl.ds(dest_ref[st * tm + r], 1), :], bufs[slot].at[pl.ds(r, 1), :],
                                  sem.at[slot]).start(priority=r % 2)

    @pl.when(step == 0)
    def _():
        gather(0, 0)

    for cur in range(2):
        @pl.when(step % 2 == cur)
        def _(cur=cur):
            @pl.when(step + 1 < n_steps)
            def _():
                gather(step + 1, 1 - cur)

            pltpu.make_async_copy(ys_ref.at[pl.ds(0, tm), :], bufs[cur], sem.at[cur]).wait()
            o_ref[0] = x_ref[0] + g2_ref[0] * bufs[cur][...]


def _combine(dest, ys, x, g2, tm):
    bsz, s, d = x.shape
    return pl.pallas_call(
        _combine_kernel,
        grid_spec=pltpu.PrefetchScalarGridSpec(
            num_scalar_prefetch=1, grid=(bsz, s // tm),
            in_specs=[pl.BlockSpec(memory_space=pl.ANY),
                      pl.BlockSpec((1, tm, d), lambda b, i, dd: (b, i, 0)),
                      pl.BlockSpec((1, 1, d), lambda b, i, dd: (b, 0, 0))],
            out_specs=pl.BlockSpec((1, tm, d), lambda b, i, dd: (b, i, 0)),
            scratch_shapes=[pltpu.VMEM((tm, d), F32), pltpu.VMEM((tm, d), F32), pltpu.SemaphoreType.DMA((2,))]),
        out_shape=jax.ShapeDtypeStruct((bsz, s, d), F32),
        compiler_params=_params(2), name="moe_combine",
    )(dest, ys, x, g2)


def _moe(h2x, cls, cnt, wgu, wd):
    t = h2x.shape[0]
    tr = min(512, t)
    dest, meta = _moe_plan(cls.reshape(t // tr, 1, tr), cnt, tr)
    n_slots = t + N_CLASSES * MOE_TILE
    return dest, _experts(meta, _invert(dest, n_slots), h2x, wgu, wd)


def kernel(x, c, ada_w, ada_b, norm_mix, norm_ffn, hy_w_in, hy_q_norm, hy_k_norm, hy_dw_w, hy_dw_b, hy_ln_g, hy_ln_b, hy_w_out, ssm_w_in, ssm_conv_w, ssm_conv_b, ssm_dt_bias, ssm_a_log, ssm_d, ssm_norm, ssm_w_out, router_w, router_bias, exp_w_gate, exp_w_up, exp_w_down):
    bsz, s, d = x.shape
    depth = ada_w.shape[0]
    assert s % 512 == 0 and bsz <= 8
    c8 = jnp.zeros((8, d), F32).at[:bsz].set(c)
    mod = _ada(c8, ada_w, ada_b)
    rwt = router_w.T
    rb = router_bias.reshape(N_EXPERTS, 1)
    lane_head = jnp.arange(ATT_WIDTH) // HEAD_DIM
    pmat = ((lane_head[:, None] == lane_head[None, :]).astype(F32) / HEAD_DIM).astype(BF16)

    moe = None
    for layer in range(depth):
        sh1, sc1, g1, sh2, sc2, g2 = (mod[layer, :bsz, i * d:(i + 1) * d].reshape(bsz, 1, d) for i in range(N_MOD))
        gm = norm_mix[layer].reshape(1, d)
        gf = norm_ffn[layer].reshape(1, d)
        j = layer // 2
        if layer % 2 == 0:
            if moe is not None:
                x = _combine(*moe, min(256, s))
            w_in = hy_w_in[j].astype(BF16)
            aw = ATT_WIDTH
            k, qt, vt, u, kmean = _hy_in(
                x, sc1, sh1, gm, w_in[:, aw:2 * aw], w_in[:, 3 * aw:], w_in[:, :aw].T, w_in[:, 2 * aw:3 * aw].T,
                (jnp.tile(hy_q_norm[j], ATT_HEADS) * (HEAD_DIM ** -0.5 * LOG2E)).reshape(aw, 1),
                jnp.tile(hy_k_norm[j], ATT_HEADS).reshape(1, aw), pmat, 512)
            bias = _gate(kmean.reshape(bsz, s // MOBA_BLOCK, aw), qt, min(1024, s))
            att_t = _attn(qt, k, vt, bias, 8)
            w_out = hy_w_out[j].astype(BF16)
            dww = jnp.zeros((CONV_HALO, hy_dw_w.shape[2]), F32).at[:CONV_WIDTH].set(hy_dw_w[j])
            x_new, h2x, cls, cnt = _hy_out(
                u, att_t, x, g1, w_out[:aw], w_out[aw:], dww, hy_dw_b[j].reshape(1, -1),
                hy_ln_g[j].reshape(1, -1), hy_ln_b[j].reshape(1, -1), gf, sc2, sh2, rwt, rb, 256)
        else:
            w_in = ssm_w_in[j]
            inner = ssm_norm.shape[1]
            heads = ssm_a_log.shape[1]
            hpg = heads // SSM_GROUPS
            conv_dim = ssm_conv_w.shape[2]

            def by_group(v):
                v = v.reshape(v.shape[:-1] + (SSM_GROUPS, hpg))
                pad = [(0, 0)] * (v.ndim - 1) + [(0, 128 - hpg)]
                return jnp.pad(v, pad).reshape(v.shape[:-2] + (SSM_GROUPS * 128,))

            x, sz, xbc, dt = _ssm_in(
                *moe, sc1, sh1, gm, w_in[:, :inner].astype(BF16), w_in[:, inner:inner + conv_dim].astype(BF16),
                by_group(w_in[:, inner + conv_dim:]).astype(BF16), ssm_conv_w[j], ssm_conv_b[j].reshape(1, -1),
                by_group(ssm_dt_bias[j]).reshape(1, -1), 256)
            chan_head = jnp.arange(inner // SSM_GROUPS) // SSM_HEAD_DIM
            e2 = (jnp.arange(256)[:, None] % 128 == chan_head[None, :]).astype(BF16)
            x_new, h2x, cls, cnt = _ssd(
                xbc, sz, dt, by_group(ssm_a_log[j]).reshape(SSM_GROUPS, 1, 128),
                jnp.repeat(ssm_d[j], SSM_HEAD_DIM).reshape(SSM_GROUPS, 1, -1),
                ssm_norm[j].reshape(SSM_GROUPS, 1, -1), ssm_w_out[j].astype(BF16), e2, x, g1, gf, sc2, sh2, rwt, rb)
        wgu = jnp.concatenate([exp_w_gate[layer], exp_w_up[layer]], axis=-1).astype(BF16)
        moe = _moe(h2x, cls, cnt, wgu, exp_w_down[layer].astype(BF16)) + (x_new, g2)
    return _combine(*moe, min(256, s))
```

```python
import functools

import jax
import jax.numpy as jnp
from jax import lax
from jax.experimental import pallas as pl
from jax.experimental.pallas import tpu as pltpu

F32 = jnp.float32
BF16 = jnp.bfloat16
I32 = jnp.int32

NORM_EPS = 1e-6
N_MOD = 6
HEAD_DIM = 64
ATT_HEADS = 8
ATT_WIDTH = ATT_HEADS * HEAD_DIM
MOBA_BLOCK = 256
MOBA_TOPK = 3
CONV_WIDTH = 31
CONV_HALO = 32
SSM_HEAD_DIM = 64
SSM_GROUPS = 4
SSM_STATE = 128
SSM_CONV = 4
SSM_CHUNK = 256
SSM_HALO = 8
SSD_SUB = 128
SSM_IN_COLS = 512
N_EXPERTS = 16
N_EXPERT_GROUPS = 4
EXPERTS_PER_GROUP = 4
N_PAIRS = 6
N_CLASSES = N_EXPERT_GROUPS * N_PAIRS
CLASS_ROWS = 32
MOE_TILE = 256
EXPERT_COLS = 256
W_LANES = 128
INVERT_UNROLL = 32
LOG2E = 1.4426950408889634
NEG = -1e30
VMEM_LIMIT = 56 * 1024 * 1024


def _params(n_axes):
    return pltpu.CompilerParams(dimension_semantics=("arbitrary",) * n_axes,
                                vmem_limit_bytes=VMEM_LIMIT)


def _dot(a, b):
    return jnp.dot(a, b, preferred_element_type=F32)


def _dot_nt(a, b):
    return lax.dot_general(a, b, (((1,), (1,)), ((), ())), preferred_element_type=F32)


def _split2(x):
    hi = x.astype(BF16)
    lo = (x - hi.astype(F32)).astype(BF16)
    return hi, lo


def _split3(x):
    a = x.astype(BF16)
    r = x - a.astype(F32)
    b = r.astype(BF16)
    c = (r - b.astype(F32)).astype(BF16)
    return a, b, c


def _silu(x):
    return x * jax.nn.sigmoid(x)


def _mod_norm(x, g, sc, sh):
    ms = jnp.mean(x * x, axis=-1, keepdims=True)
    return x * lax.rsqrt(ms + NORM_EPS) * g * (1.0 + sc) + sh


def _ada_kernel(c_ref, w_ref, b_ref, o_ref):
    cond = _silu(c_ref[...])
    ch, cl = _split2(cond)
    wh, wl = _split2(w_ref[0])
    o_ref[0] = _dot(ch, wh) + _dot(ch, wl) + _dot(cl, wh) + b_ref[0]


def _ada(c8, ada_w, ada_b):
    depth, d, n = ada_w.shape
    tn = 1536
    return pl.pallas_call(
        _ada_kernel,
        grid=(depth, n // tn),
        in_specs=[pl.BlockSpec((8, d), lambda l, j: (0, 0)),
                  pl.BlockSpec((1, d, tn), lambda l, j: (l, 0, j)),
                  pl.BlockSpec((1, 1, tn), lambda l, j: (l, 0, j))],
        out_specs=pl.BlockSpec((1, 8, tn), lambda l, j: (l, 0, j)),
        out_shape=jax.ShapeDtypeStruct((depth, 8, n), F32),
        compiler_params=_params(2), name="ada_mod",
    )(c8, ada_w, ada_b.reshape(depth, 1, n))


def _hy_in_kernel(x_ref, sc_ref, sh_ref, g_ref, wk_ref, wu_ref, wqt_ref, wvt_ref, qg_ref, kg_ref, p_ref,
                  k_ref, qt_ref, vt_ref, u_ref, km_ref):
    tm = x_ref.shape[1]
    h = _mod_norm(x_ref[0], g_ref[...], sc_ref[0], sh_ref[0])
    hb = h.astype(BF16)
    k = _dot(hb, wk_ref[...])
    khi, klo = _split2(k * k)
    ms = _dot(khi, p_ref[...]) + _dot(klo, p_ref[...])
    kn = k * lax.rsqrt(ms + NORM_EPS) * kg_ref[...]
    k_ref[0] = kn.astype(BF16)
    km_ref[0, 0] = jnp.mean(kn.reshape(tm // MOBA_BLOCK, MOBA_BLOCK, ATT_WIDTH), axis=1)
    qt = _dot_nt(wqt_ref[...], hb).reshape(ATT_HEADS, HEAD_DIM, tm)
    qms = jnp.mean(qt * qt, axis=1, keepdims=True)
    qn = (qt * lax.rsqrt(qms + NORM_EPS)).reshape(ATT_WIDTH, tm) * qg_ref[...]
    qt_ref[0] = qn.astype(BF16)
    vt_ref[0] = _dot_nt(wvt_ref[...], hb).astype(BF16)
    ag = _dot(hb, wu_ref[...])
    half = ag.shape[1] // 2
    u_ref[0] = (ag[:, :half] * jax.nn.sigmoid(ag[:, half:])).astype(BF16)


def _hy_in(x, sc, sh, g, wk, wu, wqt, wvt, qg, kg, pmat, tm):
    bsz, s, d = x.shape
    cc = wu.shape[1] // 2
    full = lambda shape: pl.BlockSpec(shape, lambda b, i: (0,) * len(shape))
    return pl.pallas_call(
        _hy_in_kernel,
        grid=(bsz, s // tm),
        in_specs=[pl.BlockSpec((1, tm, d), lambda b, i: (b, i, 0)),
                  pl.BlockSpec((1, 1, d), lambda b, i: (b, 0, 0)),
                  pl.BlockSpec((1, 1, d), lambda b, i: (b, 0, 0)),
                  full((1, d)), full(wk.shape), full(wu.shape), full(wqt.shape), full(wvt.shape),
                  full(qg.shape), full(kg.shape), full(pmat.shape)],
        out_specs=[pl.BlockSpec((1, tm, ATT_WIDTH), lambda b, i: (b, i, 0)),
                   pl.BlockSpec((1, ATT_WIDTH, tm), lambda b, i: (b, 0, i)),
                   pl.BlockSpec((1, ATT_WIDTH, tm), lambda b, i: (b, 0, i)),
                   pl.BlockSpec((1, tm, cc), lambda b, i: (b, i, 0)),
                   pl.BlockSpec((1, 1, tm // MOBA_BLOCK, ATT_WIDTH), lambda b, i: (b, i, 0, 0))],
        out_shape=[jax.ShapeDtypeStruct((bsz, s, ATT_WIDTH), BF16),
                   jax.ShapeDtypeStruct((bsz, ATT_WIDTH, s), BF16),
                   jax.ShapeDtypeStruct((bsz, ATT_WIDTH, s), BF16),
                   jax.ShapeDtypeStruct((bsz, s, cc), BF16),
                   jax.ShapeDtypeStruct((bsz, s // tm, tm // MOBA_BLOCK, ATT_WIDTH), F32)],
        compiler_params=_params(2), name="hy_in_proj",
    )(x, sc, sh, g, wk, wu, wqt, wvt, qg, kg, pmat)


def _gate_kernel(km_ref, qt_ref, o_ref):
    nb = km_ref.shape[1]
    ts = qt_ref.shape[2]
    rows = ATT_HEADS * nb
    km = km_ref[0]
    kmt = jnp.broadcast_to(km[None], (ATT_HEADS, nb, ATT_WIDTH)).reshape(rows, ATT_WIDTH)
    rh = lax.broadcasted_iota(I32, (rows, ATT_WIDTH), 0) // nb
    ch = lax.broadcasted_iota(I32, (rows, ATT_WIDTH), 1) // HEAD_DIM
    kbd = jnp.where(rh == ch, kmt, 0.0)
    khi, klo = _split2(kbd)
    qt = qt_ref[0]
    gate = (_dot(khi, qt) + _dot(klo, qt)).reshape(ATT_HEADS, nb, ts)
    blk = lax.broadcasted_iota(I32, (ATT_HEADS, nb, ts), 1)
    qblk = (pl.program_id(1) * ts + lax.broadcasted_iota(I32, (ATT_HEADS, nb, ts), 2)) // MOBA_BLOCK
    g = jnp.where(blk < qblk, gate, -jnp.inf)
    sel = jnp.zeros(g.shape, jnp.bool_)
    for _ in range(MOBA_TOPK):
        m = jnp.max(g, axis=1, keepdims=True)
        first = jnp.min(jnp.where((g == m) & (m > -jnp.inf), blk, nb), axis=1, keepdims=True)
        pick = blk == first
        sel = sel | pick
        g = jnp.where(pick, -jnp.inf, g)
    o_ref[0] = jnp.where(sel, 0.0, NEG).reshape(rows, ts)


def _gate(kmean, qt, ts):
    bsz, nb, _ = kmean.shape
    s = qt.shape[2]
    rows = ATT_HEADS * nb
    return pl.pallas_call(
        _gate_kernel,
        grid=(bsz, s // ts),
        in_specs=[pl.BlockSpec((1, nb, ATT_WIDTH), lambda b, i: (b, 0, 0)),
                  pl.BlockSpec((1, ATT_WIDTH, ts), lambda b, i: (b, 0, i))],
        out_specs=pl.BlockSpec((1, rows, ts), lambda b, i: (b, 0, i)),
        out_shape=jax.ShapeDtypeStruct((bsz, rows, s), F32),
        compiler_params=_params(2), name="moba_gate",
    )(kmean, qt)


def _attn_kernel(qt_ref, k_ref, vt_ref, bias_ref, o_ref, qa_ref, acc_ref, m_ref, *, nb, hp, lag, kb):
    blk = MOBA_BLOCK
    pw = 2 * HEAD_DIM
    va = HEAD_DIM + 16
    i = pl.program_id(2)
    row = lax.broadcasted_iota(I32, (pw, blk), 0)
    for h in range(hp):
        qp = qt_ref[0, (h // 2) * pw:(h // 2 + 1) * pw, :]
        keep = (row < HEAD_DIM) if h % 2 == 0 else (row >= HEAD_DIM)
        qa_ref[h, :pw, :] = jnp.where(keep, qp, jnp.zeros_like(qp))
        qa_ref[h, pw:pw + nb, :] = bias_ref[0, h * nb:(h + 1) * nb, :].astype(BF16)
        qa_ref[h, pw + nb:, :] = jnp.zeros((blk - pw - nb, blk), BF16)
        m_ref[h] = jnp.full((1, blk), NEG, F32)
        acc_ref[h] = jnp.zeros((va, blk), F32)
    causal = lax.broadcasted_iota(I32, (blk, blk), 0) <= lax.broadcasted_iota(I32, (blk, blk), 1)
    lane = lax.broadcasted_iota(I32, (blk, pw), 1)
    ones = jnp.ones((va - HEAD_DIM, blk), BF16)

    def step(blocks):
        units = [(b, h) for b in range(len(blocks)) for h in range(hp)]
        offs = [pl.multiple_of(j * blk, blk) for j, _, _ in blocks]

        def scores(u):
            b, h = units[u]
            pp = h // 2
            ka = jnp.concatenate([k_ref[0, pl.ds(offs[b], blk), pp * pw:(pp + 1) * pw], blocks[b][1]], axis=1)
            return blocks[b][2](_dot(ka, qa_ref[h]))

        def softmax(u, s):
            h = units[u][1]
            m_old = m_ref[h]
            m_new = jnp.maximum(m_old, jnp.max(s, axis=0, keepdims=True))
            m_ref[h] = m_new
            return jnp.exp2(s - m_new).astype(BF16), jnp.exp2(m_old - m_new)

        def accumulate(u, p, alpha):
            b, h = units[u]
            vj = jnp.concatenate([vt_ref[0, h * HEAD_DIM:(h + 1) * HEAD_DIM, pl.ds(offs[b], blk)], ones], axis=0)
            acc_ref[h] = alpha * acc_ref[h] + _dot(vj, p)

        s, pa = {}, {}
        for t in range(len(units) + lag):
            if t < len(units):
                s[t] = scores(t)
            if 1 <= t <= len(units):
                pa[t - 1] = softmax(t - 1, s.pop(t - 1))
            if t >= lag:
                accumulate(t - lag, *pa.pop(t - lag))

    def past(j):
        return (j, jnp.where(lane == j, 1.0, 0.0).astype(BF16), lambda s: s)

    own = (i, jnp.zeros((blk, pw), BF16), lambda s: jnp.where(causal, s, NEG))

    @pl.when(i == 0)
    def _():
        step([own])

    @pl.when(i > 0)
    def _():
        step([own, past(0)])

    n_rest = jnp.maximum(i - 1, 0)
    rem = n_rest % kb
    base = jnp.int32(1)
    size = 1
    while size < kb:
        take = (rem & size) != 0

        @pl.when(take)
        def _(base=base, size=size):
            step([past(base + t) for t in range(size)])

        base = base + jnp.where(take, size, 0)
        size *= 2

    def body(r, carry):
        step([past(1 + rem + kb * r + t) for t in range(kb)])
        return carry

    lax.fori_loop(0, n_rest // kb, body, 0)
    for h in range(hp):
        acc = acc_ref[h]
        o_ref[0, h * HEAD_DIM:(h + 1) * HEAD_DIM, :] = (acc[:HEAD_DIM] / acc[HEAD_DIM:HEAD_DIM + 1]).astype(BF16)


def _attn(qt, k, vt, bias, hp):
    bsz, s, _ = k.shape
    nb = s // MOBA_BLOCK
    assert nb <= MOBA_BLOCK - 2 * HEAD_DIM and nb % 16 == 0
    hw = hp * HEAD_DIM
    return pl.pallas_call(
        functools.partial(_attn_kernel, nb=nb, hp=hp, lag=6, kb=4),
        grid=(bsz, ATT_HEADS // hp, nb),
        in_specs=[pl.BlockSpec((1, hw, MOBA_BLOCK), lambda b, p, i: (b, p, i)),
                  pl.BlockSpec((1, s, hw), lambda b, p, i: (b, 0, p)),
                  pl.BlockSpec((1, hw, s), lambda b, p, i: (b, p, 0)),
                  pl.BlockSpec((1, hp * nb, MOBA_BLOCK), lambda b, p, i: (b, p, i))],
        out_specs=pl.BlockSpec((1, hw, MOBA_BLOCK), lambda b, p, i: (b, p, i)),
        out_shape=jax.ShapeDtypeStruct((bsz, ATT_WIDTH, s), BF16),
        scratch_shapes=[pltpu.VMEM((hp, MOBA_BLOCK, MOBA_BLOCK), BF16),
                        pltpu.VMEM((hp, HEAD_DIM + 16, MOBA_BLOCK), F32),
                        pltpu.VMEM((hp, 1, MOBA_BLOCK), F32)],
        compiler_params=_params(3), name="moba_attn",
    )(qt, k, vt, bias)


def _route(logit_t, rb):
    aff = jax.nn.sigmoid(logit_t)
    score = aff + rb
    s = [score[e:e + 1, :] for e in range(N_EXPERTS)]
    a = [aff[e:e + 1, :] for e in range(N_EXPERTS)]
    n = EXPERTS_PER_GROUP

    def top2_sum(v):
        best = None
        for x in range(n):
            for y in range(x + 1, n):
                t = v[x] + v[y]
                best = t if best is None else jnp.maximum(best, t)
        return best

    def first_argmax(v):
        idx = jnp.zeros(v[0].shape, I32)
        cur = v[0]
        for x in range(1, len(v)):
            better = v[x] > cur
            idx = jnp.where(better, x, idx)
            cur = jnp.where(better, v[x], cur)
        return idx

    grp = first_argmax([top2_sum(s[n * g:n * g + n]) for g in range(N_EXPERT_GROUPS)])

    def in_group(v, x):
        out = v[(N_EXPERT_GROUPS - 1) * n + x]
        for g in range(N_EXPERT_GROUPS - 2, -1, -1):
            out = jnp.where(grp == g, v[n * g + x], out)
        return out

    sg = [in_group(s, x) for x in range(n)]
    ag = [in_group(a, x) for x in range(n)]
    l1 = first_argmax(sg)
    l2 = first_argmax([jnp.where(l1 == x, -jnp.inf, sg[x]) for x in range(n)])
    lo = jnp.minimum(l1, l2)
    hi = jnp.maximum(l1, l2)
    pair = jnp.where(lo == 0, hi - 1, jnp.where(lo == 1, hi + 1, 5))
    a_lo = jnp.zeros_like(ag[0])
    a_hi = jnp.zeros_like(ag[0])
    for x in range(n):
        a_lo = jnp.where(lo == x, ag[x], a_lo)
        a_hi = jnp.where(hi == x, ag[x], a_hi)
    tot = a_lo + a_hi
    return grp * N_PAIRS + pair, a_lo / tot, a_hi / tot


def _post_mixer(x_new, g_ref, sc_ref, sh_ref, rwt_ref, rb_ref, h2x_ref, cls_ref, cnt_ref):
    tm, d = x_new.shape
    h2 = _mod_norm(x_new, g_ref[...], sc_ref[0], sh_ref[0])
    hh, hl = _split2(h2)
    rh, rl = _split2(rwt_ref[...])
    logit_t = _dot_nt(rh, hh) + _dot_nt(rh, hl) + _dot_nt(rl, hh)
    cls, w_lo, w_hi = _route(logit_t, rb_ref[...])
    cls_ref[0] = cls

    @pl.when((pl.program_id(0) == 0) & (pl.program_id(1) == 0))
    def _():
        cnt_ref[...] = jnp.zeros(cnt_ref.shape, F32)

    onehot = lax.broadcasted_iota(I32, (CLASS_ROWS, tm), 0) == cls
    cnt_ref[...] += jnp.sum(jnp.where(onehot, 1.0, 0.0), axis=1, keepdims=True)
    wrow = lax.broadcasted_iota(I32, (W_LANES, tm), 0)
    wt = jnp.where(wrow == 0, w_lo, jnp.where(wrow == 1, w_hi, 0.0))
    h2x_ref[:, :d] = h2
    h2x_ref[:, d:] = wt.T


def _hy_out_kernel(u_ref, up_ref, at_ref, x_ref, g1_ref, wtop_ref, wbot_ref, dww_ref, dwb_ref, lng_ref, lnb_ref,
                   g_ref, sc_ref, sh_ref, rwt_ref, rb_ref,
                   xn_ref, h2x_ref, cls_ref, cnt_ref, cat_ref, shifted_ref):
    tm = u_ref.shape[1]
    m_att = _dot(at_ref[0].astype(F32).T.astype(BF16), wtop_ref[...])
    prev = up_ref[0].astype(F32)
    cat_ref[:CONV_HALO, :] = jnp.where(pl.program_id(1) == 0, 0.0, prev)
    cat_ref[CONV_HALO:, :] = u_ref[0].astype(F32)
    span = tm + CONV_HALO - 8
    for r in range(1, 8):
        shifted_ref[r - 1] = cat_ref[pl.ds(r, span), :]
    y = jnp.zeros((tm, u_ref.shape[2]), F32) + dwb_ref[...]
    for j in range(CONV_WIDTH):
        a, r = divmod(CONV_HALO - CONV_WIDTH + 1 + j, 8)
        tap = cat_ref[pl.ds(8 * a, tm), :] if r == 0 else shifted_ref[r - 1, pl.ds(8 * a, tm), :]
        y = y + dww_ref[j:j + 1, :] * tap
    mu = jnp.mean(y, axis=-1, keepdims=True)
    var = jnp.mean(jnp.square(y - mu), axis=-1, keepdims=True)
    cv = _silu((y - mu) * lax.rsqrt(var + NORM_EPS) * lng_ref[...] + lnb_ref[...])
    m = m_att + _dot(cv.astype(BF16), wbot_ref[...])
    x_new = x_ref[0] + g1_ref[0] * m
    xn_ref[0] = x_new
    _post_mixer(x_new, g_ref, sc_ref, sh_ref, rwt_ref, rb_ref, h2x_ref, cls_ref, cnt_ref)


def _hy_out(u, att_t, x, g1, wtop, wbot, dww, dwb, lng, lnb, g, sc, sh, rwt, rb, tm):
    bsz, s, d = x.shape
    cc = u.shape[2]
    nt = s // tm
    full = lambda shape: pl.BlockSpec(shape, lambda b, i: (0,) * len(shape))
    per_b = pl.BlockSpec((1, 1, d), lambda b, i: (b, 0, 0))
    halo = tm // CONV_HALO
    return pl.pallas_call(
        _hy_out_kernel,
        grid=(bsz, nt),
        in_specs=[pl.BlockSpec((1, tm, cc), lambda b, i: (b, i, 0)),
                  pl.BlockSpec((1, CONV_HALO, cc), lambda b, i: (b, jnp.maximum(i * halo - 1, 0), 0)),
                  pl.BlockSpec((1, ATT_WIDTH, tm), lambda b, i: (b, 0, i)),
                  pl.BlockSpec((1, tm, d), lambda b, i: (b, i, 0)),
                  per_b, full(wtop.shape), full(wbot.shape), full(dww.shape), full(dwb.shape),
                  full(lng.shape), full(lnb.shape), full(g.shape), per_b, per_b, full(rwt.shape), full(rb.shape)],
        out_specs=[pl.BlockSpec((1, tm, d), lambda b, i: (b, i, 0)),
                   pl.BlockSpec((tm, d + W_LANES), lambda b, i: (b * nt + i, 0)),
                   pl.BlockSpec((1, 1, tm), lambda b, i: (b * nt + i, 0, 0)),
                   pl.BlockSpec((CLASS_ROWS, 128), lambda b, i: (0, 0))],
        out_shape=[jax.ShapeDtypeStruct((bsz, s, d), F32),
                   jax.ShapeDtypeStruct((bsz * s, d + W_LANES), F32),
                   jax.ShapeDtypeStruct((bsz * nt, 1, tm), I32),
                   jax.ShapeDtypeStruct((CLASS_ROWS, 128), F32)],
        scratch_shapes=[pltpu.VMEM((tm + CONV_HALO, cc), F32),
                        pltpu.VMEM((7, tm + CONV_HALO - 8, cc), F32)],
        compiler_params=_params(2), name="hy_out_proj",
    )(u, u, att_t, x, g1, wtop, wbot, dww, dwb, lng, lnb, g, sc, sh, rwt, rb)


def _ssm_in_kernel(dest_ref, opaque_ref, ys_ref, xn_ref, g2_ref, sc_ref, sh_ref, g_ref, wz_ref, wx_ref, wdt_ref,
                   cw_ref, cb_ref, dtb_ref, x_ref, sz_ref, xbc_ref, dt_ref, buf_ref, sem, *cat_refs):
    tm = xn_ref.shape[1]
    ch = SSM_IN_COLS
    step = pl.program_id(0) * pl.num_programs(1) + pl.program_id(1)
    last = pl.num_programs(0) * pl.num_programs(1) - 1
    cur = step % 2
    ahead = jnp.minimum(step + 1, last)

    def gather(tile, slot, rows, after=0):
        for r in rows:
            row = dest_ref[tile * tm + r] + after
            pltpu.make_async_copy(ys_ref.at[pl.ds(row, 1), :], buf_ref.at[slot, pl.ds(r, 1), :],
                                  sem.at[slot]).start(priority=r % 2)

    def wait(slot):
        pltpu.make_async_copy(ys_ref.at[pl.ds(0, tm), :], buf_ref.at[slot], sem.at[slot]).wait()

    @pl.when(step == 0)
    def _():
        gather(0, 0, range(tm))

    @pl.when(pl.program_id(1) == 0)
    def _():
        for cat_ref in cat_refs:
            cat_ref[:SSM_HALO, :] = jnp.zeros((SSM_HALO, ch), F32)

    wait(cur)
    x = xn_ref[0] + g2_ref[0] * buf_ref[cur]
    x_ref[0] = x
    h = _mod_norm(x, g_ref[...], sc_ref[0], sh_ref[0])
    hb = h.astype(BF16)

    def conv_matmul(c):
        v = _dot(hb, wx_ref[:, c * ch:(c + 1) * ch])
        cat_refs[c][SSM_HALO:, :] = v
        return v

    def conv_act(c, _):
        cols = slice(c * ch, (c + 1) * ch)
        cat_ref = cat_refs[c]
        y = jnp.zeros((tm, ch), F32) + cb_ref[:, cols]
        for j in range(SSM_CONV):
            y = y + cw_ref[j:j + 1, cols] * cat_ref[pl.ds(SSM_HALO - SSM_CONV + 1 + j, tm), :]
        xbc_ref[0, :, cols] = _silu(y).astype(BF16)
        cat_ref[:SSM_HALO, :] = cat_ref[pl.ds(tm, SSM_HALO), :]

    def gate_matmul(c):
        return _dot(hb, wz_ref[:, c * ch:(c + 1) * ch])

    def gate_act(c, z):
        sz_ref[0, :, c * ch:(c + 1) * ch] = _silu(z).astype(BF16)

    stages = [(conv_matmul, conv_act, c) for c in range(wx_ref.shape[1] // ch)]
    stages += [(gate_matmul, gate_act, c) for c in range(wz_ref.shape[1] // ch)]
    shares = 8
    per_share = tm // shares
    gather(ahead, 1 - cur, range(per_share))
    pending = None
    for k, (matmul, act, c) in enumerate(stages):
        out = matmul(c)
        if k + 1 < shares:
            zero = out[0, 0].astype(I32) * opaque_ref[0]
            gather(ahead, 1 - cur, range((k + 1) * per_share, (k + 2) * per_share), zero)
        if pending is not None:
            pending[0](pending[1], pending[2])
        pending = (act, c, out)
    t = _dot(hb, wdt_ref[...]) + dtb_ref[...]
    pending[0](pending[1], pending[2])
    dt_ref[0] = jnp.maximum(t, 0.0) + jnp.log(1.0 + jnp.exp(-jnp.abs(t)))

    @pl.when(step == last)
    def _():
        wait(1 - cur)


def _ssm_in(dest, ys, x_new, g2, sc, sh, g, wz, wx, wdt, cw, cb, dtb, tm):
    bsz, s, d = x_new.shape
    full = lambda shape: pl.BlockSpec(shape, lambda b, i, *_: (0,) * len(shape), pipeline_mode=pl.Buffered(1))
    per_b = pl.BlockSpec((1, 1, d), lambda b, i, *_: (b, 0, 0))
    tile = lambda n: pl.BlockSpec((1, tm, n), lambda b, i, *_: (b, i, 0))
    return pl.pallas_call(
        _ssm_in_kernel,
        grid_spec=pltpu.PrefetchScalarGridSpec(
            num_scalar_prefetch=2, grid=(bsz, s // tm),
            in_specs=[pl.BlockSpec(memory_space=pl.ANY), tile(d), per_b, per_b, per_b, full(g.shape),
                      full(wz.shape), full(wx.shape), full(wdt.shape), full(cw.shape), full(cb.shape),
                      full(dtb.shape)],
            out_specs=[tile(d), tile(wz.shape[1]), tile(wx.shape[1]), tile(wdt.shape[1])],
            scratch_shapes=[pltpu.VMEM((2, tm, d), F32), pltpu.SemaphoreType.DMA((2,))]
            + [pltpu.VMEM((tm + SSM_HALO, SSM_IN_COLS), F32)] * (wx.shape[1] // SSM_IN_COLS)),
        out_shape=[jax.ShapeDtypeStruct((bsz, s, d), F32),
                   jax.ShapeDtypeStruct((bsz, s, wz.shape[1]), BF16),
                   jax.ShapeDtypeStruct((bsz, s, wx.shape[1]), BF16),
                   jax.ShapeDtypeStruct((bsz, s, wdt.shape[1]), F32)],
        compiler_params=_params(2), name="ssm_in_proj",
    )(dest, jnp.zeros((1,), I32), ys, x_new, g2, sc, sh, g, wz, wx, wdt, cw, cb, dtb)


def _ssd_kernel(xbc_ref, sz_ref, dt_ref, alog_ref, dsk_ref, ng_ref, wout_ref, e2_ref, x_ref, g1_ref,
                g_ref, sc_ref, sh_ref, rwt_ref, rb_ref,
                xn_ref, h2x_ref, cls_ref, cnt_ref, state_ref, macc_ref, y_ref):
    L = SSD_SUB
    inner = sz_ref.shape[2]
    gw = inner // SSM_GROUPS
    pw = 2 * SSM_HEAD_DIM

    @pl.when(pl.program_id(1) == 0)
    def _():
        state_ref[...] = jnp.zeros(state_ref.shape, F32)

    subs = [slice(s * L, (s + 1) * L) for s in range(SSM_CHUNK // L)]
    brow = lax.broadcasted_iota(I32, (SSM_CHUNK, SSM_CHUNK), 0)
    bcol = lax.broadcasted_iota(I32, (SSM_CHUNK, SSM_CHUNK), 1)
    tri = jnp.where((brow // L == bcol // L) & (brow >= bcol), 1.0, 0.0).astype(BF16)
    lower = lax.broadcasted_iota(I32, (L, L), 0) >= lax.broadcasted_iota(I32, (L, L), 1)
    lo_half = lax.broadcasted_iota(I32, (L, pw), 1) < SSM_HEAD_DIM

    def expand(v):
        return _dot(jnp.concatenate(_split2(v), axis=1), e2_ref[...])

    macc_ref[...] = jnp.zeros(macc_ref.shape, F32)

    def group(g, carry):
        xcol = pl.ds(pl.multiple_of(g * gw, gw), gw)
        bcolumns = pl.ds(pl.multiple_of(inner + g * SSM_STATE, SSM_STATE), SSM_STATE)
        ccolumns = pl.ds(pl.multiple_of(inner + (SSM_GROUPS + g) * SSM_STATE, SSM_STATE), SSM_STATE)
        rate = -jnp.exp(alog_ref[g]) * LOG2E
        dt = dt_ref[0, :, pl.ds(pl.multiple_of(g * 128, 128), 128)]
        t3 = _dot(tri, jnp.concatenate(_split3(dt * rate), axis=1))
        a_cs = t3[:, :128] + t3[:, 128:256] + t3[:, 256:]
        xg = xbc_ref[0, :, xcol].astype(F32)
        xdt = xg * expand(dt)
        xdt_b = xdt.astype(BF16)
        bs = [xbc_ref[0, r, bcolumns] for r in subs]
        cs = [xbc_ref[0, r, ccolumns] for r in subs]
        cbm = [jnp.where(lower, _dot_nt(cm, bm), 0.0) for cm, bm in zip(cs, bs)]
        a_end = jnp.concatenate([jnp.broadcast_to(a_cs[r.stop - 1:r.stop, :], (L, 128)) for r in subs], axis=0)
        xw = (xdt * expand(jnp.exp2(a_end - a_cs))).astype(BF16)
        grow = expand(jnp.exp2(a_cs))
        keep = expand(jnp.exp2(jnp.concatenate([a_end[r.start:r.start + 16, :] for r in subs], axis=0)))
        st = state_ref[g]
        y_off = []
        for s, r in enumerate(subs):
            y_off.append(_dot(cs[s], st.astype(BF16)) * grow[r, :])
            st = st * keep[16 * s:16 * s + 1, :] + _dot(bs[s].astype(F32).T.astype(BF16), xw[r, :])
        state_ref[g] = st
        for s, r in enumerate(subs):
            a_sub = a_cs[r, :]
            a_sub_t = a_sub.T
            for q in range(gw // pw):
                x2 = xdt_b[r, q * pw:(q + 1) * pw]
                yp = y_off[s][:, q * pw:(q + 1) * pw]
                for e in range(2):
                    hd = 2 * q + e
                    dec = jnp.exp2(jnp.minimum(a_sub[:, hd:hd + 1] - a_sub_t[hd:hd + 1, :], 0.0))
                    xm = jnp.where(lo_half if e == 0 else jnp.logical_not(lo_half), x2, jnp.zeros_like(x2))
                    yp = yp + _dot((cbm[s] * dec).astype(BF16), xm)
                y_ref[r, q * pw:(q + 1) * pw] = yp
        gt = (y_ref[...] + xg * dsk_ref[g]) * sz_ref[0, :, xcol].astype(F32)
        ms = jnp.mean(gt * gt, axis=-1, keepdims=True)
        gn = gt * lax.rsqrt(ms + NORM_EPS) * ng_ref[g]
        macc_ref[...] += _dot(gn.astype(BF16), wout_ref[xcol, :])
        return carry

    lax.fori_loop(0, SSM_GROUPS, group, 0)
    x_new = x_ref[0] + g1_ref[0] * macc_ref[...]
    xn_ref[0] = x_new
    _post_mixer(x_new, g_ref, sc_ref, sh_ref, rwt_ref, rb_ref, h2x_ref, cls_ref, cnt_ref)


def _ssd(xbc, sz, dt, alog, dskip, ng, wout, e2, x, g1, g, sc, sh, rwt, rb):
    bsz, s, d = x.shape
    L = SSM_CHUNK
    nc = s // L
    gw = sz.shape[2] // SSM_GROUPS
    full = lambda shape: pl.BlockSpec(shape, lambda b, c: (0,) * len(shape))
    per_b = pl.BlockSpec((1, 1, d), lambda b, c: (b, 0, 0))
    tile = lambda n: pl.BlockSpec((1, L, n), lambda b, c: (b, c, 0))
    return pl.pallas_call(
        _ssd_kernel,
        grid=(bsz, nc),
        in_specs=[tile(xbc.shape[2]), tile(sz.shape[2]), tile(dt.shape[2]),
                  full(alog.shape), full(dskip.shape), full(ng.shape), full(wout.shape), full(e2.shape),
                  tile(d), per_b, full(g.shape), per_b, per_b, full(rwt.shape), full(rb.shape)],
        out_specs=[tile(d),
                   pl.BlockSpec((L, d + W_LANES), lambda b, c: (b * nc + c, 0)),
                   pl.BlockSpec((1, 1, L), lambda b, c: (b * nc + c, 0, 0)),
                   pl.BlockSpec((CLASS_ROWS, 128), lambda b, c: (0, 0))],
        out_shape=[jax.ShapeDtypeStruct((bsz, s, d), F32),
                   jax.ShapeDtypeStruct((bsz * s, d + W_LANES), F32),
                   jax.ShapeDtypeStruct((bsz * nc, 1, L), I32),
                   jax.ShapeDtypeStruct((CLASS_ROWS, 128), F32)],
        scratch_shapes=[pltpu.VMEM((SSM_GROUPS, SSM_STATE, gw), F32),
                        pltpu.VMEM((L, d), F32),
                        pltpu.VMEM((L, gw), F32)],
        compiler_params=_params(2), name="ssd_out_proj",
    )(xbc, sz, dt, alog, dskip, ng, wout, e2, x, g1, g, sc, sh, rwt, rb)


def _dest_kernel(cls_ref, start_ref, dest_ref, run_ref):
    @pl.when(pl.program_id(0) == 0)
    def _():
        run_ref[...] = start_ref[...]

    tr = cls_ref.shape[2]
    upper = lax.broadcasted_iota(I32, (tr, tr), 0) <= lax.broadcasted_iota(I32, (tr, tr), 1)
    upper = jnp.where(upper, 1.0, 0.0).astype(BF16)
    for k in range(cls_ref.shape[0]):
        onehot = lax.broadcasted_iota(I32, (CLASS_ROWS, tr), 0) == cls_ref[k]
        oh = jnp.where(onehot, 1.0, 0.0)
        prefix = _dot(oh.astype(BF16), upper)
        dest = jnp.sum(oh * (prefix - 1.0 + run_ref[...]), axis=0, keepdims=True)
        dest_ref[k] = dest.astype(I32)
        run_ref[...] += jnp.sum(oh, axis=1, keepdims=True)


def _moe_plan(cls, cnt, tr):
    nt = cls.shape[0]
    t = nt * tr
    reps = 4 if nt % 4 == 0 else 1
    padded = jnp.ceil(cnt[:, 0] / MOE_TILE) * MOE_TILE
    end = jnp.cumsum(padded)
    start = end - padded
    dest = pl.pallas_call(
        _dest_kernel, grid=(nt // reps,),
        in_specs=[pl.BlockSpec((reps, 1, tr), lambda i: (i, 0, 0)),
                  pl.BlockSpec((CLASS_ROWS, 1), lambda i: (0, 0))],
        out_specs=pl.BlockSpec((reps, 1, tr), lambda i: (i, 0, 0)),
        out_shape=jax.ShapeDtypeStruct((nt, 1, tr), I32),
        scratch_shapes=[pltpu.VMEM((CLASS_ROWS, 1), F32)],
        compiler_params=_params(1), name="moe_dest",
    )(cls, start.reshape(CLASS_ROWS, 1))
    n_tiles = t // MOE_TILE + N_CLASSES
    tile_row = jnp.arange(n_tiles, dtype=F32) * MOE_TILE
    total = end[N_CLASSES - 1]
    valid = tile_row < total
    tcls = jnp.sum((tile_row[:, None] >= end[None, :N_CLASSES]).astype(I32), axis=1)
    last = jnp.sum((total - MOE_TILE >= end[:N_CLASSES]).astype(I32))
    tcls = jnp.where(valid, tcls, last)
    grp = tcls // N_PAIRS
    pair = tcls % N_PAIRS
    lo = jnp.where(pair < 3, 0, jnp.where(pair < 5, 1, 2))
    hi = jnp.where(pair < 3, pair + 1, jnp.where(pair < 5, pair - 1, 3))
    meta = jnp.stack([grp * EXPERTS_PER_GROUP + lo, grp * EXPERTS_PER_GROUP + hi, valid.astype(I32),
                      jnp.zeros_like(tcls)]).astype(I32)
    return dest.reshape(t), meta


def _invert_kernel(dest_ref, zeros_ref, src_ref, sem):
    clear = pltpu.make_async_copy(zeros_ref, src_ref, sem)
    clear.start()
    clear.wait()

    def put(t0, carry):
        for u in range(INVERT_UNROLL):
            t = t0 * INVERT_UNROLL + u
            src_ref[dest_ref[t]] = t
        return carry

    lax.fori_loop(0, dest_ref.shape[0] // INVERT_UNROLL, put, 0)


def _invert(dest, n_slots):
    smem = pl.BlockSpec(memory_space=pltpu.SMEM)
    return pl.pallas_call(
        _invert_kernel, in_specs=[smem, pl.BlockSpec(memory_space=pl.ANY)], out_specs=smem,
        out_shape=jax.ShapeDtypeStruct((n_slots,), I32),
        scratch_shapes=[pltpu.SemaphoreType.DMA(())], name="moe_invert",
    )(dest, jnp.zeros((n_slots,), I32))


def _expert_kernel(meta_ref, src_ref, h2x_ref, wgu_a_ref, wgu_b_ref, wd_a_ref, wd_b_ref, y_ref,
                   buf0_ref, buf1_ref, sem):
    i = pl.program_id(0)
    d = y_ref.shape[1]
    bufs = (buf0_ref, buf1_ref)

    def gather(tile, slot, rows=range(MOE_TILE), after=0):
        for r in rows:
            tok = src_ref[tile * MOE_TILE + r] + after
            pltpu.make_async_copy(h2x_ref.at[pl.ds(tok, 1), :], bufs[slot].at[pl.ds(r, 1), :],
                                  sem.at[slot]).start(priority=r % 2)

    def wait(slot):
        pltpu.make_async_copy(h2x_ref.at[pl.ds(0, MOE_TILE), :], bufs[slot], sem.at[slot]).wait()

    @pl.when(i == 0)
    def _():
        gather(0, 0)

    valid = meta_ref[2, i] != 0
    cw = EXPERT_COLS
    for cur in range(2):
        @pl.when(valid & (i % 2 == cur))
        def _(cur=cur):
            wait(cur)
            xb = bufs[cur][:, :d].astype(BF16)
            wts = bufs[cur][:, d:]
            n_parts = (wgu_a_ref.shape[2] + d) // cw
            per_part = MOE_TILE // n_parts
            gather(i + 1, 1 - cur, range(per_part))
            part = [1]

            def pace(result):
                if part[0] == n_parts:
                    return
                zero = result[0, 0].astype(I32) * meta_ref[3, i]
                gather(i + 1, 1 - cur, range(part[0] * per_part, (part[0] + 1) * per_part), zero)
                part[0] += 1

            def dot_cols(a, w_ref):
                chunks = []
                for c in range(w_ref.shape[2] // cw):
                    chunks.append(_dot(a, w_ref[0, :, c * cw:(c + 1) * cw]))
                    pace(chunks[-1])
                return jnp.concatenate(chunks, axis=1)

            y = jnp.zeros(y_ref.shape, F32)
            for e, (wgu_ref, wd_ref) in enumerate(((wgu_a_ref, wd_a_ref), (wgu_b_ref, wd_b_ref))):
                gu = dot_cols(xb, wgu_ref)
                ff = gu.shape[1] // 2
                act = _silu(gu[:, :ff]) * gu[:, ff:]
                y = y + wts[:, e:e + 1] * dot_cols(act.astype(BF16), wd_ref)
            y_ref[...] = y

        @pl.when(jnp.logical_not(valid) & (meta_ref[2, jnp.maximum(i - 1, 0)] != 0) & (i % 2 == cur))
        def _(cur=cur):
            wait(cur)

    @pl.when(jnp.logical_not(valid))
    def _():
        y_ref[...] = jnp.zeros(y_ref.shape, F32)


def _experts(meta, src, h2x, wgu, wd):
    n_slots = src.shape[0]
    w = h2x.shape[1]
    d = w - W_LANES
    n_tiles = n_slots // MOE_TILE
    return pl.pallas_call(
        _expert_kernel,
        grid_spec=pltpu.PrefetchScalarGridSpec(
            num_scalar_prefetch=2, grid=(n_tiles,),
            in_specs=[pl.BlockSpec(memory_space=pl.ANY),
                      pl.BlockSpec((1,) + wgu.shape[1:], lambda i, m, s: (m[0, i], 0, 0)),
                      pl.BlockSpec((1,) + wgu.shape[1:], lambda i, m, s: (m[1, i], 0, 0)),
                      pl.BlockSpec((1,) + wd.shape[1:], lambda i, m, s: (m[0, i], 0, 0)),
                      pl.BlockSpec((1,) + wd.shape[1:], lambda i, m, s: (m[1, i], 0, 0))],
            out_specs=pl.BlockSpec((MOE_TILE, d), lambda i, m, s: (i, 0)),
            scratch_shapes=[pltpu.VMEM((MOE_TILE, w), F32), pltpu.VMEM((MOE_TILE, w), F32),
                            pltpu.SemaphoreType.DMA((2,))]),
        out_shape=jax.ShapeDtypeStruct((n_slots, d), F32),
        compiler_params=_params(1), name="moe_experts",
    )(meta, src, h2x, wgu, wgu, wd, wd)


def _combine_kernel(dest_ref, ys_ref, x_ref, g2_ref, o_ref, buf0_ref, buf1_ref, sem):
    tm = x_ref.shape[1]
    step = pl.program_id(0) * pl.num_programs(1) + pl.program_id(1)
    n_steps = pl.num_programs(0) * pl.num_programs(1)
    bufs = (buf0_ref, buf1_ref)

    def gather(st, slot):
        for r in range(tm):
            pltpu.make_async_copy(ys_ref.at[pl.ds(dest_ref[st * tm + r], 1), :], bufs[slot].at[pl.ds(r, 1), :],
                                  sem.at[slot]).start(priority=r % 2)

    @pl.when(step == 0)
    def _():
        gather(0, 0)

    for cur in range(2):
        @pl.when(step % 2 == cur)
        def _(cur=cur):
            @pl.when(step + 1 < n_steps)
            def _():
                gather(step + 1, 1 - cur)

            pltpu.make_async_copy(ys_ref.at[pl.ds(0, tm), :], bufs[cur], sem.at[cur]).wait()
            o_ref[0] = x_ref[0] + g2_ref[0] * bufs[cur][...]


def _combine(dest, ys, x, g2, tm):
    bsz, s, d = x.shape
    return pl.pallas_call(
        _combine_kernel,
        grid_spec=pltpu.PrefetchScalarGridSpec(
            num_scalar_prefetch=1, grid=(bsz, s // tm),
            in_specs=[pl.BlockSpec(memory_space=pl.ANY),
                      pl.BlockSpec((1, tm, d), lambda b, i, dd: (b, i, 0)),
                      pl.BlockSpec((1, 1, d), lambda b, i, dd: (b, 0, 0))],
            out_specs=pl.BlockSpec((1, tm, d), lambda b, i, dd: (b, i, 0)),
            scratch_shapes=[pltpu.VMEM((tm, d), F32), pltpu.VMEM((tm, d), F32), pltpu.SemaphoreType.DMA((2,))]),
        out_shape=jax.ShapeDtypeStruct((bsz, s, d), F32),
        compiler_params=_params(2), name="moe_combine",
    )(dest, ys, x, g2)


def _moe(h2x, cls, cnt, wgu, wd):
    t = h2x.shape[0]
    tr = min(512, t)
    dest, meta = _moe_plan(cls.reshape(t // tr, 1, tr), cnt, tr)
    n_slots = t + N_CLASSES * MOE_TILE
    return dest, _experts(meta, _invert(dest, n_slots), h2x, wgu, wd)


def kernel(x, c, ada_w, ada_b, norm_mix, norm_ffn, hy_w_in, hy_q_norm, hy_k_norm, hy_dw_w, hy_dw_b, hy_ln_g, hy_ln_b, hy_w_out, ssm_w_in, ssm_conv_w, ssm_conv_b, ssm_dt_bias, ssm_a_log, ssm_d, ssm_norm, ssm_w_out, router_w, router_bias, exp_w_gate, exp_w_up, exp_w_down):
    bsz, s, d = x.shape
    depth = ada_w.shape[0]
    assert s % 512 == 0 and bsz <= 8
    c8 = jnp.zeros((8, d), F32).at[:bsz].set(c)
    mod = _ada(c8, ada_w, ada_b)
    rwt = router_w.T
    rb = router_bias.reshape(N_EXPERTS, 1)
    lane_head = jnp.arange(ATT_WIDTH) // HEAD_DIM
    pmat = ((lane_head[:, None] == lane_head[None, :]).astype(F32) / HEAD_DIM).astype(BF16)

    moe = None
    for layer in range(depth):
        sh1, sc1, g1, sh2, sc2, g2 = (mod[layer, :bsz, i * d:(i + 1) * d].reshape(bsz, 1, d) for i in range(N_MOD))
        gm = norm_mix[layer].reshape(1, d)
        gf = norm_ffn[layer].reshape(1, d)
        j = layer // 2
        if layer % 2 == 0:
            if moe is not None:
                x = _combine(*moe, min(256, s))
            w_in = hy_w_in[j].astype(BF16)
            aw = ATT_WIDTH
            k, qt, vt, u, kmean = _hy_in(
                x, sc1, sh1, gm, w_in[:, aw:2 * aw], w_in[:, 3 * aw:], w_in[:, :aw].T, w_in[:, 2 * aw:3 * aw].T,
                (jnp.tile(hy_q_norm[j], ATT_HEADS) * (HEAD_DIM ** -0.5 * LOG2E)).reshape(aw, 1),
                jnp.tile(hy_k_norm[j], ATT_HEADS).reshape(1, aw), pmat, 512)
            bias = _gate(kmean.reshape(bsz, s // MOBA_BLOCK, aw), qt, min(1024, s))
            att_t = _attn(qt, k, vt, bias, 8)
            w_out = hy_w_out[j].astype(BF16)
            dww = jnp.zeros((CONV_HALO, hy_dw_w.shape[2]), F32).at[:CONV_WIDTH].set(hy_dw_w[j])
            x_new, h2x, cls, cnt = _hy_out(
                u, att_t, x, g1, w_out[:aw], w_out[aw:], dww, hy_dw_b[j].reshape(1, -1),
                hy_ln_g[j].reshape(1, -1), hy_ln_b[j].reshape(1, -1), gf, sc2, sh2, rwt, rb, 512)
        else:
            w_in = ssm_w_in[j]
            inner = ssm_norm.shape[1]
            heads = ssm_a_log.shape[1]
            hpg = heads // SSM_GROUPS
            conv_dim = ssm_conv_w.shape[2]

            def by_group(v):
                v = v.reshape(v.shape[:-1] + (SSM_GROUPS, hpg))
                pad = [(0, 0)] * (v.ndim - 1) + [(0, 128 - hpg)]
                return jnp.pad(v, pad).reshape(v.shape[:-2] + (SSM_GROUPS * 128,))

            x, sz, xbc, dt = _ssm_in(
                *moe, sc1, sh1, gm, w_in[:, :inner].astype(BF16), w_in[:, inner:inner + conv_dim].astype(BF16),
                by_group(w_in[:, inner + conv_dim:]).astype(BF16), ssm_conv_w[j], ssm_conv_b[j].reshape(1, -1),
                by_group(ssm_dt_bias[j]).reshape(1, -1), 512)
            chan_head = jnp.arange(inner // SSM_GROUPS) // SSM_HEAD_DIM
            e2 = (jnp.arange(256)[:, None] % 128 == chan_head[None, :]).astype(BF16)
            x_new, h2x, cls, cnt = _ssd(
                xbc, sz, dt, by_group(ssm_a_log[j]).reshape(SSM_GROUPS, 1, 128),
                jnp.repeat(ssm_d[j], SSM_HEAD_DIM).reshape(SSM_GROUPS, 1, -1),
                ssm_norm[j].reshape(SSM_GROUPS, 1, -1), ssm_w_out[j].astype(BF16), e2, x, g1, gf, sc2, sh2, rwt, rb)
        wgu = jnp.concatenate([exp_w_gate[layer], exp_w_up[layer]], axis=-1).astype(BF16)
        moe = _moe(h2x, cls, cnt, wgu, exp_w_down[layer].astype(BF16)) + (x_new, g2)
    return _combine(*moe, min(256, s))
```

```python
import functools

import jax
import jax.numpy as jnp
from jax import lax
from jax.experimental import pallas as pl
from jax.experimental.pallas import tpu as pltpu

F32 = jnp.float32
BF16 = jnp.bfloat16
I32 = jnp.int32

NORM_EPS = 1e-6
N_MOD = 6
HEAD_DIM = 64
ATT_HEADS = 8
ATT_WIDTH = ATT_HEADS * HEAD_DIM
MOBA_BLOCK = 256
MOBA_TOPK = 3
CONV_WIDTH = 31
CONV_HALO = 32
SSM_HEAD_DIM = 64
SSM_GROUPS = 4
SSM_STATE = 128
SSM_CONV = 4
SSM_CHUNK = 256
SSM_HALO = 8
SSD_SUB = 128
SSD_BLOCK = 512
SSM_IN_COLS = 512
N_EXPERTS = 16
N_EXPERT_GROUPS = 4
EXPERTS_PER_GROUP = 4
N_PAIRS = 6
N_CLASSES = N_EXPERT_GROUPS * N_PAIRS
CLASS_ROWS = 32
MOE_TILE = 256
EXPERT_COLS = 256
W_LANES = 128
INVERT_UNROLL = 32
LOG2E = 1.4426950408889634
NEG = -1e30
VMEM_LIMIT = 56 * 1024 * 1024


def _params(n_axes):
    return pltpu.CompilerParams(dimension_semantics=("arbitrary",) * n_axes,
                                vmem_limit_bytes=VMEM_LIMIT)


def _dot(a, b):
    return jnp.dot(a, b, preferred_element_type=F32)


def _dot_nt(a, b):
    return lax.dot_general(a, b, (((1,), (1,)), ((), ())), preferred_element_type=F32)


def _split2(x):
    hi = x.astype(BF16)
    lo = (x - hi.astype(F32)).astype(BF16)
    return hi, lo


def _split3(x):
    a = x.astype(BF16)
    r = x - a.astype(F32)
    b = r.astype(BF16)
    c = (r - b.astype(F32)).astype(BF16)
    return a, b, c


def _silu(x):
    return x * jax.nn.sigmoid(x)


def _mod_norm(x, g, sc, sh):
    ms = jnp.mean(x * x, axis=-1, keepdims=True)
    return x * lax.rsqrt(ms + NORM_EPS) * g * (1.0 + sc) + sh


def _ada_kernel(c_ref, w_ref, b_ref, o_ref):
    cond = _silu(c_ref[...])
    ch, cl = _split2(cond)
    wh, wl = _split2(w_ref[0])
    o_ref[0] = _dot(ch, wh) + _dot(ch, wl) + _dot(cl, wh) + b_ref[0]


def _ada(c8, ada_w, ada_b):
    depth, d, n = ada_w.shape
    tn = 1536
    return pl.pallas_call(
        _ada_kernel,
        grid=(depth, n // tn),
        in_specs=[pl.BlockSpec((8, d), lambda l, j: (0, 0)),
                  pl.BlockSpec((1, d, tn), lambda l, j: (l, 0, j)),
                  pl.BlockSpec((1, 1, tn), lambda l, j: (l, 0, j))],
        out_specs=pl.BlockSpec((1, 8, tn), lambda l, j: (l, 0, j)),
        out_shape=jax.ShapeDtypeStruct((depth, 8, n), F32),
        compiler_params=_params(2), name="ada_mod",
    )(c8, ada_w, ada_b.reshape(depth, 1, n))


def _hy_in_kernel(x_ref, sc_ref, sh_ref, g_ref, wk_ref, wu_ref, wqt_ref, wvt_ref, qg_ref, kg_ref, p_ref,
                  k_ref, qt_ref, vt_ref, u_ref, km_ref):
    tm = x_ref.shape[1]
    h = _mod_norm(x_ref[0], g_ref[...], sc_ref[0], sh_ref[0])
    hb = h.astype(BF16)
    k = _dot(hb, wk_ref[...])
    khi, klo = _split2(k * k)
    ms = _dot(khi, p_ref[...]) + _dot(klo, p_ref[...])
    kn = k * lax.rsqrt(ms + NORM_EPS) * kg_ref[...]
    k_ref[0] = kn.astype(BF16)
    km_ref[0, 0] = jnp.mean(kn.reshape(tm // MOBA_BLOCK, MOBA_BLOCK, ATT_WIDTH), axis=1)
    qt = _dot_nt(wqt_ref[...], hb).reshape(ATT_HEADS, HEAD_DIM, tm)
    qms = jnp.mean(qt * qt, axis=1, keepdims=True)
    qn = (qt * lax.rsqrt(qms + NORM_EPS)).reshape(ATT_WIDTH, tm) * qg_ref[...]
    qt_ref[0] = qn.astype(BF16)
    vt_ref[0] = _dot_nt(wvt_ref[...], hb).astype(BF16)
    ag = _dot(hb, wu_ref[...])
    half = ag.shape[1] // 2
    u_ref[0] = (ag[:, :half] * jax.nn.sigmoid(ag[:, half:])).astype(BF16)


def _hy_in(x, sc, sh, g, wk, wu, wqt, wvt, qg, kg, pmat, tm):
    bsz, s, d = x.shape
    cc = wu.shape[1] // 2
    full = lambda shape: pl.BlockSpec(shape, lambda b, i: (0,) * len(shape))
    return pl.pallas_call(
        _hy_in_kernel,
        grid=(bsz, s // tm),
        in_specs=[pl.BlockSpec((1, tm, d), lambda b, i: (b, i, 0)),
                  pl.BlockSpec((1, 1, d), lambda b, i: (b, 0, 0)),
                  pl.BlockSpec((1, 1, d), lambda b, i: (b, 0, 0)),
                  full((1, d)), full(wk.shape), full(wu.shape), full(wqt.shape), full(wvt.shape),
                  full(qg.shape), full(kg.shape), full(pmat.shape)],
        out_specs=[pl.BlockSpec((1, tm, ATT_WIDTH), lambda b, i: (b, i, 0)),
                   pl.BlockSpec((1, ATT_WIDTH, tm), lambda b, i: (b, 0, i)),
                   pl.BlockSpec((1, ATT_WIDTH, tm), lambda b, i: (b, 0, i)),
                   pl.BlockSpec((1, tm, cc), lambda b, i: (b, i, 0)),
                   pl.BlockSpec((1, 1, tm // MOBA_BLOCK, ATT_WIDTH), lambda b, i: (b, i, 0, 0))],
        out_shape=[jax.ShapeDtypeStruct((bsz, s, ATT_WIDTH), BF16),
                   jax.ShapeDtypeStruct((bsz, ATT_WIDTH, s), BF16),
                   jax.ShapeDtypeStruct((bsz, ATT_WIDTH, s), BF16),
                   jax.ShapeDtypeStruct((bsz, s, cc), BF16),
                   jax.ShapeDtypeStruct((bsz, s // tm, tm // MOBA_BLOCK, ATT_WIDTH), F32)],
        compiler_params=_params(2), name="hy_in_proj",
    )(x, sc, sh, g, wk, wu, wqt, wvt, qg, kg, pmat)


def _gate_kernel(km_ref, qt_ref, o_ref):
    nb = km_ref.shape[1]
    ts = qt_ref.shape[2]
    rows = ATT_HEADS * nb
    km = km_ref[0]
    kmt = jnp.broadcast_to(km[None], (ATT_HEADS, nb, ATT_WIDTH)).reshape(rows, ATT_WIDTH)
    rh = lax.broadcasted_iota(I32, (rows, ATT_WIDTH), 0) // nb
    ch = lax.broadcasted_iota(I32, (rows, ATT_WIDTH), 1) // HEAD_DIM
    kbd = jnp.where(rh == ch, kmt, 0.0)
    khi, klo = _split2(kbd)
    qt = qt_ref[0]
    gate = (_dot(khi, qt) + _dot(klo, qt)).reshape(ATT_HEADS, nb, ts)
    blk = lax.broadcasted_iota(I32, (ATT_HEADS, nb, ts), 1)
    qblk = (pl.program_id(1) * ts + lax.broadcasted_iota(I32, (ATT_HEADS, nb, ts), 2)) // MOBA_BLOCK
    g = jnp.where(blk < qblk, gate, -jnp.inf)
    sel = jnp.zeros(g.shape, jnp.bool_)
    for _ in range(MOBA_TOPK):
        m = jnp.max(g, axis=1, keepdims=True)
        first = jnp.min(jnp.where((g == m) & (m > -jnp.inf), blk, nb), axis=1, keepdims=True)
        pick = blk == first
        sel = sel | pick
        g = jnp.where(pick, -jnp.inf, g)
    o_ref[0] = jnp.where(sel, 0.0, NEG).reshape(rows, ts)


def _gate(kmean, qt, ts):
    bsz, nb, _ = kmean.shape
    s = qt.shape[2]
    rows = ATT_HEADS * nb
    return pl.pallas_call(
        _gate_kernel,
        grid=(bsz, s // ts),
        in_specs=[pl.BlockSpec((1, nb, ATT_WIDTH), lambda b, i: (b, 0, 0)),
                  pl.BlockSpec((1, ATT_WIDTH, ts), lambda b, i: (b, 0, i))],
        out_specs=pl.BlockSpec((1, rows, ts), lambda b, i: (b, 0, i)),
        out_shape=jax.ShapeDtypeStruct((bsz, rows, s), F32),
        compiler_params=_params(2), name="moba_gate",
    )(kmean, qt)


def _attn_kernel(qt_ref, k_ref, vt_ref, bias_ref, o_ref, qa_ref, acc_ref, m_ref, *, nb, hp, lag, kb):
    blk = MOBA_BLOCK
    pw = 2 * HEAD_DIM
    va = HEAD_DIM + 16
    i = pl.program_id(2)
    row = lax.broadcasted_iota(I32, (pw, blk), 0)
    for h in range(hp):
        qp = qt_ref[0, (h // 2) * pw:(h // 2 + 1) * pw, :]
        keep = (row < HEAD_DIM) if h % 2 == 0 else (row >= HEAD_DIM)
        qa_ref[h, :pw, :] = jnp.where(keep, qp, jnp.zeros_like(qp))
        qa_ref[h, pw:pw + nb, :] = bias_ref[0, h * nb:(h + 1) * nb, :].astype(BF16)
        qa_ref[h, pw + nb:, :] = jnp.zeros((blk - pw - nb, blk), BF16)
        m_ref[h] = jnp.full((1, blk), NEG, F32)
        acc_ref[h] = jnp.zeros((va, blk), F32)
    causal = lax.broadcasted_iota(I32, (blk, blk), 0) <= lax.broadcasted_iota(I32, (blk, blk), 1)
    lane = lax.broadcasted_iota(I32, (blk, pw), 1)
    ones = jnp.ones((va - HEAD_DIM, blk), BF16)

    def step(blocks):
        units = [(b, h) for b in range(len(blocks)) for h in range(hp)]
        offs = [pl.multiple_of(j * blk, blk) for j, _, _ in blocks]

        def scores(u):
            b, h = units[u]
            pp = h // 2
            ka = jnp.concatenate([k_ref[0, pl.ds(offs[b], blk), pp * pw:(pp + 1) * pw], blocks[b][1]], axis=1)
            return blocks[b][2](_dot(ka, qa_ref[h]))

        def softmax(u, s):
            h = units[u][1]
            m_old = m_ref[h]
            m_new = jnp.maximum(m_old, jnp.max(s, axis=0, keepdims=True))
            m_ref[h] = m_new
            return jnp.exp2(s - m_new).astype(BF16), jnp.exp2(m_old - m_new)

        def accumulate(u, p, alpha):
            b, h = units[u]
            vj = jnp.concatenate([vt_ref[0, h * HEAD_DIM:(h + 1) * HEAD_DIM, pl.ds(offs[b], blk)], ones], axis=0)
            acc_ref[h] = alpha * acc_ref[h] + _dot(vj, p)

        s, pa = {}, {}
        for t in range(len(units) + lag):
            if t < len(units):
                s[t] = scores(t)
            if 1 <= t <= len(units):
                pa[t - 1] = softmax(t - 1, s.pop(t - 1))
            if t >= lag:
                accumulate(t - lag, *pa.pop(t - lag))

    def past(j):
        return (j, jnp.where(lane == j, 1.0, 0.0).astype(BF16), lambda s: s)

    own = (i, jnp.zeros((blk, pw), BF16), lambda s: jnp.where(causal, s, NEG))

    @pl.when(i == 0)
    def _():
        step([own])

    @pl.when(i > 0)
    def _():
        step([own, past(0)])

    n_rest = jnp.maximum(i - 1, 0)
    rem = n_rest % kb
    base = jnp.int32(1)
    size = 1
    while size < kb:
        take = (rem & size) != 0

        @pl.when(take)
        def _(base=base, size=size):
            step([past(base + t) for t in range(size)])

        base = base + jnp.where(take, size, 0)
        size *= 2

    def body(r, carry):
        step([past(1 + rem + kb * r + t) for t in range(kb)])
        return carry

    lax.fori_loop(0, n_rest // kb, body, 0)
    for h in range(hp):
        acc = acc_ref[h]
        o_ref[0, h * HEAD_DIM:(h + 1) * HEAD_DIM, :] = (acc[:HEAD_DIM] / acc[HEAD_DIM:HEAD_DIM + 1]).astype(BF16)


def _attn(qt, k, vt, bias, hp):
    bsz, s, _ = k.shape
    nb = s // MOBA_BLOCK
    assert nb <= MOBA_BLOCK - 2 * HEAD_DIM and nb % 16 == 0
    hw = hp * HEAD_DIM
    return pl.pallas_call(
        functools.partial(_attn_kernel, nb=nb, hp=hp, lag=6, kb=4),
        grid=(bsz, ATT_HEADS // hp, nb),
        in_specs=[pl.BlockSpec((1, hw, MOBA_BLOCK), lambda b, p, i: (b, p, i)),
                  pl.BlockSpec((1, s, hw), lambda b, p, i: (b, 0, p)),
                  pl.BlockSpec((1, hw, s), lambda b, p, i: (b, p, 0)),
                  pl.BlockSpec((1, hp * nb, MOBA_BLOCK), lambda b, p, i: (b, p, i))],
        out_specs=pl.BlockSpec((1, hw, MOBA_BLOCK), lambda b, p, i: (b, p, i)),
        out_shape=jax.ShapeDtypeStruct((bsz, ATT_WIDTH, s), BF16),
        scratch_shapes=[pltpu.VMEM((hp, MOBA_BLOCK, MOBA_BLOCK), BF16),
                        pltpu.VMEM((hp, HEAD_DIM + 16, MOBA_BLOCK), F32),
                        pltpu.VMEM((hp, 1, MOBA_BLOCK), F32)],
        compiler_params=_params(3), name="moba_attn",
    )(qt, k, vt, bias)


def _route(logit_t, rb):
    aff = jax.nn.sigmoid(logit_t)
    score = aff + rb
    s = [score[e:e + 1, :] for e in range(N_EXPERTS)]
    a = [aff[e:e + 1, :] for e in range(N_EXPERTS)]
    n = EXPERTS_PER_GROUP

    def top2_sum(v):
        best = None
        for x in range(n):
            for y in range(x + 1, n):
                t = v[x] + v[y]
                best = t if best is None else jnp.maximum(best, t)
        return best

    def first_argmax(v):
        idx = jnp.zeros(v[0].shape, I32)
        cur = v[0]
        for x in range(1, len(v)):
            better = v[x] > cur
            idx = jnp.where(better, x, idx)
            cur = jnp.where(better, v[x], cur)
        return idx

    grp = first_argmax([top2_sum(s[n * g:n * g + n]) for g in range(N_EXPERT_GROUPS)])

    def in_group(v, x):
        out = v[(N_EXPERT_GROUPS - 1) * n + x]
        for g in range(N_EXPERT_GROUPS - 2, -1, -1):
            out = jnp.where(grp == g, v[n * g + x], out)
        return out

    sg = [in_group(s, x) for x in range(n)]
    ag = [in_group(a, x) for x in range(n)]
    l1 = first_argmax(sg)
    l2 = first_argmax([jnp.where(l1 == x, -jnp.inf, sg[x]) for x in range(n)])
    lo = jnp.minimum(l1, l2)
    hi = jnp.maximum(l1, l2)
    pair = jnp.where(lo == 0, hi - 1, jnp.where(lo == 1, hi + 1, 5))
    a_lo = jnp.zeros_like(ag[0])
    a_hi = jnp.zeros_like(ag[0])
    for x in range(n):
        a_lo = jnp.where(lo == x, ag[x], a_lo)
        a_hi = jnp.where(hi == x, ag[x], a_hi)
    tot = a_lo + a_hi
    return grp * N_PAIRS + pair, a_lo / tot, a_hi / tot


def _post_mixer(x_new, g_ref, sc_ref, sh_ref, rwt_ref, rb_ref, h2x_ref, cls_ref, cnt_ref):
    tm, d = x_new.shape
    h2 = _mod_norm(x_new, g_ref[...], sc_ref[0], sh_ref[0])
    hh, hl = _split2(h2)
    rh, rl = _split2(rwt_ref[...])
    logit_t = _dot_nt(rh, hh) + _dot_nt(rh, hl) + _dot_nt(rl, hh)
    cls, w_lo, w_hi = _route(logit_t, rb_ref[...])
    cls_ref[0] = cls

    @pl.when((pl.program_id(0) == 0) & (pl.program_id(1) == 0))
    def _():
        cnt_ref[...] = jnp.zeros(cnt_ref.shape, F32)

    onehot = lax.broadcasted_iota(I32, (CLASS_ROWS, tm), 0) == cls
    cnt_ref[...] += jnp.sum(jnp.where(onehot, 1.0, 0.0), axis=1, keepdims=True)
    wrow = lax.broadcasted_iota(I32, (W_LANES, tm), 0)
    wt = jnp.where(wrow == 0, w_lo, jnp.where(wrow == 1, w_hi, 0.0))
    h2x_ref[:, :d] = h2
    h2x_ref[:, d:] = wt.T


def _hy_out_kernel(u_ref, up_ref, at_ref, x_ref, g1_ref, wtop_ref, wbot_ref, dww_ref, dwb_ref, lng_ref, lnb_ref,
                   g_ref, sc_ref, sh_ref, rwt_ref, rb_ref,
                   xn_ref, h2x_ref, cls_ref, cnt_ref, cat_ref, shifted_ref):
    tm = u_ref.shape[1]
    m_att = _dot(at_ref[0].astype(F32).T.astype(BF16), wtop_ref[...])
    prev = up_ref[0].astype(F32)
    cat_ref[:CONV_HALO, :] = jnp.where(pl.program_id(1) == 0, 0.0, prev)
    cat_ref[CONV_HALO:, :] = u_ref[0].astype(F32)
    span = tm + CONV_HALO - 8
    for r in range(1, 8):
        shifted_ref[r - 1] = cat_ref[pl.ds(r, span), :]
    y = jnp.zeros((tm, u_ref.shape[2]), F32) + dwb_ref[...]
    for j in range(CONV_WIDTH):
        a, r = divmod(CONV_HALO - CONV_WIDTH + 1 + j, 8)
        tap = cat_ref[pl.ds(8 * a, tm), :] if r == 0 else shifted_ref[r - 1, pl.ds(8 * a, tm), :]
        y = y + dww_ref[j:j + 1, :] * tap
    mu = jnp.mean(y, axis=-1, keepdims=True)
    var = jnp.mean(jnp.square(y - mu), axis=-1, keepdims=True)
    cv = _silu((y - mu) * lax.rsqrt(var + NORM_EPS) * lng_ref[...] + lnb_ref[...])
    m = m_att + _dot(cv.astype(BF16), wbot_ref[...])
    x_new = x_ref[0] + g1_ref[0] * m
    xn_ref[0] = x_new
    _post_mixer(x_new, g_ref, sc_ref, sh_ref, rwt_ref, rb_ref, h2x_ref, cls_ref, cnt_ref)


def _hy_out(u, att_t, x, g1, wtop, wbot, dww, dwb, lng, lnb, g, sc, sh, rwt, rb, tm):
    bsz, s, d = x.shape
    cc = u.shape[2]
    nt = s // tm
    full = lambda shape: pl.BlockSpec(shape, lambda b, i: (0,) * len(shape))
    per_b = pl.BlockSpec((1, 1, d), lambda b, i: (b, 0, 0))
    halo = tm // CONV_HALO
    return pl.pallas_call(
        _hy_out_kernel,
        grid=(bsz, nt),
        in_specs=[pl.BlockSpec((1, tm, cc), lambda b, i: (b, i, 0)),
                  pl.BlockSpec((1, CONV_HALO, cc), lambda b, i: (b, jnp.maximum(i * halo - 1, 0), 0)),
                  pl.BlockSpec((1, ATT_WIDTH, tm), lambda b, i: (b, 0, i)),
                  pl.BlockSpec((1, tm, d), lambda b, i: (b, i, 0)),
                  per_b, full(wtop.shape), full(wbot.shape), full(dww.shape), full(dwb.shape),
                  full(lng.shape), full(lnb.shape), full(g.shape), per_b, per_b, full(rwt.shape), full(rb.shape)],
        out_specs=[pl.BlockSpec((1, tm, d), lambda b, i: (b, i, 0)),
                   pl.BlockSpec((tm, d + W_LANES), lambda b, i: (b * nt + i, 0)),
                   pl.BlockSpec((1, 1, tm), lambda b, i: (b * nt + i, 0, 0)),
                   pl.BlockSpec((CLASS_ROWS, 128), lambda b, i: (0, 0))],
        out_shape=[jax.ShapeDtypeStruct((bsz, s, d), F32),
                   jax.ShapeDtypeStruct((bsz * s, d + W_LANES), F32),
                   jax.ShapeDtypeStruct((bsz * nt, 1, tm), I32),
                   jax.ShapeDtypeStruct((CLASS_ROWS, 128), F32)],
        scratch_shapes=[pltpu.VMEM((tm + CONV_HALO, cc), F32),
                        pltpu.VMEM((7, tm + CONV_HALO - 8, cc), F32)],
        compiler_params=_params(2), name="hy_out_proj",
    )(u, u, att_t, x, g1, wtop, wbot, dww, dwb, lng, lnb, g, sc, sh, rwt, rb)


def _ssm_in_kernel(dest_ref, opaque_ref, ys_ref, xn_ref, g2_ref, sc_ref, sh_ref, g_ref, wz_ref, wx_ref, wdt_ref,
                   cw_ref, cb_ref, dtb_ref, x_ref, sz_ref, xbc_ref, dt_ref, buf_ref, sem, *cat_refs):
    tm = xn_ref.shape[1]
    ch = SSM_IN_COLS
    step = pl.program_id(0) * pl.num_programs(1) + pl.program_id(1)
    last = pl.num_programs(0) * pl.num_programs(1) - 1
    cur = step % 2
    ahead = jnp.minimum(step + 1, last)

    def gather(tile, slot, rows, after=0):
        for r in rows:
            row = dest_ref[tile * tm + r] + after
            pltpu.make_async_copy(ys_ref.at[pl.ds(row, 1), :], buf_ref.at[slot, pl.ds(r, 1), :],
                                  sem.at[slot]).start(priority=r % 2)

    def wait(slot):
        pltpu.make_async_copy(ys_ref.at[pl.ds(0, tm), :], buf_ref.at[slot], sem.at[slot]).wait()

    @pl.when(step == 0)
    def _():
        gather(0, 0, range(tm))

    @pl.when(pl.program_id(1) == 0)
    def _():
        for cat_ref in cat_refs:
            cat_ref[:SSM_HALO, :] = jnp.zeros((SSM_HALO, ch), F32)

    wait(cur)
    x = xn_ref[0] + g2_ref[0] * buf_ref[cur]
    x_ref[0] = x
    h = _mod_norm(x, g_ref[...], sc_ref[0], sh_ref[0])
    hb = h.astype(BF16)

    def conv_matmul(c):
        v = _dot(hb, wx_ref[:, c * ch:(c + 1) * ch])
        cat_refs[c][SSM_HALO:, :] = v
        return v

    def conv_act(c, _):
        cols = slice(c * ch, (c + 1) * ch)
        cat_ref = cat_refs[c]
        y = jnp.zeros((tm, ch), F32) + cb_ref[:, cols]
        for j in range(SSM_CONV):
            y = y + cw_ref[j:j + 1, cols] * cat_ref[pl.ds(SSM_HALO - SSM_CONV + 1 + j, tm), :]
        xbc_ref[0, :, cols] = _silu(y).astype(BF16)
        cat_ref[:SSM_HALO, :] = cat_ref[pl.ds(tm, SSM_HALO), :]

    def gate_matmul(c):
        return _dot(hb, wz_ref[:, c * ch:(c + 1) * ch])

    def gate_act(c, z):
        sz_ref[0, :, c * ch:(c + 1) * ch] = _silu(z).astype(BF16)

    stages = [(conv_matmul, conv_act, c) for c in range(wx_ref.shape[1] // ch)]
    stages += [(gate_matmul, gate_act, c) for c in range(wz_ref.shape[1] // ch)]
    shares = 8
    per_share = tm // shares
    gather(ahead, 1 - cur, range(per_share))
    pending = None
    for k, (matmul, act, c) in enumerate(stages):
        out = matmul(c)
        if k + 1 < shares:
            zero = out[0, 0].astype(I32) * opaque_ref[0]
            gather(ahead, 1 - cur, range((k + 1) * per_share, (k + 2) * per_share), zero)
        if pending is not None:
            pending[0](pending[1], pending[2])
        pending = (act, c, out)
    t = _dot(hb, wdt_ref[...]) + dtb_ref[...]
    pending[0](pending[1], pending[2])
    dt_ref[0] = jnp.maximum(t, 0.0) + jnp.log(1.0 + jnp.exp(-jnp.abs(t)))

    @pl.when(step == last)
    def _():
        wait(1 - cur)


def _ssm_in(dest, ys, x_new, g2, sc, sh, g, wz, wx, wdt, cw, cb, dtb, tm):
    bsz, s, d = x_new.shape
    full = lambda shape: pl.BlockSpec(shape, lambda b, i, *_: (0,) * len(shape), pipeline_mode=pl.Buffered(1))
    per_b = pl.BlockSpec((1, 1, d), lambda b, i, *_: (b, 0, 0))
    tile = lambda n: pl.BlockSpec((1, tm, n), lambda b, i, *_: (b, i, 0))
    return pl.pallas_call(
        _ssm_in_kernel,
        grid_spec=pltpu.PrefetchScalarGridSpec(
            num_scalar_prefetch=2, grid=(bsz, s // tm),
            in_specs=[pl.BlockSpec(memory_space=pl.ANY), tile(d), per_b, per_b, per_b, full(g.shape),
                      full(wz.shape), full(wx.shape), full(wdt.shape), full(cw.shape), full(cb.shape),
                      full(dtb.shape)],
            out_specs=[tile(d), tile(wz.shape[1]), tile(wx.shape[1]), tile(wdt.shape[1])],
            scratch_shapes=[pltpu.VMEM((2, tm, d), F32), pltpu.SemaphoreType.DMA((2,))]
            + [pltpu.VMEM((tm + SSM_HALO, SSM_IN_COLS), F32)] * (wx.shape[1] // SSM_IN_COLS)),
        out_shape=[jax.ShapeDtypeStruct((bsz, s, d), F32),
                   jax.ShapeDtypeStruct((bsz, s, wz.shape[1]), BF16),
                   jax.ShapeDtypeStruct((bsz, s, wx.shape[1]), BF16),
                   jax.ShapeDtypeStruct((bsz, s, wdt.shape[1]), F32)],
        compiler_params=_params(2), name="ssm_in_proj",
    )(dest, jnp.zeros((1,), I32), ys, x_new, g2, sc, sh, g, wz, wx, wdt, cw, cb, dtb)


def _ssd_kernel(xbc_ref, sz_ref, dt_ref, alog_ref, dsk_ref, ng_ref, wout_ref, e2_ref, x_ref, g1_ref,
                g_ref, sc_ref, sh_ref, rwt_ref, rb_ref,
                xn_ref, h2x_ref, cls_ref, cnt_ref, state_ref, macc_ref, y_ref):
    L = SSD_SUB
    inner = sz_ref.shape[2]
    gw = inner // SSM_GROUPS
    pw = 2 * SSM_HEAD_DIM

    @pl.when(pl.program_id(1) == 0)
    def _():
        state_ref[...] = jnp.zeros(state_ref.shape, F32)

    rows = x_ref.shape[1]
    subs = [slice(s * L, (s + 1) * L) for s in range(rows // L)]
    halves = [slice(s * SSM_CHUNK, (s + 1) * SSM_CHUNK) for s in range(rows // SSM_CHUNK)]
    brow = lax.broadcasted_iota(I32, (SSM_CHUNK, SSM_CHUNK), 0)
    bcol = lax.broadcasted_iota(I32, (SSM_CHUNK, SSM_CHUNK), 1)
    tri = jnp.where((brow // L == bcol // L) & (brow >= bcol), 1.0, 0.0).astype(BF16)
    lower = lax.broadcasted_iota(I32, (L, L), 0) >= lax.broadcasted_iota(I32, (L, L), 1)
    lo_half = lax.broadcasted_iota(I32, (L, pw), 1) < SSM_HEAD_DIM

    def expand(v):
        return _dot(jnp.concatenate(_split2(v), axis=1), e2_ref[...])

    macc_ref[...] = jnp.zeros(macc_ref.shape, F32)

    def group(g, carry):
        xcol = pl.ds(pl.multiple_of(g * gw, gw), gw)
        bcolumns = pl.ds(pl.multiple_of(inner + g * SSM_STATE, SSM_STATE), SSM_STATE)
        ccolumns = pl.ds(pl.multiple_of(inner + (SSM_GROUPS + g) * SSM_STATE, SSM_STATE), SSM_STATE)
        rate = -jnp.exp(alog_ref[g]) * LOG2E
        dt = dt_ref[0, :, pl.ds(pl.multiple_of(g * 128, 128), 128)]
        t3 = [_dot(tri, jnp.concatenate(_split3(dt[r, :] * rate), axis=1)) for r in halves]
        t3 = jnp.concatenate(t3, axis=0)
        a_cs = t3[:, :128] + t3[:, 128:256] + t3[:, 256:]
        xg = xbc_ref[0, :, xcol].astype(F32)
        xdt = xg * expand(dt)
        xdt_b = xdt.astype(BF16)
        bs = [xbc_ref[0, r, bcolumns] for r in subs]
        cs = [xbc_ref[0, r, ccolumns] for r in subs]
        cbm = [jnp.where(lower, _dot_nt(cm, bm), 0.0) for cm, bm in zip(cs, bs)]
        a_end = jnp.concatenate([jnp.broadcast_to(a_cs[r.stop - 1:r.stop, :], (L, 128)) for r in subs], axis=0)
        xw = (xdt * expand(jnp.exp2(a_end - a_cs))).astype(BF16)
        grow = expand(jnp.exp2(a_cs))
        keep = expand(jnp.exp2(jnp.concatenate([a_end[r.start:r.start + 16, :] for r in subs], axis=0)))
        st = state_ref[g]
        y_off = []
        for s, r in enumerate(subs):
            y_off.append(_dot(cs[s], st.astype(BF16)) * grow[r, :])
            st = st * keep[16 * s:16 * s + 1, :] + _dot(bs[s].astype(F32).T.astype(BF16), xw[r, :])
        state_ref[g] = st
        for s, r in enumerate(subs):
            a_sub = a_cs[r, :]
            a_sub_t = a_sub.T
            for q in range(gw // pw):
                x2 = xdt_b[r, q * pw:(q + 1) * pw]
                yp = y_off[s][:, q * pw:(q + 1) * pw]
                for e in range(2):
                    hd = 2 * q + e
                    dec = jnp.exp2(jnp.minimum(a_sub[:, hd:hd + 1] - a_sub_t[hd:hd + 1, :], 0.0))
                    xm = jnp.where(lo_half if e == 0 else jnp.logical_not(lo_half), x2, jnp.zeros_like(x2))
                    yp = yp + _dot((cbm[s] * dec).astype(BF16), xm)
                y_ref[r, q * pw:(q + 1) * pw] = yp
        gt = (y_ref[...] + xg * dsk_ref[g]) * sz_ref[0, :, xcol].astype(F32)
        ms = jnp.mean(gt * gt, axis=-1, keepdims=True)
        gn = gt * lax.rsqrt(ms + NORM_EPS) * ng_ref[g]
        macc_ref[...] += _dot(gn.astype(BF16), wout_ref[xcol, :])
        return carry

    lax.fori_loop(0, SSM_GROUPS, group, 0)
    x_new = x_ref[0] + g1_ref[0] * macc_ref[...]
    xn_ref[0] = x_new
    _post_mixer(x_new, g_ref, sc_ref, sh_ref, rwt_ref, rb_ref, h2x_ref, cls_ref, cnt_ref)


def _ssd(xbc, sz, dt, alog, dskip, ng, wout, e2, x, g1, g, sc, sh, rwt, rb):
    bsz, s, d = x.shape
    L = SSD_BLOCK
    nc = s // L
    gw = sz.shape[2] // SSM_GROUPS
    full = lambda shape: pl.BlockSpec(shape, lambda b, c: (0,) * len(shape))
    per_b = pl.BlockSpec((1, 1, d), lambda b, c: (b, 0, 0))
    tile = lambda n: pl.BlockSpec((1, L, n), lambda b, c: (b, c, 0))
    return pl.pallas_call(
        _ssd_kernel,
        grid=(bsz, nc),
        in_specs=[tile(xbc.shape[2]), tile(sz.shape[2]), tile(dt.shape[2]),
                  full(alog.shape), full(dskip.shape), full(ng.shape), full(wout.shape), full(e2.shape),
                  tile(d), per_b, full(g.shape), per_b, per_b, full(rwt.shape), full(rb.shape)],
        out_specs=[tile(d),
                   pl.BlockSpec((L, d + W_LANES), lambda b, c: (b * nc + c, 0)),
                   pl.BlockSpec((1, 1, L), lambda b, c: (b * nc + c, 0, 0)),
                   pl.BlockSpec((CLASS_ROWS, 128), lambda b, c: (0, 0))],
        out_shape=[jax.ShapeDtypeStruct((bsz, s, d), F32),
                   jax.ShapeDtypeStruct((bsz * s, d + W_LANES), F32),
                   jax.ShapeDtypeStruct((bsz * nc, 1, L), I32),
                   jax.ShapeDtypeStruct((CLASS_ROWS, 128), F32)],
        scratch_shapes=[pltpu.VMEM((SSM_GROUPS, SSM_STATE, gw), F32),
                        pltpu.VMEM((L, d), F32),
                        pltpu.VMEM((L, gw), F32)],
        compiler_params=_params(2), name="ssd_out_proj",
    )(xbc, sz, dt, alog, dskip, ng, wout, e2, x, g1, g, sc, sh, rwt, rb)


def _dest_kernel(cls_ref, start_ref, dest_ref, run_ref):
    @pl.when(pl.program_id(0) == 0)
    def _():
        run_ref[...] = start_ref[...]

    tr = cls_ref.shape[2]
    upper = lax.broadcasted_iota(I32, (tr, tr), 0) <= lax.broadcasted_iota(I32, (tr, tr), 1)
    upper = jnp.where(upper, 1.0, 0.0).astype(BF16)
    for k in range(cls_ref.shape[0]):
        onehot = lax.broadcasted_iota(I32, (CLASS_ROWS, tr), 0) == cls_ref[k]
        oh = jnp.where(onehot, 1.0, 0.0)
        prefix = _dot(oh.astype(BF16), upper)
        dest = jnp.sum(oh * (prefix - 1.0 + run_ref[...]), axis=0, keepdims=True)
        dest_ref[k] = dest.astype(I32)
        run_ref[...] += jnp.sum(oh, axis=1, keepdims=True)


def _moe_plan(cls, cnt, tr):
    nt = cls.shape[0]
    t = nt * tr
    reps = 4 if nt % 4 == 0 else 1
    padded = jnp.ceil(cnt[:, 0] / MOE_TILE) * MOE_TILE
    end = jnp.cumsum(padded)
    start = end - padded
    dest = pl.pallas_call(
        _dest_kernel, grid=(nt // reps,),
        in_specs=[pl.BlockSpec((reps, 1, tr), lambda i: (i, 0, 0)),
                  pl.BlockSpec((CLASS_ROWS, 1), lambda i: (0, 0))],
        out_specs=pl.BlockSpec((reps, 1, tr), lambda i: (i, 0, 0)),
        out_shape=jax.ShapeDtypeStruct((nt, 1, tr), I32),
        scratch_shapes=[pltpu.VMEM((CLASS_ROWS, 1), F32)],
        compiler_params=_params(1), name="moe_dest",
    )(cls, start.reshape(CLASS_ROWS, 1))
    n_tiles = t // MOE_TILE + N_CLASSES
    tile_row = jnp.arange(n_tiles, dtype=F32) * MOE_TILE
    total = end[N_CLASSES - 1]
    valid = tile_row < total
    tcls = jnp.sum((tile_row[:, None] >= end[None, :N_CLASSES]).astype(I32), axis=1)
    last = jnp.sum((total - MOE_TILE >= end[:N_CLASSES]).astype(I32))
    tcls = jnp.where(valid, tcls, last)
    grp = tcls // N_PAIRS
    pair = tcls % N_PAIRS
    lo = jnp.where(pair < 3, 0, jnp.where(pair < 5, 1, 2))
    hi = jnp.where(pair < 3, pair + 1, jnp.where(pair < 5, pair - 1, 3))
    meta = jnp.stack([grp * EXPERTS_PER_GROUP + lo, grp * EXPERTS_PER_GROUP + hi, valid.astype(I32),
                      jnp.zeros_like(tcls)]).astype(I32)
    return dest.reshape(t), meta


def _invert_kernel(dest_ref, zeros_ref, src_ref, sem):
    clear = pltpu.make_async_copy(zeros_ref, src_ref, sem)
    clear.start()
    clear.wait()

    def put(t0, carry):
        for u in range(INVERT_UNROLL):
            t = t0 * INVERT_UNROLL + u
            src_ref[dest_ref[t]] = t
        return carry

    lax.fori_loop(0, dest_ref.shape[0] // INVERT_UNROLL, put, 0)


def _invert(dest, n_slots):
    smem = pl.BlockSpec(memory_space=pltpu.SMEM)
    return pl.pallas_call(
        _invert_kernel, in_specs=[smem, pl.BlockSpec(memory_space=pl.ANY)], out_specs=smem,
        out_shape=jax.ShapeDtypeStruct((n_slots,), I32),
        scratch_shapes=[pltpu.SemaphoreType.DMA(())], name="moe_invert",
    )(dest, jnp.zeros((n_slots,), I32))


def _expert_kernel(meta_ref, src_ref, h2x_ref, wg_a_ref, wu_a_ref, wd_a_ref, wg_b_ref, wu_b_ref, wd_b_ref, y_ref,
                   buf0_ref, buf1_ref, sem):
    i = pl.program_id(0)
    d = y_ref.shape[1]
    bufs = (buf0_ref, buf1_ref)

    def gather(tile, slot, rows=range(MOE_TILE), after=0):
        for r in rows:
            tok = src_ref[tile * MOE_TILE + r] + after
            pltpu.make_async_copy(h2x_ref.at[pl.ds(tok, 1), :], bufs[slot].at[pl.ds(r, 1), :],
                                  sem.at[slot]).start(priority=r % 2)

    def wait(slot):
        pltpu.make_async_copy(h2x_ref.at[pl.ds(0, MOE_TILE), :], bufs[slot], sem.at[slot]).wait()

    @pl.when(i == 0)
    def _():
        gather(0, 0)

    valid = meta_ref[2, i] != 0
    cw = EXPERT_COLS
    for cur in range(2):
        @pl.when(valid & (i % 2 == cur))
        def _(cur=cur):
            wait(cur)
            xb = bufs[cur][:, :d].astype(BF16)
            wts = bufs[cur][:, d:]
            n_parts = (2 * wg_a_ref.shape[3] + d) // cw
            per_part = MOE_TILE // n_parts
            gather(i + 1, 1 - cur, range(per_part))
            part = [1]

            def pace(result):
                if part[0] == n_parts:
                    return
                zero = result[0, 0].astype(I32) * meta_ref[3, i]
                gather(i + 1, 1 - cur, range(part[0] * per_part, (part[0] + 1) * per_part), zero)
                part[0] += 1

            def dot_cols(a, w_ref):
                chunks = []
                for c in range(w_ref.shape[3] // cw):
                    chunks.append(_dot(a, w_ref[0, 0, :, c * cw:(c + 1) * cw]))
                    pace(chunks[-1])
                return jnp.concatenate(chunks, axis=1)

            y = jnp.zeros(y_ref.shape, F32)
            for e, (wg_ref, wu_ref, wd_ref) in enumerate(((wg_a_ref, wu_a_ref, wd_a_ref),
                                                          (wg_b_ref, wu_b_ref, wd_b_ref))):
                act = _silu(dot_cols(xb, wg_ref)) * dot_cols(xb, wu_ref)
                y = y + wts[:, e:e + 1] * dot_cols(act.astype(BF16), wd_ref)
            y_ref[...] = y

        @pl.when(jnp.logical_not(valid) & (meta_ref[2, jnp.maximum(i - 1, 0)] != 0) & (i % 2 == cur))
        def _(cur=cur):
            wait(cur)

    @pl.when(jnp.logical_not(valid))
    def _():
        y_ref[...] = jnp.zeros(y_ref.shape, F32)


def _experts(meta, src, h2x, layer, wg, wu, wd):
    n_slots = src.shape[0]
    w = h2x.shape[1]
    d = w - W_LANES
    n_tiles = n_slots // MOE_TILE
    expert = lambda arr, which: pl.BlockSpec((1, 1) + arr.shape[2:], lambda i, m, s: (layer, m[which, i], 0, 0))
    return pl.pallas_call(
        _expert_kernel,
        grid_spec=pltpu.PrefetchScalarGridSpec(
            num_scalar_prefetch=2, grid=(n_tiles,),
            in_specs=[pl.BlockSpec(memory_space=pl.ANY),
                      expert(wg, 0), expert(wu, 0), expert(wd, 0), expert(wg, 1), expert(wu, 1), expert(wd, 1)],
            out_specs=pl.BlockSpec((MOE_TILE, d), lambda i, m, s: (i, 0)),
            scratch_shapes=[pltpu.VMEM((MOE_TILE, w), F32), pltpu.VMEM((MOE_TILE, w), F32),
                            pltpu.SemaphoreType.DMA((2,))]),
        out_shape=jax.ShapeDtypeStruct((n_slots, d), F32),
        compiler_params=_params(1), name="moe_experts",
    )(meta, src, h2x, wg, wu, wd, wg, wu, wd)


def _combine_kernel(dest_ref, ys_ref, x_ref, g2_ref, o_ref, buf0_ref, buf1_ref, sem):
    tm = x_ref.shape[1]
    step = pl.program_id(0) * pl.num_programs(1) + pl.program_id(1)
    n_steps = pl.num_programs(0) * pl.num_programs(1)
    bufs = (buf0_ref, buf1_ref)

    def gather(st, slot):
        for r in range(tm):
            pltpu.make_async_copy(ys_ref.at[pl.ds(dest_ref[st * tm + r], 1), :], bufs[slot].at[pl.ds(r, 1), :],
                                  sem.at[slot]).start(priority=r % 2)

    @pl.when(step == 0)
    def _():
        gather(0, 0)

    for cur in range(2):
        @pl.when(step % 2 == cur)
        def _(cur=cur):
            @pl.when(step + 1 < n_steps)
            def _():
                gather(step + 1, 1 - cur)

            pltpu.make_async_copy(ys_ref.at[pl.ds(0, tm), :], bufs[cur], sem.at[cur]).wait()
            o_ref[0] = x_ref[0] + g2_ref[0] * bufs[cur][...]


def _combine(dest, ys, x, g2, tm):
    bsz, s, d = x.shape
    return pl.pallas_call(
        _combine_kernel,
        grid_spec=pltpu.PrefetchScalarGridSpec(
            num_scalar_prefetch=1, grid=(bsz, s // tm),
            in_specs=[pl.BlockSpec(memory_space=pl.ANY),
                      pl.BlockSpec((1, tm, d), lambda b, i, dd: (b, i, 0)),
                      pl.BlockSpec((1, 1, d), lambda b, i, dd: (b, 0, 0))],
            out_specs=pl.BlockSpec((1, tm, d), lambda b, i, dd: (b, i, 0)),
            scratch_shapes=[pltpu.VMEM((tm, d), F32), pltpu.VMEM((tm, d), F32), pltpu.SemaphoreType.DMA((2,))]),
        out_shape=jax.ShapeDtypeStruct((bsz, s, d), F32),
        compiler_params=_params(2), name="moe_combine",
    )(dest, ys, x, g2)


def _moe(h2x, cls, cnt, layer, wg, wu, wd):
    t = h2x.shape[0]
    tr = min(512, t)
    dest, meta = _moe_plan(cls.reshape(t // tr, 1, tr), cnt, tr)
    n_slots = t + N_CLASSES * MOE_TILE
    return dest, _experts(meta, _invert(dest, n_slots), h2x, layer, wg, wu, wd)


def kernel(x, c, ada_w, ada_b, norm_mix, norm_ffn, hy_w_in, hy_q_norm, hy_k_norm, hy_dw_w, hy_dw_b, hy_ln_g, hy_ln_b, hy_w_out, ssm_w_in, ssm_conv_w, ssm_conv_b, ssm_dt_bias, ssm_a_log, ssm_d, ssm_norm, ssm_w_out, router_w, router_bias, exp_w_gate, exp_w_up, exp_w_down):
    bsz, s, d = x.shape
    depth = ada_w.shape[0]
    assert s % 512 == 0 and bsz <= 8
    c8 = jnp.zeros((8, d), F32).at[:bsz].set(c)
    mod = _ada(c8, ada_w, ada_b)
    rwt = router_w.T
    rb = router_bias.reshape(N_EXPERTS, 1)
    lane_head = jnp.arange(ATT_WIDTH) // HEAD_DIM
    pmat = ((lane_head[:, None] == lane_head[None, :]).astype(F32) / HEAD_DIM).astype(BF16)

    expert_w = tuple(w.astype(BF16) for w in (exp_w_gate, exp_w_up, exp_w_down))
    moe = None
    for layer in range(depth):
        sh1, sc1, g1, sh2, sc2, g2 = (mod[layer, :bsz, i * d:(i + 1) * d].reshape(bsz, 1, d) for i in range(N_MOD))
        gm = norm_mix[layer].reshape(1, d)
        gf = norm_ffn[layer].reshape(1, d)
        j = layer // 2
        if layer % 2 == 0:
            if moe is not None:
                x = _combine(*moe, min(256, s))
            w_in = hy_w_in[j].astype(BF16)
            aw = ATT_WIDTH
            k, qt, vt, u, kmean = _hy_in(
                x, sc1, sh1, gm, w_in[:, aw:2 * aw], w_in[:, 3 * aw:], w_in[:, :aw].T, w_in[:, 2 * aw:3 * aw].T,
                (jnp.tile(hy_q_norm[j], ATT_HEADS) * (HEAD_DIM ** -0.5 * LOG2E)).reshape(aw, 1),
                jnp.tile(hy_k_norm[j], ATT_HEADS).reshape(1, aw), pmat, 512)
            bias = _gate(kmean.reshape(bsz, s // MOBA_BLOCK, aw), qt, min(1024, s))
            att_t = _attn(qt, k, vt, bias, 8)
            w_out = hy_w_out[j].astype(BF16)
            dww = jnp.zeros((CONV_HALO, hy_dw_w.shape[2]), F32).at[:CONV_WIDTH].set(hy_dw_w[j])
            x_new, h2x, cls, cnt = _hy_out(
                u, att_t, x, g1, w_out[:aw], w_out[aw:], dww, hy_dw_b[j].reshape(1, -1),
                hy_ln_g[j].reshape(1, -1), hy_ln_b[j].reshape(1, -1), gf, sc2, sh2, rwt, rb, 512)
        else:
            w_in = ssm_w_in[j]
            inner = ssm_norm.shape[1]
            heads = ssm_a_log.shape[1]
            hpg = heads // SSM_GROUPS
            conv_dim = ssm_conv_w.shape[2]

            def by_group(v):
                v = v.reshape(v.shape[:-1] + (SSM_GROUPS, hpg))
                pad = [(0, 0)] * (v.ndim - 1) + [(0, 128 - hpg)]
                return jnp.pad(v, pad).reshape(v.shape[:-2] + (SSM_GROUPS * 128,))

            x, sz, xbc, dt = _ssm_in(
                *moe, sc1, sh1, gm, w_in[:, :inner].astype(BF16), w_in[:, inner:inner + conv_dim].astype(BF16),
                by_group(w_in[:, inner + conv_dim:]).astype(BF16), ssm_conv_w[j], ssm_conv_b[j].reshape(1, -1),
                by_group(ssm_dt_bias[j]).reshape(1, -1), 512)
            chan_head = jnp.arange(inner // SSM_GROUPS) // SSM_HEAD_DIM
            e2 = (jnp.arange(256)[:, None] % 128 == chan_head[None, :]).astype(BF16)
            x_new, h2x, cls, cnt = _ssd(
                xbc, sz, dt, by_group(ssm_a_log[j]).reshape(SSM_GROUPS, 1, 128),
                jnp.repeat(ssm_d[j], SSM_HEAD_DIM).reshape(SSM_GROUPS, 1, -1),
                ssm_norm[j].reshape(SSM_GROUPS, 1, -1), ssm_w_out[j].astype(BF16), e2, x, g1, gf, sc2, sh2, rwt, rb)
        moe = _moe(h2x, cls, cnt, layer, *expert_w) + (x_new, g2)
    return _combine(*moe, min(256, s))
```

```python
import functools

import jax
import jax.numpy as jnp
from jax import lax
from jax.experimental import pallas as pl
from jax.experimental.pallas import tpu as pltpu

F32 = jnp.float32
BF16 = jnp.bfloat16
I32 = jnp.int32

NORM_EPS = 1e-6
N_MOD = 6
HEAD_DIM = 64
ATT_HEADS = 8
ATT_WIDTH = ATT_HEADS * HEAD_DIM
MOBA_BLOCK = 256
MOBA_TOPK = 3
CONV_WIDTH = 31
CONV_HALO = 32
SSM_HEAD_DIM = 64
SSM_GROUPS = 4
SSM_STATE = 128
SSM_CONV = 4
SSM_CHUNK = 256
SSM_HALO = 8
SSD_SUB = 128
SSD_BLOCK = 512
SSM_IN_COLS = 512
N_EXPERTS = 16
N_EXPERT_GROUPS = 4
EXPERTS_PER_GROUP = 4
N_PAIRS = 6
N_CLASSES = N_EXPERT_GROUPS * N_PAIRS
CLASS_ROWS = 32
MOE_TILE = 256
EXPERT_COLS = 256
W_LANES = 128
INVERT_UNROLL = 32
LOG2E = 1.4426950408889634
NEG = -1e30
VMEM_LIMIT = 56 * 1024 * 1024


def _params(n_axes):
    return pltpu.CompilerParams(dimension_semantics=("arbitrary",) * n_axes,
                                vmem_limit_bytes=VMEM_LIMIT)


def _dot(a, b):
    return jnp.dot(a, b, preferred_element_type=F32)


def _dot_nt(a, b):
    return lax.dot_general(a, b, (((1,), (1,)), ((), ())), preferred_element_type=F32)


def _split2(x):
    hi = x.astype(BF16)
    lo = (x - hi.astype(F32)).astype(BF16)
    return hi, lo


def _split3(x):
    a = x.astype(BF16)
    r = x - a.astype(F32)
    b = r.astype(BF16)
    c = (r - b.astype(F32)).astype(BF16)
    return a, b, c


def _silu(x):
    return x * jax.nn.sigmoid(x)


def _mod_norm(x, g, sc, sh):
    ms = jnp.mean(x * x, axis=-1, keepdims=True)
    return x * lax.rsqrt(ms + NORM_EPS) * g * (1.0 + sc) + sh


def _ada_kernel(c_ref, w_ref, b_ref, o_ref):
    cond = _silu(c_ref[...])
    ch, cl = _split2(cond)
    wh, wl = _split2(w_ref[0])
    o_ref[0] = _dot(ch, wh) + _dot(ch, wl) + _dot(cl, wh) + b_ref[0]


def _ada(c8, ada_w, ada_b):
    depth, d, n = ada_w.shape
    tn = 1536
    return pl.pallas_call(
        _ada_kernel,
        grid=(depth, n // tn),
        in_specs=[pl.BlockSpec((8, d), lambda l, j: (0, 0)),
                  pl.BlockSpec((1, d, tn), lambda l, j: (l, 0, j)),
                  pl.BlockSpec((1, 1, tn), lambda l, j: (l, 0, j))],
        out_specs=pl.BlockSpec((1, 8, tn), lambda l, j: (l, 0, j)),
        out_shape=jax.ShapeDtypeStruct((depth, 8, n), F32),
        compiler_params=_params(2), name="ada_mod",
    )(c8, ada_w, ada_b.reshape(depth, 1, n))


def _hy_in_kernel(x_ref, sc_ref, sh_ref, g_ref, wk_ref, wu_ref, wqt_ref, wvt_ref, qg_ref, kg_ref, p_ref,
                  k_ref, qt_ref, vt_ref, u_ref, km_ref):
    tm = x_ref.shape[1]
    h = _mod_norm(x_ref[0], g_ref[...], sc_ref[0], sh_ref[0])
    hb = h.astype(BF16)
    k = _dot(hb, wk_ref[...])
    khi, klo = _split2(k * k)
    ms = _dot(khi, p_ref[...]) + _dot(klo, p_ref[...])
    kn = k * lax.rsqrt(ms + NORM_EPS) * kg_ref[...]
    k_ref[0] = kn.astype(BF16)
    km_ref[0, 0] = jnp.mean(kn.reshape(tm // MOBA_BLOCK, MOBA_BLOCK, ATT_WIDTH), axis=1)
    qt = _dot_nt(wqt_ref[...], hb).reshape(ATT_HEADS, HEAD_DIM, tm)
    qms = jnp.mean(qt * qt, axis=1, keepdims=True)
    qn = (qt * lax.rsqrt(qms + NORM_EPS)).reshape(ATT_WIDTH, tm) * qg_ref[...]
    qt_ref[0] = qn.astype(BF16)
    vt_ref[0] = _dot_nt(wvt_ref[...], hb).astype(BF16)
    ag = _dot(hb, wu_ref[...])
    half = ag.shape[1] // 2
    u_ref[0] = (ag[:, :half] * jax.nn.sigmoid(ag[:, half:])).astype(BF16)


def _hy_in(x, sc, sh, g, wk, wu, wqt, wvt, qg, kg, pmat, tm):
    bsz, s, d = x.shape
    cc = wu.shape[1] // 2
    full = lambda shape: pl.BlockSpec(shape, lambda b, i: (0,) * len(shape))
    return pl.pallas_call(
        _hy_in_kernel,
        grid=(bsz, s // tm),
        in_specs=[pl.BlockSpec((1, tm, d), lambda b, i: (b, i, 0)),
                  pl.BlockSpec((1, 1, d), lambda b, i: (b, 0, 0)),
                  pl.BlockSpec((1, 1, d), lambda b, i: (b, 0, 0)),
                  full((1, d)), full(wk.shape), full(wu.shape), full(wqt.shape), full(wvt.shape),
                  full(qg.shape), full(kg.shape), full(pmat.shape)],
        out_specs=[pl.BlockSpec((1, tm, ATT_WIDTH), lambda b, i: (b, i, 0)),
                   pl.BlockSpec((1, ATT_WIDTH, tm), lambda b, i: (b, 0, i)),
                   pl.BlockSpec((1, ATT_WIDTH, tm), lambda b, i: (b, 0, i)),
                   pl.BlockSpec((1, tm, cc), lambda b, i: (b, i, 0)),
                   pl.BlockSpec((1, 1, tm // MOBA_BLOCK, ATT_WIDTH), lambda b, i: (b, i, 0, 0))],
        out_shape=[jax.ShapeDtypeStruct((bsz, s, ATT_WIDTH), BF16),
                   jax.ShapeDtypeStruct((bsz, ATT_WIDTH, s), BF16),
                   jax.ShapeDtypeStruct((bsz, ATT_WIDTH, s), BF16),
                   jax.ShapeDtypeStruct((bsz, s, cc), BF16),
                   jax.ShapeDtypeStruct((bsz, s // tm, tm // MOBA_BLOCK, ATT_WIDTH), F32)],
        compiler_params=_params(2), name="hy_in_proj",
    )(x, sc, sh, g, wk, wu, wqt, wvt, qg, kg, pmat)


def _gate_kernel(km_ref, qt_ref, o_ref):
    nb = km_ref.shape[1]
    ts = qt_ref.shape[2]
    rows = ATT_HEADS * nb
    km = km_ref[0]
    kmt = jnp.broadcast_to(km[None], (ATT_HEADS, nb, ATT_WIDTH)).reshape(rows, ATT_WIDTH)
    rh = lax.broadcasted_iota(I32, (rows, ATT_WIDTH), 0) // nb
    ch = lax.broadcasted_iota(I32, (rows, ATT_WIDTH), 1) // HEAD_DIM
    kbd = jnp.where(rh == ch, kmt, 0.0)
    khi, klo = _split2(kbd)
    qt = qt_ref[0]
    gate = (_dot(khi, qt) + _dot(klo, qt)).reshape(ATT_HEADS, nb, ts)
    blk = lax.broadcasted_iota(I32, (ATT_HEADS, nb, ts), 1)
    qblk = (pl.program_id(1) * ts + lax.broadcasted_iota(I32, (ATT_HEADS, nb, ts), 2)) // MOBA_BLOCK
    g = jnp.where(blk < qblk, gate, -jnp.inf)
    sel = jnp.zeros(g.shape, jnp.bool_)
    for _ in range(MOBA_TOPK):
        m = jnp.max(g, axis=1, keepdims=True)
        first = jnp.min(jnp.where((g == m) & (m > -jnp.inf), blk, nb), axis=1, keepdims=True)
        pick = blk == first
        sel = sel | pick
        g = jnp.where(pick, -jnp.inf, g)
    o_ref[0] = jnp.where(sel, 0.0, NEG).reshape(rows, ts)


def _gate(kmean, qt, ts):
    bsz, nb, _ = kmean.shape
    s = qt.shape[2]
    rows = ATT_HEADS * nb
    return pl.pallas_call(
        _gate_kernel,
        grid=(bsz, s // ts),
        in_specs=[pl.BlockSpec((1, nb, ATT_WIDTH), lambda b, i: (b, 0, 0)),
                  pl.BlockSpec((1, ATT_WIDTH, ts), lambda b, i: (b, 0, i))],
        out_specs=pl.BlockSpec((1, rows, ts), lambda b, i: (b, 0, i)),
        out_shape=jax.ShapeDtypeStruct((bsz, rows, s), F32),
        compiler_params=_params(2), name="moba_gate",
    )(kmean, qt)


def _attn_kernel(qt_ref, k_ref, vt_ref, bias_ref, o_ref, qa_ref, acc_ref, m_ref, *, nb, hp, lag, kb):
    blk = MOBA_BLOCK
    pw = 2 * HEAD_DIM
    va = HEAD_DIM + 16
    i = pl.program_id(2)
    row = lax.broadcasted_iota(I32, (pw, blk), 0)
    for h in range(hp):
        qp = qt_ref[0, (h // 2) * pw:(h // 2 + 1) * pw, :]
        keep = (row < HEAD_DIM) if h % 2 == 0 else (row >= HEAD_DIM)
        qa_ref[h, :pw, :] = jnp.where(keep, qp, jnp.zeros_like(qp))
        qa_ref[h, pw:pw + nb, :] = bias_ref[0, h * nb:(h + 1) * nb, :].astype(BF16)
        qa_ref[h, pw + nb:, :] = jnp.zeros((blk - pw - nb, blk), BF16)
        m_ref[h] = jnp.full((1, blk), NEG, F32)
        acc_ref[h] = jnp.zeros((va, blk), F32)
    causal = lax.broadcasted_iota(I32, (blk, blk), 0) <= lax.broadcasted_iota(I32, (blk, blk), 1)
    lane = lax.broadcasted_iota(I32, (blk, pw), 1)
    ones = jnp.ones((va - HEAD_DIM, blk), BF16)

    def step(blocks):
        units = [(b, h) for b in range(len(blocks)) for h in range(hp)]
        offs = [pl.multiple_of(j * blk, blk) for j, _, _ in blocks]

        def scores(u):
            b, h = units[u]
            pp = h // 2
            ka = jnp.concatenate([k_ref[0, pl.ds(offs[b], blk), pp * pw:(pp + 1) * pw], blocks[b][1]], axis=1)
            return blocks[b][2](_dot(ka, qa_ref[h]))

        def softmax(u, s):
            h = units[u][1]
            m_old = m_ref[h]
            m_new = jnp.maximum(m_old, jnp.max(s, axis=0, keepdims=True))
            m_ref[h] = m_new
            return jnp.exp2(s - m_new).astype(BF16), jnp.exp2(m_old - m_new)

        def accumulate(u, p, alpha):
            b, h = units[u]
            vj = jnp.concatenate([vt_ref[0, h * HEAD_DIM:(h + 1) * HEAD_DIM, pl.ds(offs[b], blk)], ones], axis=0)
            acc_ref[h] = alpha * acc_ref[h] + _dot(vj, p)

        s, pa = {}, {}
        for t in range(len(units) + lag):
            if t < len(units):
                s[t] = scores(t)
            if 1 <= t <= len(units):
                pa[t - 1] = softmax(t - 1, s.pop(t - 1))
            if t >= lag:
                accumulate(t - lag, *pa.pop(t - lag))

    def past(j):
        return (j, jnp.where(lane == j, 1.0, 0.0).astype(BF16), lambda s: s)

    own = (i, jnp.zeros((blk, pw), BF16), lambda s: jnp.where(causal, s, NEG))

    @pl.when(i == 0)
    def _():
        step([own])

    @pl.when(i > 0)
    def _():
        step([own, past(0)])

    n_rest = jnp.maximum(i - 1, 0)
    rem = n_rest % kb
    base = jnp.int32(1)
    size = 1
    while size < kb:
        take = (rem & size) != 0

        @pl.when(take)
        def _(base=base, size=size):
            step([past(base + t) for t in range(size)])

        base = base + jnp.where(take, size, 0)
        size *= 2

    def body(r, carry):
        step([past(1 + rem + kb * r + t) for t in range(kb)])
        return carry

    lax.fori_loop(0, n_rest // kb, body, 0)
    for h in range(hp):
        acc = acc_ref[h]
        o_ref[0, h * HEAD_DIM:(h + 1) * HEAD_DIM, :] = (acc[:HEAD_DIM] / acc[HEAD_DIM:HEAD_DIM + 1]).astype(BF16)


def _attn(qt, k, vt, bias, hp):
    bsz, s, _ = k.shape
    nb = s // MOBA_BLOCK
    assert nb <= MOBA_BLOCK - 2 * HEAD_DIM and nb % 16 == 0
    hw = hp * HEAD_DIM
    return pl.pallas_call(
        functools.partial(_attn_kernel, nb=nb, hp=hp, lag=6, kb=4),
        grid=(bsz, ATT_HEADS // hp, nb),
        in_specs=[pl.BlockSpec((1, hw, MOBA_BLOCK), lambda b, p, i: (b, p, i)),
                  pl.BlockSpec((1, s, hw), lambda b, p, i: (b, 0, p)),
                  pl.BlockSpec((1, hw, s), lambda b, p, i: (b, p, 0)),
                  pl.BlockSpec((1, hp * nb, MOBA_BLOCK), lambda b, p, i: (b, p, i))],
        out_specs=pl.BlockSpec((1, hw, MOBA_BLOCK), lambda b, p, i: (b, p, i)),
        out_shape=jax.ShapeDtypeStruct((bsz, ATT_WIDTH, s), BF16),
        scratch_shapes=[pltpu.VMEM((hp, MOBA_BLOCK, MOBA_BLOCK), BF16),
                        pltpu.VMEM((hp, HEAD_DIM + 16, MOBA_BLOCK), F32),
                        pltpu.VMEM((hp, 1, MOBA_BLOCK), F32)],
        compiler_params=_params(3), name="moba_attn",
    )(qt, k, vt, bias)


def _route(logit_t, rb):
    aff = jax.nn.sigmoid(logit_t)
    score = aff + rb
    s = [score[e:e + 1, :] for e in range(N_EXPERTS)]
    a = [aff[e:e + 1, :] for e in range(N_EXPERTS)]
    n = EXPERTS_PER_GROUP

    def top2_sum(v):
        best = None
        for x in range(n):
            for y in range(x + 1, n):
                t = v[x] + v[y]
                best = t if best is None else jnp.maximum(best, t)
        return best

    def first_argmax(v):
        idx = jnp.zeros(v[0].shape, I32)
        cur = v[0]
        for x in range(1, len(v)):
            better = v[x] > cur
            idx = jnp.where(better, x, idx)
            cur = jnp.where(better, v[x], cur)
        return idx

    grp = first_argmax([top2_sum(s[n * g:n * g + n]) for g in range(N_EXPERT_GROUPS)])

    def in_group(v, x):
        out = v[(N_EXPERT_GROUPS - 1) * n + x]
        for g in range(N_EXPERT_GROUPS - 2, -1, -1):
            out = jnp.where(grp == g, v[n * g + x], out)
        return out

    sg = [in_group(s, x) for x in range(n)]
    ag = [in_group(a, x) for x in range(n)]
    l1 = first_argmax(sg)
    l2 = first_argmax([jnp.where(l1 == x, -jnp.inf, sg[x]) for x in range(n)])
    lo = jnp.minimum(l1, l2)
    hi = jnp.maximum(l1, l2)
    pair = jnp.where(lo == 0, hi - 1, jnp.where(lo == 1, hi + 1, 5))
    a_lo = jnp.zeros_like(ag[0])
    a_hi = jnp.zeros_like(ag[0])
    for x in range(n):
        a_lo = jnp.where(lo == x, ag[x], a_lo)
        a_hi = jnp.where(hi == x, ag[x], a_hi)
    tot = a_lo + a_hi
    return grp * N_PAIRS + pair, a_lo / tot, a_hi / tot


def _post_mixer(x_new, g_ref, sc_ref, sh_ref, rwt_ref, rb_ref, h2x_ref, cls_ref, cnt_ref):
    tm, d = x_new.shape
    h2 = _mod_norm(x_new, g_ref[...], sc_ref[0], sh_ref[0])
    hh, hl = _split2(h2)
    rh, rl = _split2(rwt_ref[...])
    logit_t = _dot_nt(rh, hh) + _dot_nt(rh, hl) + _dot_nt(rl, hh)
    cls, w_lo, w_hi = _route(logit_t, rb_ref[...])
    cls_ref[0] = cls

    @pl.when((pl.program_id(0) == 0) & (pl.program_id(1) == 0))
    def _():
        cnt_ref[...] = jnp.zeros(cnt_ref.shape, F32)

    onehot = lax.broadcasted_iota(I32, (CLASS_ROWS, tm), 0) == cls
    cnt_ref[...] += jnp.sum(jnp.where(onehot, 1.0, 0.0), axis=1, keepdims=True)
    wrow = lax.broadcasted_iota(I32, (W_LANES, tm), 0)
    wt = jnp.where(wrow == 0, w_lo, jnp.where(wrow == 1, w_hi, 0.0))
    h2x_ref[:, :d] = h2
    h2x_ref[:, d:] = wt.T


def _hy_out_kernel(u_ref, up_ref, at_ref, x_ref, g1_ref, wtop_ref, wbot_ref, dww_ref, dwb_ref, lng_ref, lnb_ref,
                   g_ref, sc_ref, sh_ref, rwt_ref, rb_ref,
                   xn_ref, h2x_ref, cls_ref, cnt_ref, cat_ref, shifted_ref):
    tm = u_ref.shape[1]
    m_att = _dot(at_ref[0].astype(F32).T.astype(BF16), wtop_ref[...])
    prev = up_ref[0].astype(F32)
    cat_ref[:CONV_HALO, :] = jnp.where(pl.program_id(1) == 0, 0.0, prev)
    cat_ref[CONV_HALO:, :] = u_ref[0].astype(F32)
    span = tm + CONV_HALO - 8
    for r in range(1, 8):
        shifted_ref[r - 1] = cat_ref[pl.ds(r, span), :]
    y = jnp.zeros((tm, u_ref.shape[2]), F32) + dwb_ref[...]
    for j in range(CONV_WIDTH):
        a, r = divmod(CONV_HALO - CONV_WIDTH + 1 + j, 8)
        tap = cat_ref[pl.ds(8 * a, tm), :] if r == 0 else shifted_ref[r - 1, pl.ds(8 * a, tm), :]
        y = y + dww_ref[j:j + 1, :] * tap
    mu = jnp.mean(y, axis=-1, keepdims=True)
    var = jnp.mean(jnp.square(y - mu), axis=-1, keepdims=True)
    cv = _silu((y - mu) * lax.rsqrt(var + NORM_EPS) * lng_ref[...] + lnb_ref[...])
    m = m_att + _dot(cv.astype(BF16), wbot_ref[...])
    x_new = x_ref[0] + g1_ref[0] * m
    xn_ref[0] = x_new
    _post_mixer(x_new, g_ref, sc_ref, sh_ref, rwt_ref, rb_ref, h2x_ref, cls_ref, cnt_ref)


def _hy_out(u, att_t, x, g1, wtop, wbot, dww, dwb, lng, lnb, g, sc, sh, rwt, rb, tm):
    bsz, s, d = x.shape
    cc = u.shape[2]
    nt = s // tm
    full = lambda shape: pl.BlockSpec(shape, lambda b, i: (0,) * len(shape))
    per_b = pl.BlockSpec((1, 1, d), lambda b, i: (b, 0, 0))
    halo = tm // CONV_HALO
    return pl.pallas_call(
        _hy_out_kernel,
        grid=(bsz, nt),
        in_specs=[pl.BlockSpec((1, tm, cc), lambda b, i: (b, i, 0)),
                  pl.BlockSpec((1, CONV_HALO, cc), lambda b, i: (b, jnp.maximum(i * halo - 1, 0), 0)),
                  pl.BlockSpec((1, ATT_WIDTH, tm), lambda b, i: (b, 0, i)),
                  pl.BlockSpec((1, tm, d), lambda b, i: (b, i, 0)),
                  per_b, full(wtop.shape), full(wbot.shape), full(dww.shape), full(dwb.shape),
                  full(lng.shape), full(lnb.shape), full(g.shape), per_b, per_b, full(rwt.shape), full(rb.shape)],
        out_specs=[pl.BlockSpec((1, tm, d), lambda b, i: (b, i, 0)),
                   pl.BlockSpec((tm, d + W_LANES), lambda b, i: (b * nt + i, 0)),
                   pl.BlockSpec((1, 1, tm), lambda b, i: (b * nt + i, 0, 0)),
                   pl.BlockSpec((CLASS_ROWS, 128), lambda b, i: (0, 0))],
        out_shape=[jax.ShapeDtypeStruct((bsz, s, d), F32),
                   jax.ShapeDtypeStruct((bsz * s, d + W_LANES), F32),
                   jax.ShapeDtypeStruct((bsz * nt, 1, tm), I32),
                   jax.ShapeDtypeStruct((CLASS_ROWS, 128), F32)],
        scratch_shapes=[pltpu.VMEM((tm + CONV_HALO, cc), F32),
                        pltpu.VMEM((7, tm + CONV_HALO - 8, cc), F32)],
        compiler_params=_params(2), name="hy_out_proj",
    )(u, u, att_t, x, g1, wtop, wbot, dww, dwb, lng, lnb, g, sc, sh, rwt, rb)


def _ssm_in_kernel(dest_ref, opaque_ref, ys_ref, xn_ref, g2_ref, sc_ref, sh_ref, g_ref, wz_ref, wx_ref, wdt_ref,
                   cw_ref, cb_ref, dtb_ref, x_ref, sz_ref, xbc_ref, dt_ref, buf_ref, sem, *cat_refs):
    tm = xn_ref.shape[1]
    ch = SSM_IN_COLS
    step = pl.program_id(0) * pl.num_programs(1) + pl.program_id(1)
    last = pl.num_programs(0) * pl.num_programs(1) - 1
    cur = step % 2
    ahead = jnp.minimum(step + 1, last)

    def gather(tile, slot, rows, after=0):
        for r in rows:
            row = dest_ref[tile * tm + r] + after
            pltpu.make_async_copy(ys_ref.at[pl.ds(row, 1), :], buf_ref.at[slot, pl.ds(r, 1), :],
                                  sem.at[slot]).start(priority=r % 2)

    def wait(slot):
        pltpu.make_async_copy(ys_ref.at[pl.ds(0, tm), :], buf_ref.at[slot], sem.at[slot]).wait()

    @pl.when(step == 0)
    def _():
        gather(0, 0, range(tm))

    @pl.when(pl.program_id(1) == 0)
    def _():
        for cat_ref in cat_refs:
            cat_ref[:SSM_HALO, :] = jnp.zeros((SSM_HALO, ch), F32)

    wait(cur)
    x = xn_ref[0] + g2_ref[0] * buf_ref[cur]
    x_ref[0] = x
    h = _mod_norm(x, g_ref[...], sc_ref[0], sh_ref[0])
    hb = h.astype(BF16)

    def conv_matmul(c):
        v = _dot(hb, wx_ref[:, c * ch:(c + 1) * ch])
        cat_refs[c][SSM_HALO:, :] = v
        return v

    def conv_act(c, _):
        cols = slice(c * ch, (c + 1) * ch)
        cat_ref = cat_refs[c]
        y = jnp.zeros((tm, ch), F32) + cb_ref[:, cols]
        for j in range(SSM_CONV):
            y = y + cw_ref[j:j + 1, cols] * cat_ref[pl.ds(SSM_HALO - SSM_CONV + 1 + j, tm), :]
        xbc_ref[0, :, cols] = _silu(y).astype(BF16)
        cat_ref[:SSM_HALO, :] = cat_ref[pl.ds(tm, SSM_HALO), :]

    def gate_matmul(c):
        return _dot(hb, wz_ref[:, c * ch:(c + 1) * ch])

    def gate_act(c, z):
        sz_ref[0, :, c * ch:(c + 1) * ch] = _silu(z).astype(BF16)

    stages = [(conv_matmul, conv_act, c) for c in range(wx_ref.shape[1] // ch)]
    stages += [(gate_matmul, gate_act, c) for c in range(wz_ref.shape[1] // ch)]
    shares = 8
    per_share = tm // shares
    gather(ahead, 1 - cur, range(per_share))
    pending = None
    for k, (matmul, act, c) in enumerate(stages):
        out = matmul(c)
        if k + 1 < shares:
            zero = out[0, 0].astype(I32) * opaque_ref[0]
            gather(ahead, 1 - cur, range((k + 1) * per_share, (k + 2) * per_share), zero)
        if pending is not None:
            pending[0](pending[1], pending[2])
        pending = (act, c, out)
    t = _dot(hb, wdt_ref[...]) + dtb_ref[...]
    pending[0](pending[1], pending[2])
    dt_ref[0] = jnp.maximum(t, 0.0) + jnp.log(1.0 + jnp.exp(-jnp.abs(t)))

    @pl.when(step == last)
    def _():
        wait(1 - cur)


def _ssm_in(dest, ys, x_new, g2, sc, sh, g, wz, wx, wdt, cw, cb, dtb, tm):
    bsz, s, d = x_new.shape
    full = lambda shape: pl.BlockSpec(shape, lambda b, i, *_: (0,) * len(shape), pipeline_mode=pl.Buffered(1))
    per_b = pl.BlockSpec((1, 1, d), lambda b, i, *_: (b, 0, 0))
    tile = lambda n: pl.BlockSpec((1, tm, n), lambda b, i, *_: (b, i, 0))
    return pl.pallas_call(
        _ssm_in_kernel,
        grid_spec=pltpu.PrefetchScalarGridSpec(
            num_scalar_prefetch=2, grid=(bsz, s // tm),
            in_specs=[pl.BlockSpec(memory_space=pl.ANY), tile(d), per_b, per_b, per_b, full(g.shape),
                      full(wz.shape), full(wx.shape), full(wdt.shape), full(cw.shape), full(cb.shape),
                      full(dtb.shape)],
            out_specs=[tile(d), tile(wz.shape[1]), tile(wx.shape[1]), tile(wdt.shape[1])],
            scratch_shapes=[pltpu.VMEM((2, tm, d), F32), pltpu.SemaphoreType.DMA((2,))]
            + [pltpu.VMEM((tm + SSM_HALO, SSM_IN_COLS), F32)] * (wx.shape[1] // SSM_IN_COLS)),
        out_shape=[jax.ShapeDtypeStruct((bsz, s, d), F32),
                   jax.ShapeDtypeStruct((bsz, s, wz.shape[1]), BF16),
                   jax.ShapeDtypeStruct((bsz, s, wx.shape[1]), BF16),
                   jax.ShapeDtypeStruct((bsz, s, wdt.shape[1]), F32)],
        compiler_params=_params(2), name="ssm_in_proj",
    )(dest, jnp.zeros((1,), I32), ys, x_new, g2, sc, sh, g, wz, wx, wdt, cw, cb, dtb)


def _ssd_kernel(xbc_ref, sz_ref, dt_ref, alog_ref, dsk_ref, ng_ref, wout_ref, e2_ref, x_ref, g1_ref,
                g_ref, sc_ref, sh_ref, rwt_ref, rb_ref,
                xn_ref, h2x_ref, cls_ref, cnt_ref, state_ref, macc_ref, y_ref):
    L = SSD_SUB
    inner = sz_ref.shape[2]
    gw = inner // SSM_GROUPS
    pw = 2 * SSM_HEAD_DIM

    @pl.when(pl.program_id(1) == 0)
    def _():
        state_ref[...] = jnp.zeros(state_ref.shape, F32)

    rows = x_ref.shape[1]
    subs = [slice(s * L, (s + 1) * L) for s in range(rows // L)]
    halves = [slice(s * SSM_CHUNK, (s + 1) * SSM_CHUNK) for s in range(rows // SSM_CHUNK)]
    brow = lax.broadcasted_iota(I32, (SSM_CHUNK, SSM_CHUNK), 0)
    bcol = lax.broadcasted_iota(I32, (SSM_CHUNK, SSM_CHUNK), 1)
    tri = jnp.where((brow // L == bcol // L) & (brow >= bcol), 1.0, 0.0).astype(BF16)
    lower = lax.broadcasted_iota(I32, (L, L), 0) >= lax.broadcasted_iota(I32, (L, L), 1)
    lo_half = lax.broadcasted_iota(I32, (L, pw), 1) < SSM_HEAD_DIM

    def expand(v):
        return _dot(jnp.concatenate(_split2(v), axis=1), e2_ref[...])

    macc_ref[...] = jnp.zeros(macc_ref.shape, F32)

    def group(g, carry):
        xcol = pl.ds(pl.multiple_of(g * gw, gw), gw)
        bcolumns = pl.ds(pl.multiple_of(inner + g * SSM_STATE, SSM_STATE), SSM_STATE)
        ccolumns = pl.ds(pl.multiple_of(inner + (SSM_GROUPS + g) * SSM_STATE, SSM_STATE), SSM_STATE)
        rate = -jnp.exp(alog_ref[g]) * LOG2E
        dt = dt_ref[0, :, pl.ds(pl.multiple_of(g * 128, 128), 128)]
        t3 = [_dot(tri, jnp.concatenate(_split3(dt[r, :] * rate), axis=1)) for r in halves]
        t3 = jnp.concatenate(t3, axis=0)
        a_cs = t3[:, :128] + t3[:, 128:256] + t3[:, 256:]
        xg = xbc_ref[0, :, xcol].astype(F32)
        bs = [xbc_ref[0, r, bcolumns] for r in subs]
        cs = [xbc_ref[0, r, ccolumns] for r in subs]
        cbm = [jnp.where(lower, _dot_nt(cm, bm), 0.0) for cm, bm in zip(cs, bs)]
        a_end = jnp.concatenate([jnp.broadcast_to(a_cs[r.stop - 1:r.stop, :], (L, 128)) for r in subs], axis=0)
        xw = (xg * expand(dt * jnp.exp2(a_end - a_cs))).astype(BF16)
        grow = expand(jnp.exp2(a_cs))
        keep = expand(jnp.exp2(jnp.concatenate([a_end[r.start:r.start + 16, :] for r in subs], axis=0)))
        st = state_ref[g]
        y_off = []
        for s, r in enumerate(subs):
            y_off.append(_dot(cs[s], st.astype(BF16)) * grow[r, :])
            st = st * keep[16 * s:16 * s + 1, :] + _dot(bs[s].astype(F32).T.astype(BF16), xw[r, :])
        state_ref[g] = st
        for s, r in enumerate(subs):
            a_sub = a_cs[r, :]
            a_sub_t = a_sub.T
            dt_sub_t = dt[r, :].T
            for q in range(gw // pw):
                x2 = xbc_ref[0, r, pl.ds(pl.multiple_of(g * gw + q * pw, pw), pw)]
                yp = y_off[s][:, q * pw:(q + 1) * pw]
                for e in range(2):
                    hd = 2 * q + e
                    dec = jnp.exp2(jnp.minimum(a_sub[:, hd:hd + 1] - a_sub_t[hd:hd + 1, :], 0.0))
                    xm = jnp.where(lo_half if e == 0 else jnp.logical_not(lo_half), x2, jnp.zeros_like(x2))
                    yp = yp + _dot((cbm[s] * dec * dt_sub_t[hd:hd + 1, :]).astype(BF16), xm)
                y_ref[r, q * pw:(q + 1) * pw] = yp
        gt = (y_ref[...] + xg * dsk_ref[g]) * sz_ref[0, :, xcol].astype(F32)
        ms = jnp.mean(gt * gt, axis=-1, keepdims=True)
        gn = gt * lax.rsqrt(ms + NORM_EPS) * ng_ref[g]
        macc_ref[...] += _dot(gn.astype(BF16), wout_ref[xcol, :])
        return carry

    lax.fori_loop(0, SSM_GROUPS, group, 0)
    x_new = x_ref[0] + g1_ref[0] * macc_ref[...]
    xn_ref[0] = x_new
    _post_mixer(x_new, g_ref, sc_ref, sh_ref, rwt_ref, rb_ref, h2x_ref, cls_ref, cnt_ref)


def _ssd(xbc, sz, dt, alog, dskip, ng, wout, e2, x, g1, g, sc, sh, rwt, rb):
    bsz, s, d = x.shape
    L = SSD_BLOCK
    nc = s // L
    gw = sz.shape[2] // SSM_GROUPS
    full = lambda shape: pl.BlockSpec(shape, lambda b, c: (0,) * len(shape))
    per_b = pl.BlockSpec((1, 1, d), lambda b, c: (b, 0, 0))
    tile = lambda n: pl.BlockSpec((1, L, n), lambda b, c: (b, c, 0))
    return pl.pallas_call(
        _ssd_kernel,
        grid=(bsz, nc),
        in_specs=[tile(xbc.shape[2]), tile(sz.shape[2]), tile(dt.shape[2]),
                  full(alog.shape), full(dskip.shape), full(ng.shape), full(wout.shape), full(e2.shape),
                  tile(d), per_b, full(g.shape), per_b, per_b, full(rwt.shape), full(rb.shape)],
        out_specs=[tile(d),
                   pl.BlockSpec((L, d + W_LANES), lambda b, c: (b * nc + c, 0)),
                   pl.BlockSpec((1, 1, L), lambda b, c: (b * nc + c, 0, 0)),
                   pl.BlockSpec((CLASS_ROWS, 128), lambda b, c: (0, 0))],
        out_shape=[jax.ShapeDtypeStruct((bsz, s, d), F32),
                   jax.ShapeDtypeStruct((bsz * s, d + W_LANES), F32),
                   jax.ShapeDtypeStruct((bsz * nc, 1, L), I32),
                   jax.ShapeDtypeStruct((CLASS_ROWS, 128), F32)],
        scratch_shapes=[pltpu.VMEM((SSM_GROUPS, SSM_STATE, gw), F32),
                        pltpu.VMEM((L, d), F32),
                        pltpu.VMEM((L, gw), F32)],
        compiler_params=_params(2), name="ssd_out_proj",
    )(xbc, sz, dt, alog, dskip, ng, wout, e2, x, g1, g, sc, sh, rwt, rb)


def _dest_kernel(cls_ref, start_ref, dest_ref, run_ref):
    @pl.when(pl.program_id(0) == 0)
    def _():
        run_ref[...] = start_ref[...]

    tr = cls_ref.shape[2]
    upper = lax.broadcasted_iota(I32, (tr, tr), 0) <= lax.broadcasted_iota(I32, (tr, tr), 1)
    upper = jnp.where(upper, 1.0, 0.0).astype(BF16)
    for k in range(cls_ref.shape[0]):
        onehot = lax.broadcasted_iota(I32, (CLASS_ROWS, tr), 0) == cls_ref[k]
        oh = jnp.where(onehot, 1.0, 0.0)
        prefix = _dot(oh.astype(BF16), upper)
        dest = jnp.sum(oh * (prefix - 1.0 + run_ref[...]), axis=0, keepdims=True)
        dest_ref[k] = dest.astype(I32)
        run_ref[...] += jnp.sum(oh, axis=1, keepdims=True)


def _moe_plan(cls, cnt, tr):
    nt = cls.shape[0]
    t = nt * tr
    reps = 4 if nt % 4 == 0 else 1
    padded = jnp.ceil(cnt[:, 0] / MOE_TILE) * MOE_TILE
    end = jnp.cumsum(padded)
    start = end - padded
    dest = pl.pallas_call(
        _dest_kernel, grid=(nt // reps,),
        in_specs=[pl.BlockSpec((reps, 1, tr), lambda i: (i, 0, 0)),
                  pl.BlockSpec((CLASS_ROWS, 1), lambda i: (0, 0))],
        out_specs=pl.BlockSpec((reps, 1, tr), lambda i: (i, 0, 0)),
        out_shape=jax.ShapeDtypeStruct((nt, 1, tr), I32),
        scratch_shapes=[pltpu.VMEM((CLASS_ROWS, 1), F32)],
        compiler_params=_params(1), name="moe_dest",
    )(cls, start.reshape(CLASS_ROWS, 1))
    n_tiles = t // MOE_TILE + N_CLASSES
    tile_row = jnp.arange(n_tiles, dtype=F32) * MOE_TILE
    total = end[N_CLASSES - 1]
    valid = tile_row < total
    tcls = jnp.sum((tile_row[:, None] >= end[None, :N_CLASSES]).astype(I32), axis=1)
    last = jnp.sum((total - MOE_TILE >= end[:N_CLASSES]).astype(I32))
    tcls = jnp.where(valid, tcls, last)
    grp = tcls // N_PAIRS
    pair = tcls % N_PAIRS
    lo = jnp.where(pair < 3, 0, jnp.where(pair < 5, 1, 2))
    hi = jnp.where(pair < 3, pair + 1, jnp.where(pair < 5, pair - 1, 3))
    meta = jnp.stack([grp * EXPERTS_PER_GROUP + lo, grp * EXPERTS_PER_GROUP + hi, valid.astype(I32),
                      jnp.zeros_like(tcls)]).astype(I32)
    return dest.reshape(t), meta


def _invert_kernel(dest_ref, zeros_ref, src_ref, sem):
    clear = pltpu.make_async_copy(zeros_ref, src_ref, sem)
    clear.start()
    clear.wait()

    def put(t0, carry):
        for u in range(INVERT_UNROLL):
            t = t0 * INVERT_UNROLL + u
            src_ref[dest_ref[t]] = t
        return carry

    lax.fori_loop(0, dest_ref.shape[0] // INVERT_UNROLL, put, 0)


def _invert(dest, n_slots):
    smem = pl.BlockSpec(memory_space=pltpu.SMEM)
    return pl.pallas_call(
        _invert_kernel, in_specs=[smem, pl.BlockSpec(memory_space=pl.ANY)], out_specs=smem,
        out_shape=jax.ShapeDtypeStruct((n_slots,), I32),
        scratch_shapes=[pltpu.SemaphoreType.DMA(())], name="moe_invert",
    )(dest, jnp.zeros((n_slots,), I32))


def _expert_kernel(meta_ref, src_ref, h2x_ref, wg_a_ref, wu_a_ref, wd_a_ref, wg_b_ref, wu_b_ref, wd_b_ref, y_ref,
                   buf0_ref, buf1_ref, sem):
    i = pl.program_id(0)
    d = y_ref.shape[1]
    bufs = (buf0_ref, buf1_ref)

    def gather(tile, slot, rows=range(MOE_TILE), after=0):
        for r in rows:
            tok = src_ref[tile * MOE_TILE + r] + after
            pltpu.make_async_copy(h2x_ref.at[pl.ds(tok, 1), :], bufs[slot].at[pl.ds(r, 1), :],
                                  sem.at[slot]).start(priority=r % 2)

    def wait(slot):
        pltpu.make_async_copy(h2x_ref.at[pl.ds(0, MOE_TILE), :], bufs[slot], sem.at[slot]).wait()

    @pl.when(i == 0)
    def _():
        gather(0, 0)

    valid = meta_ref[2, i] != 0
    cw = EXPERT_COLS
    for cur in range(2):
        @pl.when(valid & (i % 2 == cur))
        def _(cur=cur):
            wait(cur)
            xb = bufs[cur][:, :d].astype(BF16)
            wts = bufs[cur][:, d:]
            n_parts = (2 * wg_a_ref.shape[3] + d) // cw
            per_part = MOE_TILE // n_parts
            gather(i + 1, 1 - cur, range(per_part))
            part = [1]

            def pace(result):
                if part[0] == n_parts:
                    return
                zero = result[0, 0].astype(I32) * meta_ref[3, i]
                gather(i + 1, 1 - cur, range(part[0] * per_part, (part[0] + 1) * per_part), zero)
                part[0] += 1

            def dot_cols(a, w_ref):
                chunks = []
                for c in range(w_ref.shape[3] // cw):
                    chunks.append(_dot(a, w_ref[0, 0, :, c * cw:(c + 1) * cw]))
                    pace(chunks[-1])
                return jnp.concatenate(chunks, axis=1)

            y = jnp.zeros(y_ref.shape, F32)
            for e, (wg_ref, wu_ref, wd_ref) in enumerate(((wg_a_ref, wu_a_ref, wd_a_ref),
                                                          (wg_b_ref, wu_b_ref, wd_b_ref))):
                act = _silu(dot_cols(xb, wg_ref)) * dot_cols(xb, wu_ref)
                y = y + wts[:, e:e + 1] * dot_cols(act.astype(BF16), wd_ref)
            y_ref[...] = y

        @pl.when(jnp.logical_not(valid) & (meta_ref[2, jnp.maximum(i - 1, 0)] != 0) & (i % 2 == cur))
        def _(cur=cur):
            wait(cur)

    @pl.when(jnp.logical_not(valid))
    def _():
        y_ref[...] = jnp.zeros(y_ref.shape, F32)


def _experts(meta, src, h2x, layer, wg, wu, wd):
    n_slots = src.shape[0]
    w = h2x.shape[1]
    d = w - W_LANES
    n_tiles = n_slots // MOE_TILE
    expert = lambda arr, which: pl.BlockSpec((1, 1) + arr.shape[2:], lambda i, m, s: (layer, m[which, i], 0, 0))
    return pl.pallas_call(
        _expert_kernel,
        grid_spec=pltpu.PrefetchScalarGridSpec(
            num_scalar_prefetch=2, grid=(n_tiles,),
            in_specs=[pl.BlockSpec(memory_space=pl.ANY),
                      expert(wg, 0), expert(wu, 0), expert(wd, 0), expert(wg, 1), expert(wu, 1), expert(wd, 1)],
            out_specs=pl.BlockSpec((MOE_TILE, d), lambda i, m, s: (i, 0)),
            scratch_shapes=[pltpu.VMEM((MOE_TILE, w), F32), pltpu.VMEM((MOE_TILE, w), F32),
                            pltpu.SemaphoreType.DMA((2,))]),
        out_shape=jax.ShapeDtypeStruct((n_slots, d), F32),
        compiler_params=_params(1), name="moe_experts",
    )(meta, src, h2x, wg, wu, wd, wg, wu, wd)


def _combine_kernel(dest_ref, ys_ref, x_ref, g2_ref, o_ref, buf0_ref, buf1_ref, sem):
    tm = x_ref.shape[1]
    step = pl.program_id(0) * pl.num_programs(1) + pl.program_id(1)
    n_steps = pl.num_programs(0) * pl.num_programs(1)
    bufs = (buf0_ref, buf1_ref)

    def gather(st, slot):
        for r in range(tm):
            pltpu.make_async_copy(ys_ref.at[pl.ds(dest_ref[st * tm + r], 1), :], bufs[slot].at[pl.ds(r, 1), :],
                                  sem.at[slot]).start(priority=r % 2)

    @pl.when(step == 0)
    def _():
        gather(0, 0)

    for cur in range(2):
        @pl.when(step % 2 == cur)
        def _(cur=cur):
            @pl.when(step + 1 < n_steps)
            def _():
                gather(step + 1, 1 - cur)

            pltpu.make_async_copy(ys_ref.at[pl.ds(0, tm), :], bufs[cur], sem.at[cur]).wait()
            o_ref[0] = x_ref[0] + g2_ref[0] * bufs[cur][...]


def _combine(dest, ys, x, g2, tm):
    bsz, s, d = x.shape
    return pl.pallas_call(
        _combine_kernel,
        grid_spec=pltpu.PrefetchScalarGridSpec(
            num_scalar_prefetch=1, grid=(bsz, s // tm),
            in_specs=[pl.BlockSpec(memory_space=pl.ANY),
                      pl.BlockSpec((1, tm, d), lambda b, i, dd: (b, i, 0)),
                      pl.BlockSpec((1, 1, d), lambda b, i, dd: (b, 0, 0))],
            out_specs=pl.BlockSpec((1, tm, d), lambda b, i, dd: (b, i, 0)),
            scratch_shapes=[pltpu.VMEM((tm, d), F32), pltpu.VMEM((tm, d), F32), pltpu.SemaphoreType.DMA((2,))]),
        out_shape=jax.ShapeDtypeStruct((bsz, s, d), F32),
        compiler_params=_params(2), name="moe_combine",
    )(dest, ys, x, g2)


def _moe(h2x, cls, cnt, layer, wg, wu, wd):
    t = h2x.shape[0]
    tr = min(512, t)
    dest, meta = _moe_plan(cls.reshape(t // tr, 1, tr), cnt, tr)
    n_slots = t + N_CLASSES * MOE_TILE
    return dest, _experts(meta, _invert(dest, n_slots), h2x, layer, wg, wu, wd)


def kernel(x, c, ada_w, ada_b, norm_mix, norm_ffn, hy_w_in, hy_q_norm, hy_k_norm, hy_dw_w, hy_dw_b, hy_ln_g, hy_ln_b, hy_w_out, ssm_w_in, ssm_conv_w, ssm_conv_b, ssm_dt_bias, ssm_a_log, ssm_d, ssm_norm, ssm_w_out, router_w, router_bias, exp_w_gate, exp_w_up, exp_w_down):
    bsz, s, d = x.shape
    depth = ada_w.shape[0]
    assert s % 512 == 0 and bsz <= 8
    c8 = jnp.zeros((8, d), F32).at[:bsz].set(c)
    mod = _ada(c8, ada_w, ada_b)
    rwt = router_w.T
    rb = router_bias.reshape(N_EXPERTS, 1)
    lane_head = jnp.arange(ATT_WIDTH) // HEAD_DIM
    pmat = ((lane_head[:, None] == lane_head[None, :]).astype(F32) / HEAD_DIM).astype(BF16)

    expert_w = tuple(w.astype(BF16) for w in (exp_w_gate, exp_w_up, exp_w_down))
    moe = None
    for layer in range(depth):
        sh1, sc1, g1, sh2, sc2, g2 = (mod[layer, :bsz, i * d:(i + 1) * d].reshape(bsz, 1, d) for i in range(N_MOD))
        gm = norm_mix[layer].reshape(1, d)
        gf = norm_ffn[layer].reshape(1, d)
        j = layer // 2
        if layer % 2 == 0:
            if moe is not None:
                x = _combine(*moe, min(256, s))
            w_in = hy_w_in[j].astype(BF16)
            aw = ATT_WIDTH
            k, qt, vt, u, kmean = _hy_in(
                x, sc1, sh1, gm, w_in[:, aw:2 * aw], w_in[:, 3 * aw:], w_in[:, :aw].T, w_in[:, 2 * aw:3 * aw].T,
                (jnp.tile(hy_q_norm[j], ATT_HEADS) * (HEAD_DIM ** -0.5 * LOG2E)).reshape(aw, 1),
                jnp.tile(hy_k_norm[j], ATT_HEADS).reshape(1, aw), pmat, min(1024, s))
            bias = _gate(kmean.reshape(bsz, s // MOBA_BLOCK, aw), qt, min(2048, s))
            att_t = _attn(qt, k, vt, bias, 8)
            w_out = hy_w_out[j].astype(BF16)
            dww = jnp.zeros((CONV_HALO, hy_dw_w.shape[2]), F32).at[:CONV_WIDTH].set(hy_dw_w[j])
            x_new, h2x, cls, cnt = _hy_out(
                u, att_t, x, g1, w_out[:aw], w_out[aw:], dww, hy_dw_b[j].reshape(1, -1),
                hy_ln_g[j].reshape(1, -1), hy_ln_b[j].reshape(1, -1), gf, sc2, sh2, rwt, rb, 512)
        else:
            w_in = ssm_w_in[j]
            inner = ssm_norm.shape[1]
            heads = ssm_a_log.shape[1]
            hpg = heads // SSM_GROUPS
            conv_dim = ssm_conv_w.shape[2]

            def by_group(v):
                v = v.reshape(v.shape[:-1] + (SSM_GROUPS, hpg))
                pad = [(0, 0)] * (v.ndim - 1) + [(0, 128 - hpg)]
                return jnp.pad(v, pad).reshape(v.shape[:-2] + (SSM_GROUPS * 128,))

            x, sz, xbc, dt = _ssm_in(
                *moe, sc1, sh1, gm, w_in[:, :inner].astype(BF16), w_in[:, inner:inner + conv_dim].astype(BF16),
                by_group(w_in[:, inner + conv_dim:]).astype(BF16), ssm_conv_w[j], ssm_conv_b[j].reshape(1, -1),
                by_group(ssm_dt_bias[j]).reshape(1, -1), 512)
            chan_head = jnp.arange(inner // SSM_GROUPS) // SSM_HEAD_DIM
            e2 = (jnp.arange(256)[:, None] % 128 == chan_head[None, :]).astype(BF16)
            x_new, h2x, cls, cnt = _ssd(
                xbc, sz, dt, by_group(ssm_a_log[j]).reshape(SSM_GROUPS, 1, 128),
                jnp.repeat(ssm_d[j], SSM_HEAD_DIM).reshape(SSM_GROUPS, 1, -1),
                ssm_norm[j].reshape(SSM_GROUPS, 1, -1), ssm_w_out[j].astype(BF16), e2, x, g1, gf, sc2, sh2, rwt, rb)
        moe = _moe(h2x, cls, cnt, layer, *expert_w) + (x_new, g2)
    return _combine(*moe, min(256, s))
```

```python
import functools

import jax
import jax.numpy as jnp
from jax import lax
from jax.experimental import pallas as pl
from jax.experimental.pallas import tpu as pltpu

F32 = jnp.float32
BF16 = jnp.bfloat16
I32 = jnp.int32

NORM_EPS = 1e-6
N_MOD = 6
HEAD_DIM = 64
ATT_HEADS = 8
ATT_WIDTH = ATT_HEADS * HEAD_DIM
MOBA_BLOCK = 256
MOBA_TOPK = 3
CONV_WIDTH = 31
CONV_HALO = 32
SSM_HEAD_DIM = 64
SSM_GROUPS = 4
SSM_STATE = 128
SSM_CONV = 4
SSM_CHUNK = 256
SSM_HALO = 8
SSD_SUB = 128
SSD_BLOCK = 512
SSM_IN_COLS = 512
N_EXPERTS = 16
N_EXPERT_GROUPS = 4
EXPERTS_PER_GROUP = 4
N_PAIRS = 6
N_CLASSES = N_EXPERT_GROUPS * N_PAIRS
CLASS_ROWS = 32
MOE_TILE = 256
EXPERT_COLS = 256
W_LANES = 128
INVERT_UNROLL = 32
LOG2E = 1.4426950408889634
NEG = -1e30
VMEM_LIMIT = 56 * 1024 * 1024


def _params(n_axes):
    return pltpu.CompilerParams(dimension_semantics=("arbitrary",) * n_axes,
                                vmem_limit_bytes=VMEM_LIMIT)


def _dot(a, b):
    return jnp.dot(a, b, preferred_element_type=F32)


def _dot_nt(a, b):
    return lax.dot_general(a, b, (((1,), (1,)), ((), ())), preferred_element_type=F32)


def _split2(x):
    hi = x.astype(BF16)
    lo = (x - hi.astype(F32)).astype(BF16)
    return hi, lo


def _split3(x):
    a = x.astype(BF16)
    r = x - a.astype(F32)
    b = r.astype(BF16)
    c = (r - b.astype(F32)).astype(BF16)
    return a, b, c


def _silu(x):
    return x * jax.nn.sigmoid(x)


def _mod_norm(x, g, sc, sh):
    ms = jnp.mean(x * x, axis=-1, keepdims=True)
    return x * lax.rsqrt(ms + NORM_EPS) * g * (1.0 + sc) + sh


def _ada_kernel(c_ref, w_ref, b_ref, o_ref):
    cond = _silu(c_ref[...])
    ch, cl = _split2(cond)
    wh, wl = _split2(w_ref[0])
    o_ref[0] = _dot(ch, wh) + _dot(ch, wl) + _dot(cl, wh) + b_ref[0]


def _ada(c8, ada_w, ada_b):
    depth, d, n = ada_w.shape
    tn = 1536
    return pl.pallas_call(
        _ada_kernel,
        grid=(depth, n // tn),
        in_specs=[pl.BlockSpec((8, d), lambda l, j: (0, 0)),
                  pl.BlockSpec((1, d, tn), lambda l, j: (l, 0, j)),
                  pl.BlockSpec((1, 1, tn), lambda l, j: (l, 0, j))],
        out_specs=pl.BlockSpec((1, 8, tn), lambda l, j: (l, 0, j)),
        out_shape=jax.ShapeDtypeStruct((depth, 8, n), F32),
        compiler_params=_params(2), name="ada_mod",
    )(c8, ada_w, ada_b.reshape(depth, 1, n))


def _hy_in_kernel(x_ref, sc_ref, sh_ref, g_ref, wk_ref, wu_ref, wqt_ref, wvt_ref, qg_ref, kg_ref, p_ref,
                  k_ref, qt_ref, vt_ref, u_ref, km_ref):
    tm = x_ref.shape[1]
    h = _mod_norm(x_ref[0], g_ref[...], sc_ref[0], sh_ref[0])
    hb = h.astype(BF16)
    k = _dot(hb, wk_ref[...])
    khi, klo = _split2(k * k)
    ms = _dot(khi, p_ref[...]) + _dot(klo, p_ref[...])
    kn = k * lax.rsqrt(ms + NORM_EPS) * kg_ref[...]
    k_ref[0] = kn.astype(BF16)
    km_ref[0, 0] = jnp.mean(kn.reshape(tm // MOBA_BLOCK, MOBA_BLOCK, ATT_WIDTH), axis=1)
    qt = _dot_nt(wqt_ref[...], hb).reshape(ATT_HEADS, HEAD_DIM, tm)
    qms = jnp.mean(qt * qt, axis=1, keepdims=True)
    qn = (qt * lax.rsqrt(qms + NORM_EPS)).reshape(ATT_WIDTH, tm) * qg_ref[...]
    qt_ref[0] = qn.astype(BF16)
    vt_ref[0] = _dot_nt(wvt_ref[...], hb).astype(BF16)
    ag = _dot(hb, wu_ref[...])
    half = ag.shape[1] // 2
    u_ref[0] = (ag[:, :half] * jax.nn.sigmoid(ag[:, half:])).astype(BF16)


def _hy_in(x, sc, sh, g, wk, wu, wqt, wvt, qg, kg, pmat, tm):
    bsz, s, d = x.shape
    cc = wu.shape[1] // 2
    full = lambda shape: pl.BlockSpec(shape, lambda b, i: (0,) * len(shape))
    return pl.pallas_call(
        _hy_in_kernel,
        grid=(bsz, s // tm),
        in_specs=[pl.BlockSpec((1, tm, d), lambda b, i: (b, i, 0)),
                  pl.BlockSpec((1, 1, d), lambda b, i: (b, 0, 0)),
                  pl.BlockSpec((1, 1, d), lambda b, i: (b, 0, 0)),
                  full((1, d)), full(wk.shape), full(wu.shape), full(wqt.shape), full(wvt.shape),
                  full(qg.shape), full(kg.shape), full(pmat.shape)],
        out_specs=[pl.BlockSpec((1, tm, ATT_WIDTH), lambda b, i: (b, i, 0)),
                   pl.BlockSpec((1, ATT_WIDTH, tm), lambda b, i: (b, 0, i)),
                   pl.BlockSpec((1, ATT_WIDTH, tm), lambda b, i: (b, 0, i)),
                   pl.BlockSpec((1, tm, cc), lambda b, i: (b, i, 0)),
                   pl.BlockSpec((1, 1, tm // MOBA_BLOCK, ATT_WIDTH), lambda b, i: (b, i, 0, 0))],
        out_shape=[jax.ShapeDtypeStruct((bsz, s, ATT_WIDTH), BF16),
                   jax.ShapeDtypeStruct((bsz, ATT_WIDTH, s), BF16),
                   jax.ShapeDtypeStruct((bsz, ATT_WIDTH, s), BF16),
                   jax.ShapeDtypeStruct((bsz, s, cc), BF16),
                   jax.ShapeDtypeStruct((bsz, s // tm, tm // MOBA_BLOCK, ATT_WIDTH), F32)],
        compiler_params=_params(2), name="hy_in_proj",
    )(x, sc, sh, g, wk, wu, wqt, wvt, qg, kg, pmat)


def _gate_kernel(km_ref, qt_ref, o_ref):
    nb = km_ref.shape[1]
    ts = qt_ref.shape[2]
    rows = ATT_HEADS * nb
    km = km_ref[0]
    kmt = jnp.broadcast_to(km[None], (ATT_HEADS, nb, ATT_WIDTH)).reshape(rows, ATT_WIDTH)
    rh = lax.broadcasted_iota(I32, (rows, ATT_WIDTH), 0) // nb
    ch = lax.broadcasted_iota(I32, (rows, ATT_WIDTH), 1) // HEAD_DIM
    kbd = jnp.where(rh == ch, kmt, 0.0)
    khi, klo = _split2(kbd)
    qt = qt_ref[0]
    gate = (_dot(khi, qt) + _dot(klo, qt)).reshape(ATT_HEADS, nb, ts)
    blk = lax.broadcasted_iota(I32, (ATT_HEADS, nb, ts), 1)
    qblk = (pl.program_id(1) * ts + lax.broadcasted_iota(I32, (ATT_HEADS, nb, ts), 2)) // MOBA_BLOCK
    g = jnp.where(blk < qblk, gate, -jnp.inf)
    sel = jnp.zeros(g.shape, jnp.bool_)
    for _ in range(MOBA_TOPK):
        m = jnp.max(g, axis=1, keepdims=True)
        first = jnp.min(jnp.where((g == m) & (m > -jnp.inf), blk, nb), axis=1, keepdims=True)
        pick = blk == first
        sel = sel | pick
        g = jnp.where(pick, -jnp.inf, g)
    o_ref[0] = jnp.where(sel, 0.0, NEG).reshape(rows, ts)


def _gate(kmean, qt, ts):
    bsz, nb, _ = kmean.shape
    s = qt.shape[2]
    rows = ATT_HEADS * nb
    return pl.pallas_call(
        _gate_kernel,
        grid=(bsz, s // ts),
        in_specs=[pl.BlockSpec((1, nb, ATT_WIDTH), lambda b, i: (b, 0, 0)),
                  pl.BlockSpec((1, ATT_WIDTH, ts), lambda b, i: (b, 0, i))],
        out_specs=pl.BlockSpec((1, rows, ts), lambda b, i: (b, 0, i)),
        out_shape=jax.ShapeDtypeStruct((bsz, rows, s), F32),
        compiler_params=_params(2), name="moba_gate",
    )(kmean, qt)


def _attn_kernel(qt_ref, k_ref, vt_ref, bias_ref, o_ref, qa_ref, acc_ref, m_ref, *, nb, hp, lag, kb):
    blk = MOBA_BLOCK
    pw = 2 * HEAD_DIM
    va = HEAD_DIM + 16
    i = pl.program_id(2)
    row = lax.broadcasted_iota(I32, (pw, blk), 0)
    for h in range(hp):
        qp = qt_ref[0, (h // 2) * pw:(h // 2 + 1) * pw, :]
        keep = (row < HEAD_DIM) if h % 2 == 0 else (row >= HEAD_DIM)
        qa_ref[h, :pw, :] = jnp.where(keep, qp, jnp.zeros_like(qp))
        qa_ref[h, pw:pw + nb, :] = bias_ref[0, h * nb:(h + 1) * nb, :].astype(BF16)
        qa_ref[h, pw + nb:, :] = jnp.zeros((blk - pw - nb, blk), BF16)
        m_ref[h] = jnp.full((1, blk), NEG, F32)
        acc_ref[h] = jnp.zeros((va, blk), F32)
    causal = lax.broadcasted_iota(I32, (blk, blk), 0) <= lax.broadcasted_iota(I32, (blk, blk), 1)
    lane = lax.broadcasted_iota(I32, (blk, pw), 1)
    ones = jnp.ones((va - HEAD_DIM, blk), BF16)

    def step(blocks):
        units = [(b, h) for b in range(len(blocks)) for h in range(hp)]
        offs = [pl.multiple_of(j * blk, blk) for j, _, _ in blocks]

        def scores(u):
            b, h = units[u]
            pp = h // 2
            ka = jnp.concatenate([k_ref[0, pl.ds(offs[b], blk), pp * pw:(pp + 1) * pw], blocks[b][1]], axis=1)
            return blocks[b][2](_dot(ka, qa_ref[h]))

        def softmax(u, s):
            h = units[u][1]
            m_old = m_ref[h]
            m_new = jnp.maximum(m_old, jnp.max(s, axis=0, keepdims=True))
            m_ref[h] = m_new
            return jnp.exp2(s - m_new).astype(BF16), jnp.exp2(m_old - m_new)

        def accumulate(u, p, alpha):
            b, h = units[u]
            vj = jnp.concatenate([vt_ref[0, h * HEAD_DIM:(h + 1) * HEAD_DIM, pl.ds(offs[b], blk)], ones], axis=0)
            acc_ref[h] = alpha * acc_ref[h] + _dot(vj, p)

        s, pa = {}, {}
        for t in range(len(units) + lag):
            if t < len(units):
                s[t] = scores(t)
            if 1 <= t <= len(units):
                pa[t - 1] = softmax(t - 1, s.pop(t - 1))
            if t >= lag:
                accumulate(t - lag, *pa.pop(t - lag))

    def past(j):
        return (j, jnp.where(lane == j, 1.0, 0.0).astype(BF16), lambda s: s)

    own = (i, jnp.zeros((blk, pw), BF16), lambda s: jnp.where(causal, s, NEG))

    @pl.when(i == 0)
    def _():
        step([own])

    @pl.when(i > 0)
    def _():
        step([own, past(0)])

    n_rest = jnp.maximum(i - 1, 0)
    rem = n_rest % kb
    base = jnp.int32(1)
    size = 1
    while size < kb:
        take = (rem & size) != 0

        @pl.when(take)
        def _(base=base, size=size):
            step([past(base + t) for t in range(size)])

        base = base + jnp.where(take, size, 0)
        size *= 2

    def body(r, carry):
        step([past(1 + rem + kb * r + t) for t in range(kb)])
        return carry

    lax.fori_loop(0, n_rest // kb, body, 0)
    for h in range(hp):
        acc = acc_ref[h]
        o_ref[0, h * HEAD_DIM:(h + 1) * HEAD_DIM, :] = (acc[:HEAD_DIM] / acc[HEAD_DIM:HEAD_DIM + 1]).astype(BF16)


def _attn(qt, k, vt, bias, hp):
    bsz, s, _ = k.shape
    nb = s // MOBA_BLOCK
    assert nb <= MOBA_BLOCK - 2 * HEAD_DIM and nb % 16 == 0
    hw = hp * HEAD_DIM
    return pl.pallas_call(
        functools.partial(_attn_kernel, nb=nb, hp=hp, lag=6, kb=4),
        grid=(bsz, ATT_HEADS // hp, nb),
        in_specs=[pl.BlockSpec((1, hw, MOBA_BLOCK), lambda b, p, i: (b, p, i)),
                  pl.BlockSpec((1, s, hw), lambda b, p, i: (b, 0, p)),
                  pl.BlockSpec((1, hw, s), lambda b, p, i: (b, p, 0)),
                  pl.BlockSpec((1, hp * nb, MOBA_BLOCK), lambda b, p, i: (b, p, i))],
        out_specs=pl.BlockSpec((1, hw, MOBA_BLOCK), lambda b, p, i: (b, p, i)),
        out_shape=jax.ShapeDtypeStruct((bsz, ATT_WIDTH, s), BF16),
        scratch_shapes=[pltpu.VMEM((hp, MOBA_BLOCK, MOBA_BLOCK), BF16),
                        pltpu.VMEM((hp, HEAD_DIM + 16, MOBA_BLOCK), F32),
                        pltpu.VMEM((hp, 1, MOBA_BLOCK), F32)],
        compiler_params=_params(3), name="moba_attn",
    )(qt, k, vt, bias)


def _route(logit_t, rb):
    aff = jax.nn.sigmoid(logit_t)
    score = aff + rb
    s = [score[e:e + 1, :] for e in range(N_EXPERTS)]
    a = [aff[e:e + 1, :] for e in range(N_EXPERTS)]
    n = EXPERTS_PER_GROUP

    def top2_sum(v):
        best = None
        for x in range(n):
            for y in range(x + 1, n):
                t = v[x] + v[y]
                best = t if best is None else jnp.maximum(best, t)
        return best

    def first_argmax(v):
        idx = jnp.zeros(v[0].shape, I32)
        cur = v[0]
        for x in range(1, len(v)):
            better = v[x] > cur
            idx = jnp.where(better, x, idx)
            cur = jnp.where(better, v[x], cur)
        return idx

    grp = first_argmax([top2_sum(s[n * g:n * g + n]) for g in range(N_EXPERT_GROUPS)])

    def in_group(v, x):
        out = v[(N_EXPERT_GROUPS - 1) * n + x]
        for g in range(N_EXPERT_GROUPS - 2, -1, -1):
            out = jnp.where(grp == g, v[n * g + x], out)
        return out

    sg = [in_group(s, x) for x in range(n)]
    ag = [in_group(a, x) for x in range(n)]
    l1 = first_argmax(sg)
    l2 = first_argmax([jnp.where(l1 == x, -jnp.inf, sg[x]) for x in range(n)])
    lo = jnp.minimum(l1, l2)
    hi = jnp.maximum(l1, l2)
    pair = jnp.where(lo == 0, hi - 1, jnp.where(lo == 1, hi + 1, 5))
    a_lo = jnp.zeros_like(ag[0])
    a_hi = jnp.zeros_like(ag[0])
    for x in range(n):
        a_lo = jnp.where(lo == x, ag[x], a_lo)
        a_hi = jnp.where(hi == x, ag[x], a_hi)
    tot = a_lo + a_hi
    return grp * N_PAIRS + pair, a_lo / tot, a_hi / tot


def _post_mixer(x_new, g_ref, sc_ref, sh_ref, rwt_ref, rb_ref, h2x_ref, cls_ref, cnt_ref):
    tm, d = x_new.shape
    h2 = _mod_norm(x_new, g_ref[...], sc_ref[0], sh_ref[0])
    hh, hl = _split2(h2)
    rh, rl = _split2(rwt_ref[...])
    logit_t = _dot_nt(rh, hh) + _dot_nt(rh, hl) + _dot_nt(rl, hh)
    cls, w_lo, w_hi = _route(logit_t, rb_ref[...])
    cls_ref[0] = cls

    @pl.when((pl.program_id(0) == 0) & (pl.program_id(1) == 0))
    def _():
        cnt_ref[...] = jnp.zeros(cnt_ref.shape, F32)

    onehot = lax.broadcasted_iota(I32, (CLASS_ROWS, tm), 0) == cls
    cnt_ref[...] += jnp.sum(jnp.where(onehot, 1.0, 0.0), axis=1, keepdims=True)
    wrow = lax.broadcasted_iota(I32, (W_LANES, tm), 0)
    wt = jnp.where(wrow == 0, w_lo, jnp.where(wrow == 1, w_hi, 0.0))
    h2x_ref[:, :d] = h2
    h2x_ref[:, d:] = wt.T


def _hy_out_kernel(u_ref, up_ref, at_ref, x_ref, g1_ref, wtop_ref, wbot_ref, dww_ref, dwb_ref, lng_ref, lnb_ref,
                   g_ref, sc_ref, sh_ref, rwt_ref, rb_ref,
                   xn_ref, h2x_ref, cls_ref, cnt_ref, cat_ref, shifted_ref):
    tm = u_ref.shape[1]
    m_att = _dot(at_ref[0].astype(F32).T.astype(BF16), wtop_ref[...])
    prev = up_ref[0].astype(F32)
    cat_ref[:CONV_HALO, :] = jnp.where(pl.program_id(1) == 0, 0.0, prev)
    cat_ref[CONV_HALO:, :] = u_ref[0].astype(F32)
    span = tm + CONV_HALO - 8
    for r in range(1, 8):
        shifted_ref[r - 1] = cat_ref[pl.ds(r, span), :]
    y = jnp.zeros((tm, u_ref.shape[2]), F32) + dwb_ref[...]
    for j in range(CONV_WIDTH):
        a, r = divmod(CONV_HALO - CONV_WIDTH + 1 + j, 8)
        tap = cat_ref[pl.ds(8 * a, tm), :] if r == 0 else shifted_ref[r - 1, pl.ds(8 * a, tm), :]
        y = y + dww_ref[j:j + 1, :] * tap
    mu = jnp.mean(y, axis=-1, keepdims=True)
    var = jnp.mean(jnp.square(y - mu), axis=-1, keepdims=True)
    cv = _silu((y - mu) * lax.rsqrt(var + NORM_EPS) * lng_ref[...] + lnb_ref[...])
    m = m_att + _dot(cv.astype(BF16), wbot_ref[...])
    x_new = x_ref[0] + g1_ref[0] * m
    xn_ref[0] = x_new
    _post_mixer(x_new, g_ref, sc_ref, sh_ref, rwt_ref, rb_ref, h2x_ref, cls_ref, cnt_ref)


def _hy_out(u, att_t, x, g1, wtop, wbot, dww, dwb, lng, lnb, g, sc, sh, rwt, rb, tm):
    bsz, s, d = x.shape
    cc = u.shape[2]
    nt = s // tm
    full = lambda shape: pl.BlockSpec(shape, lambda b, i: (0,) * len(shape))
    per_b = pl.BlockSpec((1, 1, d), lambda b, i: (b, 0, 0))
    halo = tm // CONV_HALO
    return pl.pallas_call(
        _hy_out_kernel,
        grid=(bsz, nt),
        in_specs=[pl.BlockSpec((1, tm, cc), lambda b, i: (b, i, 0)),
                  pl.BlockSpec((1, CONV_HALO, cc), lambda b, i: (b, jnp.maximum(i * halo - 1, 0), 0)),
                  pl.BlockSpec((1, ATT_WIDTH, tm), lambda b, i: (b, 0, i)),
                  pl.BlockSpec((1, tm, d), lambda b, i: (b, i, 0)),
                  per_b, full(wtop.shape), full(wbot.shape), full(dww.shape), full(dwb.shape),
                  full(lng.shape), full(lnb.shape), full(g.shape), per_b, per_b, full(rwt.shape), full(rb.shape)],
        out_specs=[pl.BlockSpec((1, tm, d), lambda b, i: (b, i, 0)),
                   pl.BlockSpec((tm, d + W_LANES), lambda b, i: (b * nt + i, 0)),
                   pl.BlockSpec((1, 1, tm), lambda b, i: (b * nt + i, 0, 0)),
                   pl.BlockSpec((CLASS_ROWS, 128), lambda b, i: (0, 0))],
        out_shape=[jax.ShapeDtypeStruct((bsz, s, d), F32),
                   jax.ShapeDtypeStruct((bsz * s, d + W_LANES), F32),
                   jax.ShapeDtypeStruct((bsz * nt, 1, tm), I32),
                   jax.ShapeDtypeStruct((CLASS_ROWS, 128), F32)],
        scratch_shapes=[pltpu.VMEM((tm + CONV_HALO, cc), F32),
                        pltpu.VMEM((7, tm + CONV_HALO - 8, cc), F32)],
        compiler_params=_params(2), name="hy_out_proj",
    )(u, u, att_t, x, g1, wtop, wbot, dww, dwb, lng, lnb, g, sc, sh, rwt, rb)


def _ssm_in_kernel(dest_ref, opaque_ref, ys_ref, xn_ref, g2_ref, sc_ref, sh_ref, g_ref, wz_ref, wx_ref, wdt_ref,
                   cw_ref, cb_ref, dtb_ref, x_ref, sz_ref, xbc_ref, dt_ref, buf_ref, sem, *cat_refs):
    tm = xn_ref.shape[1]
    ch = SSM_IN_COLS
    step = pl.program_id(0) * pl.num_programs(1) + pl.program_id(1)
    last = pl.num_programs(0) * pl.num_programs(1) - 1
    cur = step % 2
    ahead = jnp.minimum(step + 1, last)

    def gather(tile, slot, rows, after=0):
        for r in rows:
            row = dest_ref[tile * tm + r] + after
            pltpu.make_async_copy(ys_ref.at[pl.ds(row, 1), :], buf_ref.at[slot, pl.ds(r, 1), :],
                                  sem.at[slot]).start(priority=r % 2)

    def wait(slot):
        pltpu.make_async_copy(ys_ref.at[pl.ds(0, tm), :], buf_ref.at[slot], sem.at[slot]).wait()

    @pl.when(step == 0)
    def _():
        gather(0, 0, range(tm))

    @pl.when(pl.program_id(1) == 0)
    def _():
        for cat_ref in cat_refs:
            cat_ref[:SSM_HALO, :] = jnp.zeros((SSM_HALO, ch), F32)

    wait(cur)
    x = xn_ref[0] + g2_ref[0] * buf_ref[cur]
    x_ref[0] = x
    h = _mod_norm(x, g_ref[...], sc_ref[0], sh_ref[0])
    hb = h.astype(BF16)

    def conv_matmul(c):
        v = _dot(hb, wx_ref[:, c * ch:(c + 1) * ch])
        cat_refs[c][SSM_HALO:, :] = v
        return v

    def conv_act(c, _):
        cols = slice(c * ch, (c + 1) * ch)
        cat_ref = cat_refs[c]
        y = jnp.zeros((tm, ch), F32) + cb_ref[:, cols]
        for j in range(SSM_CONV):
            y = y + cw_ref[j:j + 1, cols] * cat_ref[pl.ds(SSM_HALO - SSM_CONV + 1 + j, tm), :]
        xbc_ref[0, :, cols] = _silu(y).astype(BF16)
        cat_ref[:SSM_HALO, :] = cat_ref[pl.ds(tm, SSM_HALO), :]

    def gate_matmul(c):
        return _dot(hb, wz_ref[:, c * ch:(c + 1) * ch])

    def gate_act(c, z):
        sz_ref[0, :, c * ch:(c + 1) * ch] = _silu(z).astype(BF16)

    stages = [(conv_matmul, conv_act, c) for c in range(wx_ref.shape[1] // ch)]
    stages += [(gate_matmul, gate_act, c) for c in range(wz_ref.shape[1] // ch)]
    shares = 8
    per_share = tm // shares
    gather(ahead, 1 - cur, range(per_share))
    pending = None
    for k, (matmul, act, c) in enumerate(stages):
        out = matmul(c)
        if k + 1 < shares:
            zero = out[0, 0].astype(I32) * opaque_ref[0]
            gather(ahead, 1 - cur, range((k + 1) * per_share, (k + 2) * per_share), zero)
        if pending is not None:
            pending[0](pending[1], pending[2])
        pending = (act, c, out)
    t = _dot(hb, wdt_ref[...]) + dtb_ref[...]
    pending[0](pending[1], pending[2])
    dt_ref[0] = jnp.maximum(t, 0.0) + jnp.log(1.0 + jnp.exp(-jnp.abs(t)))

    @pl.when(step == last)
    def _():
        wait(1 - cur)


def _ssm_in(dest, ys, x_new, g2, sc, sh, g, wz, wx, wdt, cw, cb, dtb, tm):
    bsz, s, d = x_new.shape
    full = lambda shape: pl.BlockSpec(shape, lambda b, i, *_: (0,) * len(shape), pipeline_mode=pl.Buffered(1))
    per_b = pl.BlockSpec((1, 1, d), lambda b, i, *_: (b, 0, 0))
    tile = lambda n: pl.BlockSpec((1, tm, n), lambda b, i, *_: (b, i, 0))
    return pl.pallas_call(
        _ssm_in_kernel,
        grid_spec=pltpu.PrefetchScalarGridSpec(
            num_scalar_prefetch=2, grid=(bsz, s // tm),
            in_specs=[pl.BlockSpec(memory_space=pl.ANY), tile(d), per_b, per_b, per_b, full(g.shape),
                      full(wz.shape), full(wx.shape), full(wdt.shape), full(cw.shape), full(cb.shape),
                      full(dtb.shape)],
            out_specs=[tile(d), tile(wz.shape[1]), tile(wx.shape[1]), tile(wdt.shape[1])],
            scratch_shapes=[pltpu.VMEM((2, tm, d), F32), pltpu.SemaphoreType.DMA((2,))]
            + [pltpu.VMEM((tm + SSM_HALO, SSM_IN_COLS), F32)] * (wx.shape[1] // SSM_IN_COLS)),
        out_shape=[jax.ShapeDtypeStruct((bsz, s, d), F32),
                   jax.ShapeDtypeStruct((bsz, s, wz.shape[1]), BF16),
                   jax.ShapeDtypeStruct((bsz, s, wx.shape[1]), BF16),
                   jax.ShapeDtypeStruct((bsz, s, wdt.shape[1]), F32)],
        compiler_params=_params(2), name="ssm_in_proj",
    )(dest, jnp.zeros((1,), I32), ys, x_new, g2, sc, sh, g, wz, wx, wdt, cw, cb, dtb)


def _ssd_kernel(xbc_ref, sz_ref, dt_ref, alog_ref, dsk_ref, ng_ref, wout_ref, e2_ref, x_ref, g1_ref,
                g_ref, sc_ref, sh_ref, rwt_ref, rb_ref,
                xn_ref, h2x_ref, cls_ref, cnt_ref, state_ref, macc_ref, y_ref):
    L = SSD_SUB
    inner = sz_ref.shape[2]
    gw = inner // SSM_GROUPS
    pw = 2 * SSM_HEAD_DIM

    @pl.when(pl.program_id(1) == 0)
    def _():
        state_ref[...] = jnp.zeros(state_ref.shape, F32)

    rows = x_ref.shape[1]
    subs = [slice(s * L, (s + 1) * L) for s in range(rows // L)]
    halves = [slice(s * SSM_CHUNK, (s + 1) * SSM_CHUNK) for s in range(rows // SSM_CHUNK)]
    brow = lax.broadcasted_iota(I32, (SSM_CHUNK, SSM_CHUNK), 0)
    bcol = lax.broadcasted_iota(I32, (SSM_CHUNK, SSM_CHUNK), 1)
    tri = jnp.where((brow // L == bcol // L) & (brow >= bcol), 1.0, 0.0).astype(BF16)
    lower = lax.broadcasted_iota(I32, (L, L), 0) >= lax.broadcasted_iota(I32, (L, L), 1)
    lo_half = lax.broadcasted_iota(I32, (L, pw), 1) < SSM_HEAD_DIM

    def expand(v):
        return _dot(jnp.concatenate(_split2(v), axis=1), e2_ref[...])

    macc_ref[...] = jnp.zeros(macc_ref.shape, F32)

    def group(g, carry):
        xcol = pl.ds(pl.multiple_of(g * gw, gw), gw)
        bcolumns = pl.ds(pl.multiple_of(inner + g * SSM_STATE, SSM_STATE), SSM_STATE)
        ccolumns = pl.ds(pl.multiple_of(inner + (SSM_GROUPS + g) * SSM_STATE, SSM_STATE), SSM_STATE)
        rate = -jnp.exp(alog_ref[g]) * LOG2E
        dt = dt_ref[0, :, pl.ds(pl.multiple_of(g * 128, 128), 128)]
        t3 = [_dot(tri, jnp.concatenate(_split3(dt[r, :] * rate), axis=1)) for r in halves]
        t3 = jnp.concatenate(t3, axis=0)
        a_cs = t3[:, :128] + t3[:, 128:256] + t3[:, 256:]
        xg = xbc_ref[0, :, xcol].astype(F32)
        bs = [xbc_ref[0, r, bcolumns] for r in subs]
        cs = [xbc_ref[0, r, ccolumns] for r in subs]
        cbm = [jnp.where(lower, _dot_nt(cm, bm), 0.0) for cm, bm in zip(cs, bs)]
        a_end = jnp.concatenate([jnp.broadcast_to(a_cs[r.stop - 1:r.stop, :], (L, 128)) for r in subs], axis=0)
        xw = (xg * expand(dt * jnp.exp2(a_end - a_cs))).astype(BF16)
        grow = expand(jnp.exp2(a_cs))
        keep = expand(jnp.exp2(jnp.concatenate([a_end[r.start:r.start + 16, :] for r in subs], axis=0)))
        st = state_ref[g]
        y_off = []
        for s, r in enumerate(subs):
            y_off.append(_dot(cs[s], st.astype(BF16)) * grow[r, :])
            st = st * keep[16 * s:16 * s + 1, :] + _dot(bs[s].astype(F32).T.astype(BF16), xw[r, :])
        state_ref[g] = st
        for s, r in enumerate(subs):
            a_sub = a_cs[r, :]
            a_sub_t = a_sub.T
            dt_sub_t = dt[r, :].T
            for q in range(gw // pw):
                x2 = xbc_ref[0, r, pl.ds(pl.multiple_of(g * gw + q * pw, pw), pw)]
                yp = y_off[s][:, q * pw:(q + 1) * pw]
                for e in range(2):
                    hd = 2 * q + e
                    dec = jnp.exp2(jnp.minimum(a_sub[:, hd:hd + 1] - a_sub_t[hd:hd + 1, :], 0.0))
                    xm = jnp.where(lo_half if e == 0 else jnp.logical_not(lo_half), x2, jnp.zeros_like(x2))
                    yp = yp + _dot((cbm[s] * dec * dt_sub_t[hd:hd + 1, :]).astype(BF16), xm)
                y_ref[r, q * pw:(q + 1) * pw] = yp
        gt = (y_ref[...] + xg * dsk_ref[g]) * sz_ref[0, :, xcol].astype(F32)
        ms = jnp.mean(gt * gt, axis=-1, keepdims=True)
        gn = gt * lax.rsqrt(ms + NORM_EPS) * ng_ref[g]
        macc_ref[...] += _dot(gn.astype(BF16), wout_ref[xcol, :])
        return carry

    lax.fori_loop(0, SSM_GROUPS, group, 0)
    x_new = x_ref[0] + g1_ref[0] * macc_ref[...]
    xn_ref[0] = x_new
    _post_mixer(x_new, g_ref, sc_ref, sh_ref, rwt_ref, rb_ref, h2x_ref, cls_ref, cnt_ref)


def _ssd(xbc, sz, dt, alog, dskip, ng, wout, e2, x, g1, g, sc, sh, rwt, rb):
    bsz, s, d = x.shape
    L = SSD_BLOCK
    nc = s // L
    gw = sz.shape[2] // SSM_GROUPS
    full = lambda shape: pl.BlockSpec(shape, lambda b, c: (0,) * len(shape))
    per_b = pl.BlockSpec((1, 1, d), lambda b, c: (b, 0, 0))
    tile = lambda n: pl.BlockSpec((1, L, n), lambda b, c: (b, c, 0))
    return pl.pallas_call(
        _ssd_kernel,
        grid=(bsz, nc),
        in_specs=[tile(xbc.shape[2]), tile(sz.shape[2]), tile(dt.shape[2]),
                  full(alog.shape), full(dskip.shape), full(ng.shape), full(wout.shape), full(e2.shape),
                  tile(d), per_b, full(g.shape), per_b, per_b, full(rwt.shape), full(rb.shape)],
        out_specs=[tile(d),
                   pl.BlockSpec((L, d + W_LANES), lambda b, c: (b * nc + c, 0)),
                   pl.BlockSpec((1, 1, L), lambda b, c: (b * nc + c, 0, 0)),
                   pl.BlockSpec((CLASS_ROWS, 128), lambda b, c: (0, 0))],
        out_shape=[jax.ShapeDtypeStruct((bsz, s, d), F32),
                   jax.ShapeDtypeStruct((bsz * s, d + W_LANES), F32),
                   jax.ShapeDtypeStruct((bsz * nc, 1, L), I32),
                   jax.ShapeDtypeStruct((CLASS_ROWS, 128), F32)],
        scratch_shapes=[pltpu.VMEM((SSM_GROUPS, SSM_STATE, gw), F32),
                        pltpu.VMEM((L, d), F32),
                        pltpu.VMEM((L, gw), F32)],
        compiler_params=_params(2), name="ssd_out_proj",
    )(xbc, sz, dt, alog, dskip, ng, wout, e2, x, g1, g, sc, sh, rwt, rb)


def _dest_kernel(cls_ref, start_ref, dest_ref, run_ref):
    @pl.when(pl.program_id(0) == 0)
    def _():
        run_ref[...] = start_ref[...]

    tr = cls_ref.shape[2]
    upper = lax.broadcasted_iota(I32, (tr, tr), 0) <= lax.broadcasted_iota(I32, (tr, tr), 1)
    upper = jnp.where(upper, 1.0, 0.0).astype(BF16)
    for k in range(cls_ref.shape[0]):
        onehot = lax.broadcasted_iota(I32, (CLASS_ROWS, tr), 0) == cls_ref[k]
        oh = jnp.where(onehot, 1.0, 0.0)
        prefix = _dot(oh.astype(BF16), upper)
        dest = jnp.sum(oh * (prefix - 1.0 + run_ref[...]), axis=0, keepdims=True)
        dest_ref[k] = dest.astype(I32)
        run_ref[...] += jnp.sum(oh, axis=1, keepdims=True)


def _moe_plan(cls, cnt, tr):
    nt = cls.shape[0]
    t = nt * tr
    reps = 4 if nt % 4 == 0 else 1
    padded = jnp.ceil(cnt[:, 0] / MOE_TILE) * MOE_TILE
    end = jnp.cumsum(padded)
    start = end - padded
    dest = pl.pallas_call(
        _dest_kernel, grid=(nt // reps,),
        in_specs=[pl.BlockSpec((reps, 1, tr), lambda i: (i, 0, 0)),
                  pl.BlockSpec((CLASS_ROWS, 1), lambda i: (0, 0))],
        out_specs=pl.BlockSpec((reps, 1, tr), lambda i: (i, 0, 0)),
        out_shape=jax.ShapeDtypeStruct((nt, 1, tr), I32),
        scratch_shapes=[pltpu.VMEM((CLASS_ROWS, 1), F32)],
        compiler_params=_params(1), name="moe_dest",
    )(cls, start.reshape(CLASS_ROWS, 1))
    n_tiles = t // MOE_TILE + N_CLASSES
    tile_row = jnp.arange(n_tiles, dtype=F32) * MOE_TILE
    total = end[N_CLASSES - 1]
    valid = tile_row < total
    tcls = jnp.sum((tile_row[:, None] >= end[None, :N_CLASSES]).astype(I32), axis=1)
    last = jnp.sum((total - MOE_TILE >= end[:N_CLASSES]).astype(I32))
    tcls = jnp.where(valid, tcls, last)
    grp = tcls // N_PAIRS
    pair = tcls % N_PAIRS
    lo = jnp.where(pair < 3, 0, jnp.where(pair < 5, 1, 2))
    hi = jnp.where(pair < 3, pair + 1, jnp.where(pair < 5, pair - 1, 3))
    meta = jnp.stack([grp * EXPERTS_PER_GROUP + lo, grp * EXPERTS_PER_GROUP + hi, valid.astype(I32),
                      jnp.zeros_like(tcls)]).astype(I32)
    return dest.reshape(t), meta


def _invert_kernel(dest_ref, zeros_ref, src_ref, sem):
    clear = pltpu.make_async_copy(zeros_ref, src_ref, sem)
    clear.start()
    clear.wait()

    def put(t0, carry):
        for u in range(INVERT_UNROLL):
            t = t0 * INVERT_UNROLL + u
            src_ref[dest_ref[t]] = t
        return carry

    lax.fori_loop(0, dest_ref.shape[0] // INVERT_UNROLL, put, 0)


def _invert(dest, n_slots):
    smem = pl.BlockSpec(memory_space=pltpu.SMEM)
    return pl.pallas_call(
        _invert_kernel, in_specs=[smem, pl.BlockSpec(memory_space=pl.ANY)], out_specs=smem,
        out_shape=jax.ShapeDtypeStruct((n_slots,), I32),
        scratch_shapes=[pltpu.SemaphoreType.DMA(())], name="moe_invert",
    )(dest, jnp.zeros((n_slots,), I32))


def _expert_kernel(meta_ref, src_ref, h2x_ref, wg_a_ref, wu_a_ref, wd_a_ref, wg_b_ref, wu_b_ref, wd_b_ref, y_ref,
                   buf0_ref, buf1_ref, sem):
    i = pl.program_id(0)
    d = y_ref.shape[1]
    bufs = (buf0_ref, buf1_ref)

    def gather(tile, slot, rows=range(MOE_TILE), after=0):
        for r in rows:
            tok = src_ref[tile * MOE_TILE + r] + after
            pltpu.make_async_copy(h2x_ref.at[pl.ds(tok, 1), :], bufs[slot].at[pl.ds(r, 1), :],
                                  sem.at[slot]).start(priority=r % 2)

    def wait(slot):
        pltpu.make_async_copy(h2x_ref.at[pl.ds(0, MOE_TILE), :], bufs[slot], sem.at[slot]).wait()

    @pl.when(i == 0)
    def _():
        gather(0, 0)

    valid = meta_ref[2, i] != 0
    cw = EXPERT_COLS
    for cur in range(2):
        @pl.when(valid & (i % 2 == cur))
        def _(cur=cur):
            wait(cur)
            xb = bufs[cur][:, :d].astype(BF16)
            wts = bufs[cur][:, d:]
            n_parts = 3 * (2 * wg_a_ref.shape[3] + d) // (2 * cw)
            bounds = [k * MOE_TILE // n_parts for k in range(n_parts + 1)]
            gather(i + 1, 1 - cur, range(bounds[1]))
            part = [1]

            def pace(result):
                if part[0] == n_parts:
                    return
                zero = result[0, 0].astype(I32) * meta_ref[3, i]
                gather(i + 1, 1 - cur, range(bounds[part[0]], bounds[part[0] + 1]), zero)
                part[0] += 1

            def dot_cols(a, w_ref):
                chunks = []
                for c in range(w_ref.shape[3] // cw):
                    chunks.append(_dot(a, w_ref[0, 0, :, c * cw:(c + 1) * cw]))
                    pace(chunks[-1])
                return jnp.concatenate(chunks, axis=1)

            y = jnp.zeros(y_ref.shape, F32)
            for e, (wg_ref, wu_ref, wd_ref) in enumerate(((wg_a_ref, wu_a_ref, wd_a_ref),
                                                          (wg_b_ref, wu_b_ref, wd_b_ref))):
                act = _silu(dot_cols(xb, wg_ref)) * dot_cols(xb, wu_ref)
                y = y + wts[:, e:e + 1] * dot_cols(act.astype(BF16), wd_ref)
            y_ref[...] = y

        @pl.when(jnp.logical_not(valid) & (meta_ref[2, jnp.maximum(i - 1, 0)] != 0) & (i % 2 == cur))
        def _(cur=cur):
            wait(cur)

    @pl.when(jnp.logical_not(valid))
    def _():
        y_ref[...] = jnp.zeros(y_ref.shape, F32)


def _experts(meta, src, h2x, layer, wg, wu, wd):
    n_slots = src.shape[0]
    w = h2x.shape[1]
    d = w - W_LANES
    n_tiles = n_slots // MOE_TILE
    expert = lambda arr, which: pl.BlockSpec((1, 1) + arr.shape[2:], lambda i, m, s: (layer, m[which, i], 0, 0))
    return pl.pallas_call(
        _expert_kernel,
        grid_spec=pltpu.PrefetchScalarGridSpec(
            num_scalar_prefetch=2, grid=(n_tiles,),
            in_specs=[pl.BlockSpec(memory_space=pl.ANY),
                      expert(wg, 0), expert(wu, 0), expert(wd, 0), expert(wg, 1), expert(wu, 1), expert(wd, 1)],
            out_specs=pl.BlockSpec((MOE_TILE, d), lambda i, m, s: (i, 0)),
            scratch_shapes=[pltpu.VMEM((MOE_TILE, w), F32), pltpu.VMEM((MOE_TILE, w), F32),
                            pltpu.SemaphoreType.DMA((2,))]),
        out_shape=jax.ShapeDtypeStruct((n_slots, d), F32),
        compiler_params=_params(1), name="moe_experts",
    )(meta, src, h2x, wg, wu, wd, wg, wu, wd)


def _combine_kernel(dest_ref, ys_ref, x_ref, g2_ref, o_ref, buf0_ref, buf1_ref, sem):
    tm = x_ref.shape[1]
    step = pl.program_id(0) * pl.num_programs(1) + pl.program_id(1)
    n_steps = pl.num_programs(0) * pl.num_programs(1)
    bufs = (buf0_ref, buf1_ref)

    def gather(st, slot):
        for r in range(tm):
            pltpu.make_async_copy(ys_ref.at[pl.ds(dest_ref[st * tm + r], 1), :], bufs[slot].at[pl.ds(r, 1), :],
                                  sem.at[slot]).start(priority=r % 2)

    @pl.when(step == 0)
    def _():
        gather(0, 0)

    for cur in range(2):
        @pl.when(step % 2 == cur)
        def _(cur=cur):
            @pl.when(step + 1 < n_steps)
            def _():
                gather(step + 1, 1 - cur)

            pltpu.make_async_copy(ys_ref.at[pl.ds(0, tm), :], bufs[cur], sem.at[cur]).wait()
            o_ref[0] = x_ref[0] + g2_ref[0] * bufs[cur][...]


def _combine(dest, ys, x, g2, tm):
    bsz, s, d = x.shape
    return pl.pallas_call(
        _combine_kernel,
        grid_spec=pltpu.PrefetchScalarGridSpec(
            num_scalar_prefetch=1, grid=(bsz, s // tm),
            in_specs=[pl.BlockSpec(memory_space=pl.ANY),
                      pl.BlockSpec((1, tm, d), lambda b, i, dd: (b, i, 0)),
                      pl.BlockSpec((1, 1, d), lambda b, i, dd: (b, 0, 0))],
            out_specs=pl.BlockSpec((1, tm, d), lambda b, i, dd: (b, i, 0)),
            scratch_shapes=[pltpu.VMEM((tm, d), F32), pltpu.VMEM((tm, d), F32), pltpu.SemaphoreType.DMA((2,))]),
        out_shape=jax.ShapeDtypeStruct((bsz, s, d), F32),
        compiler_params=_params(2), name="moe_combine",
    )(dest, ys, x, g2)


def _moe(h2x, cls, cnt, layer, wg, wu, wd):
    t = h2x.shape[0]
    tr = min(512, t)
    dest, meta = _moe_plan(cls.reshape(t // tr, 1, tr), cnt, tr)
    n_slots = t + N_CLASSES * MOE_TILE
    return dest, _experts(meta, _invert(dest, n_slots), h2x, layer, wg, wu, wd)


def kernel(x, c, ada_w, ada_b, norm_mix, norm_ffn, hy_w_in, hy_q_norm, hy_k_norm, hy_dw_w, hy_dw_b, hy_ln_g, hy_ln_b, hy_w_out, ssm_w_in, ssm_conv_w, ssm_conv_b, ssm_dt_bias, ssm_a_log, ssm_d, ssm_norm, ssm_w_out, router_w, router_bias, exp_w_gate, exp_w_up, exp_w_down):
    bsz, s, d = x.shape
    depth = ada_w.shape[0]
    assert s % 512 == 0 and bsz <= 8
    c8 = jnp.zeros((8, d), F32).at[:bsz].set(c)
    mod = _ada(c8, ada_w, ada_b)
    rwt = router_w.T
    rb = router_bias.reshape(N_EXPERTS, 1)
    lane_head = jnp.arange(ATT_WIDTH) // HEAD_DIM
    pmat = ((lane_head[:, None] == lane_head[None, :]).astype(F32) / HEAD_DIM).astype(BF16)

    expert_w = tuple(w.astype(BF16) for w in (exp_w_gate, exp_w_up, exp_w_down))
    moe = None
    for layer in range(depth):
        sh1, sc1, g1, sh2, sc2, g2 = (mod[layer, :bsz, i * d:(i + 1) * d].reshape(bsz, 1, d) for i in range(N_MOD))
        gm = norm_mix[layer].reshape(1, d)
        gf = norm_ffn[layer].reshape(1, d)
        j = layer // 2
        if layer % 2 == 0:
            if moe is not None:
                x = _combine(*moe, min(256, s))
            w_in = hy_w_in[j].astype(BF16)
            aw = ATT_WIDTH
            k, qt, vt, u, kmean = _hy_in(
                x, sc1, sh1, gm, w_in[:, aw:2 * aw], w_in[:, 3 * aw:], w_in[:, :aw].T, w_in[:, 2 * aw:3 * aw].T,
                (jnp.tile(hy_q_norm[j], ATT_HEADS) * (HEAD_DIM ** -0.5 * LOG2E)).reshape(aw, 1),
                jnp.tile(hy_k_norm[j], ATT_HEADS).reshape(1, aw), pmat, min(1024, s))
            bias = _gate(kmean.reshape(bsz, s // MOBA_BLOCK, aw), qt, min(2048, s))
            att_t = _attn(qt, k, vt, bias, 8)
            w_out = hy_w_out[j].astype(BF16)
            dww = jnp.zeros((CONV_HALO, hy_dw_w.shape[2]), F32).at[:CONV_WIDTH].set(hy_dw_w[j])
            x_new, h2x, cls, cnt = _hy_out(
                u, att_t, x, g1, w_out[:aw], w_out[aw:], dww, hy_dw_b[j].reshape(1, -1),
                hy_ln_g[j].reshape(1, -1), hy_ln_b[j].reshape(1, -1), gf, sc2, sh2, rwt, rb, 512)
        else:
            w_in = ssm_w_in[j]
            inner = ssm_norm.shape[1]
            heads = ssm_a_log.shape[1]
            hpg = heads // SSM_GROUPS
            conv_dim = ssm_conv_w.shape[2]

            def by_group(v):
                v = v.reshape(v.shape[:-1] + (SSM_GROUPS, hpg))
                pad = [(0, 0)] * (v.ndim - 1) + [(0, 128 - hpg)]
                return jnp.pad(v, pad).reshape(v.shape[:-2] + (SSM_GROUPS * 128,))

            x, sz, xbc, dt = _ssm_in(
                *moe, sc1, sh1, gm, w_in[:, :inner].astype(BF16), w_in[:, inner:inner + conv_dim].astype(BF16),
                by_group(w_in[:, inner + conv_dim:]).astype(BF16), ssm_conv_w[j], ssm_conv_b[j].reshape(1, -1),
                by_group(ssm_dt_bias[j]).reshape(1, -1), 512)
            chan_head = jnp.arange(inner // SSM_GROUPS) // SSM_HEAD_DIM
            e2 = (jnp.arange(256)[:, None] % 128 == chan_head[None, :]).astype(BF16)
            x_new, h2x, cls, cnt = _ssd(
                xbc, sz, dt, by_group(ssm_a_log[j]).reshape(SSM_GROUPS, 1, 128),
                jnp.repeat(ssm_d[j], SSM_HEAD_DIM).reshape(SSM_GROUPS, 1, -1),
                ssm_norm[j].reshape(SSM_GROUPS, 1, -1), ssm_w_out[j].astype(BF16), e2, x, g1, gf, sc2, sh2, rwt, rb)
        moe = _moe(h2x, cls, cnt, layer, *expert_w) + (x_new, g2)
    return _combine(*moe, min(256, s))
```
